```python
import math
import jax, jax.numpy as jnp
from jax import lax
import numpy as np

D_MODEL = 2048
BATCH = 8
SEQ = 2048
DEPTH = 1

PLE_DIM = 256
A_HEADS = 8
A_QK_DIM = 64
A_V_DIM = 2 * A_QK_DIM
A_WIDTH = A_HEADS * A_V_DIM
Q_BLOCK = 128
R_HEADS = 16
R_HEAD = 64
R_WIDTH = R_HEADS * R_HEAD
DECAY_LORA = 64
ICLR_LORA = 64
GATE_LORA = 160
N_EXPERTS = 64
TOP_K = 8
EXPERT_FF = 512
SHARED_FF = 512
ROUTED_SCALE = 2.5
MOE_BLOCK = 128
RMS_EPS = 1e-6
GN_EPS = 64e-5
Q_COLS = A_HEADS * 2 * A_QK_DIM
A_COLS = 2 * Q_COLS + A_WIDTH
R_COLS = 3 * R_WIDTH + 2 * DECAY_LORA + 2 * ICLR_LORA + GATE_LORA
IN_COLS = A_COLS + R_COLS

kernel_name = "hybrid_diffattn_rwkv7_moe_block"


def rms_norm(x, g):
    xf = x.astype(jnp.float32)
    y = xf * lax.rsqrt(jnp.mean(xf * xf, axis=-1, keepdims=True) + RMS_EPS)
    return (y * g.astype(jnp.float32)).astype(x.dtype)


def alibi_slopes(n_heads):
    h = jnp.arange(1, n_heads + 1, dtype=jnp.float32)
    return jnp.exp2(-8.0 * h / n_heads)


def diff_attention(q, k, v, lam, subln_g, lambda_init):
    B, T, H, _, dk = q.shape
    dv = v.shape[-1]
    nq = T // Q_BLOCK
    scale = dk ** -0.5
    qb = q.reshape(B, nq, Q_BLOCK, H, 2, dk).transpose(1, 4, 0, 3, 2, 5)
    kt = k.transpose(3, 0, 2, 1, 4)
    vt = v.transpose(0, 2, 1, 3)
    slopes = alibi_slopes(H)
    kpos = jnp.arange(T)

    def one_block(args):
        qblk, blk = args
        qpos = blk * Q_BLOCK + jnp.arange(Q_BLOCK)
        dist = jnp.abs(qpos[:, None] - kpos[None, :]).astype(jnp.float32)
        bias = -slopes[:, None, None] * dist
        s = jnp.einsum('mbhqd,mbhkd->mbhqk', qblk, kt).astype(jnp.float32) * scale + bias
        prob = jax.nn.softmax(s, axis=-1)
        attn = prob[0] - lam * prob[1]
        return jnp.einsum('bhqk,bhkd->bhqd', attn.astype(vt.dtype), vt)

    o = lax.map(one_block, (qb, jnp.arange(nq)))
    o = o.transpose(1, 0, 3, 2, 4).reshape(B, T, H, dv)
    o = rms_norm(o, subln_g) * (1.0 - lambda_init)
    return o.reshape(B, T, H * dv)


def centred_shift(z, mu_prev, mu_next):
    zp = jnp.pad(z[:, :-1], ((0, 0), (1, 0), (0, 0)))
    zn = jnp.pad(z[:, 1:], ((0, 0), (0, 1), (0, 0)))
    return z + mu_prev * (zp - z) + mu_next * (zn - z)


def rwkv7_step(S, inp):
    r, w, kk, b, k, v = inp
    sa = jnp.einsum('dbhij,dbhj->dbhi', S, kk)
    S = S * w[..., None, :] - sa[..., :, None] * b[..., None, :] + v[..., :, None] * k[..., None, :]
    y = jnp.einsum('dbhij,dbhj->dbhi', S, r)
    return S, y


def rwkv7_bidir(z, w0, w_decay_up, a0, w_iclr_up, w_gate_up, k_k, k_a, r_k, ln_x_g, ln_x_b):
    B, T, _ = z.shape
    C, H, N = R_WIDTH, R_HEADS, R_HEAD
    f32 = jnp.float32
    o3 = 3 * C
    o4 = o3 + 2 * DECAY_LORA
    o5 = o4 + 2 * ICLR_LORA
    r = z[..., :C]
    k = z[..., C:2 * C]
    v = z[..., 2 * C:o3]
    lw = z[..., o3:o4].reshape(B, T, 2, DECAY_LORA)
    la = z[..., o4:o5].reshape(B, T, 2, ICLR_LORA)
    lg = z[..., o5:]
    zw = jnp.einsum('btdr,drc->dbtc', jnp.tanh(lw), w_decay_up) + w0[:, None, None, :]
    w = jnp.exp(-math.exp(-0.5) * jax.nn.sigmoid(zw.astype(f32)))
    a = jax.nn.sigmoid((jnp.einsum('btdr,drc->dbtc', la, w_iclr_up) + a0[:, None, None, :]).astype(f32))
    g = jax.nn.sigmoid(lg) @ w_gate_up
    kk = (k * k_k).astype(f32).reshape(B, T, H, N)
    kk = kk * lax.rsqrt(jnp.maximum(jnp.sum(kk * kk, axis=-1, keepdims=True), 1e-12))
    kk = kk.reshape(B, T, C)
    kt = k.astype(f32)[None] * (1.0 + (a - 1.0) * k_a.astype(f32))
    b = kk[None] * a

    def to_scan(u):
        u = jnp.broadcast_to(u, (2, B, T, C))
        u = jnp.stack([u[0], jnp.flip(u[1], axis=1)]).astype(f32)
        return u.reshape(2, B, T, H, N).transpose(2, 0, 1, 3, 4)

    xs = (to_scan(r), to_scan(w), to_scan(kk), to_scan(b), to_scan(kt), to_scan(v))
    s0 = jnp.zeros((2, B, H, N, N), f32)
    _, ys = lax.scan(rwkv7_step, s0, xs)
    ys = ys.transpose(1, 2, 0, 3, 4)
    y = ys[0] + jnp.flip(ys[1], axis=1)
    mu = jnp.mean(y, axis=-1, keepdims=True)
    var = jnp.mean(jnp.square(y - mu), axis=-1, keepdims=True)
    yn = ((y - mu) * lax.rsqrt(var + GN_EPS)).reshape(B, T, C) * ln_x_g + ln_x_b
    rh = r.astype(f32).reshape(B, T, H, N)
    coef = jnp.sum(rh[None] * kt.reshape(2, B, T, H, N) * r_k.astype(f32), axis=(0, 4))
    bonus = (coef[..., None] * v.astype(f32).reshape(B, T, H, N)).reshape(B, T, C)
    return ((yn + bonus) * g.astype(f32)).astype(z.dtype)


def moe_ffn(xf, w_router, b_router, w_e_gate, w_e_up, w_e_down, w_s_gate, w_s_up, w_s_down):
    n_tok, d = xf.shape
    scores = jax.nn.sigmoid((xf @ w_router).astype(jnp.float32))
    _, idx = lax.top_k(scores + b_router.astype(jnp.float32), TOP_K)
    gsel = jnp.take_along_axis(scores, idx, axis=-1)
    gsel = gsel / jnp.sum(gsel, axis=-1, keepdims=True) * ROUTED_SCALE
    m = n_tok * TOP_K
    e_flat = idx.reshape(m)
    tok_flat = jnp.repeat(jnp.arange(n_tok, dtype=jnp.int32), TOP_K)
    g_flat = gsel.reshape(m)
    order = jnp.argsort(e_flat)
    e_s, tok_s, g_s = e_flat[order], tok_flat[order], g_flat[order]
    counts = jnp.bincount(e_flat, length=N_EXPERTS)
    starts = jnp.cumsum(counts) - counts
    padded = (counts + MOE_BLOCK - 1) // MOE_BLOCK * MOE_BLOCK
    pends = jnp.cumsum(padded)
    pstarts = pends - padded
    dest = pstarts[e_s] + jnp.arange(m) - starts[e_s]
    n_blocks = -(-m // MOE_BLOCK) + N_EXPERTS
    cap = n_blocks * MOE_BLOCK
    slot_tok = jnp.zeros((cap,), jnp.int32).at[dest].set(tok_s)
    slot_g = jnp.zeros((cap,), jnp.float32).at[dest].set(g_s)
    block_e = jnp.minimum(jnp.searchsorted(pends, jnp.arange(n_blocks) * MOE_BLOCK, side='right'),
                          N_EXPERTS - 1)

    def expert_block(args):
        tok, gb, e = args
        xb = xf[tok]
        hb = jax.nn.silu(xb @ w_e_gate[e]) * (xb @ w_e_up[e])
        return (hb @ w_e_down[e]) * gb[:, None].astype(xf.dtype)

    yb = lax.map(expert_block, (slot_tok.reshape(n_blocks, MOE_BLOCK),
                                slot_g.reshape(n_blocks, MOE_BLOCK), block_e))
    routed = jax.ops.segment_sum(yb.reshape(cap, d), slot_tok, num_segments=n_tok)
    shared = (jax.nn.silu(xf @ w_s_gate) * (xf @ w_s_up)) @ w_s_down
    return routed + shared


def setup_inputs(seed: int = 0) -> dict:
    key = jax.random.key(seed)
    ks = iter(jax.random.split(key, 48))
    f32 = jnp.float32
    L, D = DEPTH, D_MODEL

    def nrm(shape, scale):
        return jax.random.normal(next(ks), shape, f32) * scale

    def gain(shape):
        return 1.0 + 0.02 * jax.random.normal(next(ks), shape, f32)

    def unif(shape, lo, hi):
        return jax.random.uniform(next(ks), shape, f32, lo, hi)

    return {
        "x": nrm((BATCH, SEQ, D), 1.0),
        "p": nrm((L, BATCH, SEQ, PLE_DIM), 1.0),
        "norm_mix_g": gain((L, D)),
        "w_in": nrm((L, D, IN_COLS), D ** -0.5),
        "w_branch_gate": nrm((L, D, 2 * D), D ** -0.5),
        "lambda_q1": nrm((L, A_QK_DIM), 0.1),
        "lambda_k1": nrm((L, A_QK_DIM), 0.1),
        "lambda_q2": nrm((L, A_QK_DIM), 0.1),
        "lambda_k2": nrm((L, A_QK_DIM), 0.1),
        "subln_g": gain((L, A_V_DIM)),
        "shift_prev": unif((L, R_COLS), 0.0, 0.5),
        "shift_next": unif((L, R_COLS), 0.0, 0.5),
        "w0": unif((L, 2, R_WIDTH), -4.0, 2.0),
        "w_decay_up": nrm((L, 2, DECAY_LORA, R_WIDTH), 0.5 * DECAY_LORA ** -0.5),
        "a0": nrm((L, 2, R_WIDTH), 0.5),
        "w_iclr_up": nrm((L, 2, ICLR_LORA, R_WIDTH), 0.5 * ICLR_LORA ** -0.5),
        "w_gate_up": nrm((L, GATE_LORA, R_WIDTH), GATE_LORA ** -0.5),
        "k_k": 0.85 + nrm((L, R_WIDTH), 0.1),
        "k_a": 1.0 + nrm((L, R_WIDTH), 0.1),
        "r_k": nrm((L, R_HEADS, R_HEAD), 0.1),
        "ln_x_g": gain((L, R_WIDTH)),
        "ln_x_b": nrm((L, R_WIDTH), 0.02),
        "w_br_attn": nrm((L, A_WIDTH, D), A_WIDTH ** -0.5),
        "w_br_rwkv": nrm((L, R_WIDTH, D), R_WIDTH ** -0.5),
        "w_out": nrm((L, D, D), D ** -0.5),
        "norm_ffn_g": gain((L, D)),
        "w_router": nrm((L, D, N_EXPERTS), D ** -0.5),
        "b_router": nrm((L, N_EXPERTS), 0.01),
        "w_e_gate": nrm((L, N_EXPERTS, D, EXPERT_FF), D ** -0.5),
        "w_e_up": nrm((L, N_EXPERTS, D, EXPERT_FF), D ** -0.5),
        "w_e_down": nrm((L, N_EXPERTS, EXPERT_FF, D), EXPERT_FF ** -0.5),
        "w_s_gate": nrm((L, D, SHARED_FF), D ** -0.5),
        "w_s_up": nrm((L, D, SHARED_FF), D ** -0.5),
        "w_s_down": nrm((L, SHARED_FF, D), SHARED_FF ** -0.5),
        "norm_ple_g": gain((L, D)),
        "w_ple_gate": nrm((L, D, D), D ** -0.5),
        "w_ple_proj": nrm((L, PLE_DIM, D), PLE_DIM ** -0.5),
        "norm_final_g": gain((D,)),
    }


def reference(x, p, norm_mix_g, w_in, w_branch_gate, lambda_q1, lambda_k1, lambda_q2, lambda_k2,
              subln_g, shift_prev, shift_next, w0, w_decay_up, a0, w_iclr_up, w_gate_up, k_k, k_a,
              r_k, ln_x_g, ln_x_b, w_br_attn, w_br_rwkv, w_out, norm_ffn_g, w_router, b_router,
              w_e_gate, w_e_up, w_e_down, w_s_gate, w_s_up, w_s_down, norm_ple_g, w_ple_gate,
              w_ple_proj, norm_final_g):
    B, T, D = x.shape
    f32 = jnp.float32
    h = x
    for i in range(DEPTH):
        lambda_init = 0.8 - 0.6 * math.exp(-0.3 * i)
        u = rms_norm(h, norm_mix_g[i])
        z = u @ w_in[i]
        za, zr = z[..., :A_COLS], z[..., A_COLS:]
        q = za[..., :Q_COLS].reshape(B, T, A_HEADS, 2, A_QK_DIM)
        k = za[..., Q_COLS:2 * Q_COLS].reshape(B, T, A_HEADS, 2, A_QK_DIM)
        v = za[..., 2 * Q_COLS:].reshape(B, T, A_HEADS, A_V_DIM)
        lam = (jnp.exp(jnp.sum(lambda_q1[i].astype(f32) * lambda_k1[i].astype(f32)))
               - jnp.exp(jnp.sum(lambda_q2[i].astype(f32) * lambda_k2[i].astype(f32)))
               + lambda_init)
        ya = diff_attention(q, k, v, lam, subln_g[i], lambda_init)
        zr = centred_shift(zr, shift_prev[i], shift_next[i])
        yr = rwkv7_bidir(zr, w0[i], w_decay_up[i], a0[i], w_iclr_up[i], w_gate_up[i],
                         k_k[i], k_a[i], r_k[i], ln_x_g[i], ln_x_b[i])
        gates = jax.nn.sigmoid(u @ w_branch_gate[i])
        merged = gates[..., :D] * (ya @ w_br_attn[i]) + gates[..., D:] * (yr @ w_br_rwkv[i])
        h = h + merged @ w_out[i]
        u = rms_norm(h, norm_ffn_g[i])
        y_ffn = moe_ffn(u.reshape(B * T, D), w_router[i], b_router[i], w_e_gate[i], w_e_up[i],
                        w_e_down[i], w_s_gate[i], w_s_up[i], w_s_down[i])
        h = h + y_ffn.reshape(B, T, D)
        u = rms_norm(h, norm_ple_g[i])
        h = h + jax.nn.sigmoid(u @ w_ple_gate[i]) * (p[i] @ w_ple_proj[i])
    return rms_norm(h, norm_final_g)
```

```python
import functools
import math

import jax
import jax.numpy as jnp
from jax import lax
from jax.experimental import pallas as pl
from jax.experimental.pallas import tpu as pltpu

F32 = jnp.float32
BF16 = jnp.bfloat16
HI = lax.Precision.HIGHEST

D_MODEL = 2048
PLE_DIM = 256
A_HEADS = 8
A_QK_DIM = 64
A_V_DIM = 2 * A_QK_DIM
A_WIDTH = A_HEADS * A_V_DIM
R_HEADS = 16
R_HEAD = 64
R_WIDTH = R_HEADS * R_HEAD
DECAY_LORA = 64
ICLR_LORA = 64
GATE_LORA = 160
N_EXPERTS = 64
TOP_K = 8
EXPERT_FF = 512
SHARED_FF = 512
ROUTED_SCALE = 2.5
RMS_EPS = 1e-6
GN_EPS = 64e-5
Q_COLS = A_HEADS * 2 * A_QK_DIM
A_COLS = 2 * Q_COLS + A_WIDTH
R_COLS = 3 * R_WIDTH + 2 * DECAY_LORA + 2 * ICLR_LORA + GATE_LORA
R_COLS_PAD = 3584
LANES = 128
CHUNK = 64
EXPERT_BLOCK = 256
VMEM_LIMIT = 56 * 1024 * 1024


def _params(sem):
    return pltpu.CompilerParams(dimension_semantics=sem, vmem_limit_bytes=VMEM_LIMIT)


def _sigmoid(x):
    return 1.0 / (1.0 + jnp.exp(-x))


def _rmsnorm_kernel(x_ref, g_ref, o_ref):
    x = x_ref[...].astype(F32)
    ms = jnp.mean(x * x, axis=-1, keepdims=True)
    o_ref[...] = (x * lax.rsqrt(ms + RMS_EPS) * g_ref[...]).astype(o_ref.dtype)


def rmsnorm(x, g, out_dtype, tm=512):
    m, d = x.shape
    tm = min(tm, m)
    return pl.pallas_call(
        _rmsnorm_kernel,
        grid=(m // tm,),
        in_specs=[pl.BlockSpec((tm, d), lambda i: (i, 0)),
                  pl.BlockSpec((1, d), lambda i: (0, 0))],
        out_specs=pl.BlockSpec((tm, d), lambda i: (i, 0)),
        out_shape=jax.ShapeDtypeStruct((m, d), out_dtype),
        compiler_params=_params(("parallel",)),
        name="rmsnorm",
    )(x, g.reshape(1, d).astype(F32))


def _mm_kernel(x_ref, w_ref, *rest, epilogue):
    o_ref = rest[-1]
    acc = jnp.dot(x_ref[...], w_ref[...], preferred_element_type=F32)
    if epilogue is not None:
        acc = epilogue(acc, *[e[...] for e in rest[:-1]])
    o_ref[...] = acc.astype(o_ref.dtype)


def matmul(x, w, *, out_dtype, epilogue=None, extras=(), tm=512, tn=512, name="matmul"):
    m, k = x.shape
    n = w.shape[1]
    tm = min(tm, m)
    tn = min(tn, n)
    in_specs = [pl.BlockSpec((tm, k), lambda i, j: (i, 0)),
                pl.BlockSpec((k, tn), lambda i, j: (0, j))]
    in_specs += [pl.BlockSpec((tm, tn), lambda i, j: (i, j)) for _ in extras]
    return pl.pallas_call(
        functools.partial(_mm_kernel, epilogue=epilogue),
        grid=(m // tm, pl.cdiv(n, tn)),
        in_specs=in_specs,
        out_specs=pl.BlockSpec((tm, tn), lambda i, j: (i, j)),
        out_shape=jax.ShapeDtypeStruct((m, n), out_dtype),
        compiler_params=_params(("parallel", "parallel")),
        name=name,
    )(x, w, *extras)


def _residual_add(acc, res):
    return res + acc


def _sigmoid_epilogue(acc):
    return _sigmoid(acc)


def _attn_kernel(slopes_ref, q_ref, k_ref, v_ref, lq1_ref, lk1_ref, lq2_ref, lk2_ref, sg_ref,
                 o_ref, *, tq, seq, lambda_init):
    h = pl.program_id(1)
    qi = pl.program_id(2)
    lane = lax.broadcasted_iota(jnp.int32, (tq, LANES), 1)
    q = q_ref[...] * jnp.asarray(A_QK_DIM ** -0.5, BF16)
    zero = jnp.zeros_like(q)
    q0 = jnp.where(lane < A_QK_DIM, q, zero)
    q1 = jnp.where(lane < A_QK_DIM, zero, q)
    k = k_ref[...]
    v = v_ref[...]
    dims = (((1,), (1,)), ((), ()))
    qpos = qi * tq + lax.broadcasted_iota(jnp.int32, (tq, seq), 0)
    kpos = lax.broadcasted_iota(jnp.int32, (tq, seq), 1)
    bias = -slopes_ref[h] * jnp.abs(qpos - kpos).astype(F32)

    def one_map(qm):
        s = lax.dot_general(qm, k, dims, preferred_element_type=F32) + bias
        e = jnp.exp(s - jnp.max(s, axis=-1, keepdims=True))
        l = jnp.sum(e, axis=-1, keepdims=True)
        return jnp.dot(e.astype(BF16), v, preferred_element_type=F32) / l

    lam = (jnp.exp(jnp.sum(lq1_ref[...] * lk1_ref[...], axis=-1, keepdims=True))
           - jnp.exp(jnp.sum(lq2_ref[...] * lk2_ref[...], axis=-1, keepdims=True))
           + lambda_init)
    o = one_map(q0) - lam * one_map(q1)
    ms = jnp.mean(o * o, axis=-1, keepdims=True)
    o = o * lax.rsqrt(ms + RMS_EPS) * sg_ref[...]
    o_ref[...] = (o * (1.0 - lambda_init)).astype(o_ref.dtype)


def diff_attention(za, lq1, lk1, lq2, lk2, subln_g, lambda_init, tq=256):
    b, t, _ = za.shape
    tq = min(tq, t)
    h = jnp.arange(1, A_HEADS + 1, dtype=F32)
    slopes = jnp.exp2(-8.0 * h / A_HEADS)
    vec = lambda a: a.reshape(1, -1).astype(F32)
    small = lambda n: pl.BlockSpec((1, n), lambda bi, hi, qi, s: (0, 0))
    grid_spec = pltpu.PrefetchScalarGridSpec(
        num_scalar_prefetch=1,
        grid=(b, A_HEADS, t // tq),
        in_specs=[
            pl.BlockSpec((None, tq, LANES), lambda bi, hi, qi, s: (bi, qi, hi)),
            pl.BlockSpec((None, t, LANES), lambda bi, hi, qi, s: (bi, 0, A_HEADS + hi)),
            pl.BlockSpec((None, t, LANES), lambda bi, hi, qi, s: (bi, 0, 2 * A_HEADS + hi)),
            small(A_QK_DIM), small(A_QK_DIM), small(A_QK_DIM), small(A_QK_DIM), small(A_V_DIM),
        ],
        out_specs=pl.BlockSpec((None, tq, LANES), lambda bi, hi, qi, s: (bi, qi, hi)),
    )
    return pl.pallas_call(
        functools.partial(_attn_kernel, tq=tq, seq=t, lambda_init=lambda_init),
        grid_spec=grid_spec,
        out_shape=jax.ShapeDtypeStruct((b, t, A_WIDTH), BF16),
        compiler_params=_params(("parallel", "parallel", "parallel")),
        name="diff_attention",
    )(slopes, za, za, za, vec(lq1), vec(lk1), vec(lq2), vec(lk2), vec(subln_g))


def _head_sum(x):
    ri = lax.broadcasted_iota(jnp.int32, (LANES, LANES), 0) // R_HEAD
    ci = lax.broadcasted_iota(jnp.int32, (LANES, LANES), 1) // R_HEAD
    ones = (ri == ci).astype(F32)
    return jnp.dot(x, ones, precision=HI, preferred_element_type=F32)


def _head_sum_wide(x):
    return jnp.concatenate(
        [_head_sum(x[:, p * LANES:(p + 1) * LANES]) for p in range(x.shape[1] // LANES)], axis=1)


def _rwkv_prep_kernel(z_ref, zprev_ref, znext_ref, mup_ref, mun_ref, w0_ref, wdu_ref, a0_ref,
                      wiu_ref, wgu_ref, kk_ref, ka_ref, rk_ref,
                      r_out, kkn_out, v_out, lw_out, b_out, kt_out, bonus_out, g_out, *, tt):
    i = pl.program_id(1)
    n_i = pl.num_programs(1)
    c = R_WIDTH
    z = z_ref[...]
    row = lax.broadcasted_iota(jnp.int32, (tt, 1), 0)
    prev_row = jnp.where(i > 0, zprev_ref[7:8, :], 0.0)
    next_row = jnp.where(i < n_i - 1, znext_ref[0:1, :], 0.0)
    zp = jnp.where(row == 0, prev_row, pltpu.roll(z, 1, 0))
    zn = jnp.where(row == tt - 1, next_row, pltpu.roll(z, tt - 1, 0))
    zs = z + mup_ref[...] * (zp - z) + mun_ref[...] * (zn - z)

    r = zs[:, :c]
    k = zs[:, c:2 * c]
    v = zs[:, 2 * c:3 * c]
    o3 = 3 * c
    lane = lax.broadcasted_iota(jnp.int32, (tt, LANES), 1)
    first = lane < DECAY_LORA
    lw = jnp.tanh(zs[:, o3:o3 + LANES])
    la = zs[:, o3 + LANES:o3 + 2 * LANES]
    lg = _sigmoid(zs[:, o3 + 2 * LANES:o3 + 4 * LANES])
    g_out[...] = jnp.dot(lg.astype(BF16), wgu_ref[...], preferred_element_type=F32)

    kk = k * kk_ref[...]
    ss = _head_sum_wide(kk * kk)
    kk = kk * lax.rsqrt(jnp.maximum(ss, 1e-12))
    r_out[...] = r
    kkn_out[...] = kk
    v_out[...] = v

    kt_sum = jnp.zeros_like(k)
    for d in range(2):
        keep = first if d == 0 else jnp.logical_not(first)
        lw_d = jnp.where(keep, lw, 0.0).astype(BF16)
        la_d = jnp.where(keep, la, 0.0).astype(BF16)
        zw = jnp.dot(lw_d, wdu_ref[...], preferred_element_type=F32) + w0_ref[d:d + 1, :]
        lw_out[d] = -math.exp(-0.5) * _sigmoid(zw)
        a = _sigmoid(jnp.dot(la_d, wiu_ref[...], preferred_element_type=F32) + a0_ref[d:d + 1, :])
        kt = k * (1.0 + (a - 1.0) * ka_ref[...])
        kt_out[d] = kt
        b_out[d] = kk * a
        kt_sum = kt_sum + kt
    coef = _head_sum_wide(r * kt_sum * rk_ref[...])
    bonus_out[...] = coef * v


def rwkv_prep(zr, shift_prev, shift_next, w0, w_decay_up, a0, w_iclr_up, w_gate_up, k_k, k_a, r_k,
              tt=128):
    b, t, cp = zr.shape
    tt = min(tt, t)
    c = R_WIDTH
    pad = cp - R_COLS
    row = lambda a: a.reshape(1, -1).astype(F32)
    mup = jnp.pad(row(shift_prev), ((0, 0), (0, pad)))
    mun = jnp.pad(row(shift_next), ((0, 0), (0, pad)))
    wdu = w_decay_up.reshape(2 * DECAY_LORA, c).astype(BF16)
    wiu = w_iclr_up.reshape(2 * ICLR_LORA, c).astype(BF16)
    wgu = jnp.pad(w_gate_up, ((0, 2 * LANES - GATE_LORA), (0, 0))).astype(BF16)
    nb8 = t // 8
    const = lambda shape: pl.BlockSpec(shape, lambda bi, i: (0,) * len(shape))
    in_specs = [
        pl.BlockSpec((None, tt, cp), lambda bi, i: (bi, i, 0)),
        pl.BlockSpec((None, 8, cp), lambda bi, i: (bi, jnp.maximum(i * (tt // 8) - 1, 0), 0)),
        pl.BlockSpec((None, 8, cp), lambda bi, i: (bi, jnp.minimum((i + 1) * (tt // 8), nb8 - 1), 0)),
        const((1, cp)), const((1, cp)), const((2, c)), const((2 * DECAY_LORA, c)), const((2, c)),
        const((2 * ICLR_LORA, c)), const((2 * LANES, c)), const((1, c)), const((1, c)), const((1, c)),
    ]
    one = pl.BlockSpec((None, tt, c), lambda bi, i: (bi, i, 0))
    two = pl.BlockSpec((2, None, tt, c), lambda bi, i: (0, bi, i, 0))
    s1 = jax.ShapeDtypeStruct((b, t, c), F32)
    s2 = jax.ShapeDtypeStruct((2, b, t, c), F32)
    return pl.pallas_call(
        functools.partial(_rwkv_prep_kernel, tt=tt),
        grid=(b, t // tt),
        in_specs=in_specs,
        out_specs=[one, one, one, two, two, two, one, one],
        out_shape=[s1, s1, s1, s2, s2, s2, s1, s1],
        compiler_params=_params(("parallel", "parallel")),
        name="rwkv_prep",
    )(zr, zr, zr, mup, mun, w0.astype(F32), wdu, a0.astype(F32), wiu, wgu, row(k_k), row(k_a), row(r_k))


def _scan_kernel(r_ref, kk_ref, v_ref, lw_ref, b_ref, kt_ref, y_ref, s_ref, *, chunk):
    d = pl.program_id(0)
    cidx = pl.program_id(2)
    ln = chunk

    @pl.when(cidx == 0)
    def _():
        s_ref[...] = jnp.zeros_like(s_ref)

    sgn = 1 - 2 * d
    ri = lax.broadcasted_iota(jnp.int32, (ln, ln), 0)
    ci = lax.broadcasted_iota(jnp.int32, (ln, ln), 1)
    cum_mask = ((ri - ci) * sgn >= 0).astype(F32)
    lw = lw_ref[...]
    cum = jnp.dot(cum_mask, lw, precision=HI, preferred_element_type=F32)
    g_in = jnp.exp(cum)
    g_ex = jnp.exp(cum - lw)
    g_inv = jnp.exp(-cum)
    g_tot = jnp.exp(jnp.sum(lw, axis=0, keepdims=True))
    rh = r_ref[...] * g_in
    ah = kk_ref[...] * g_ex
    bh = b_ref[...] * g_inv
    kh = kt_ref[...] * g_inv
    vv = v_ref[...]

    r2 = lax.broadcasted_iota(jnp.int32, (2 * ln, 2 * ln), 0)
    c2 = lax.broadcasted_iota(jnp.int32, (2 * ln, 2 * ln), 1)
    order = (r2 - c2) * sgn
    strict = order > 0
    incl = order >= 0
    eye = (r2 == c2).astype(F32)
    head0 = lax.broadcasted_iota(jnp.int32, (ln, LANES), 1) < R_HEAD
    lane_c = (((1,), (1,)), ((), ()))
    row_c = (((0,), (0,)), ((), ()))
    dot = functools.partial(jnp.dot, precision=HI, preferred_element_type=F32)
    dotg = functools.partial(lax.dot_general, precision=HI, preferred_element_type=F32)

    for p in range(r_ref.shape[1] // LANES):
        sl = slice(p * LANES, (p + 1) * LANES)

        def stack(x):
            xs = x[:, sl]
            return jnp.concatenate([jnp.where(head0, xs, 0.0), jnp.where(head0, 0.0, xs)], axis=0)

        a_s, r_s, b_s, k_s, v_s = stack(ah), stack(rh), stack(bh), stack(kh), stack(vv)
        ar = jnp.concatenate([a_s, r_s], axis=0)
        bk = jnp.concatenate([b_s, k_s], axis=0)
        g1 = dotg(ar, bk, lane_c)
        n = jnp.where(strict, g1[:2 * ln, :2 * ln], 0.0)
        m_ak = jnp.where(strict, g1[:2 * ln, 2 * ln:], 0.0)
        m_rb = jnp.where(incl, g1[2 * ln:, :2 * ln], 0.0)
        m_rk = jnp.where(incl, g1[2 * ln:, 2 * ln:], 0.0)
        x = eye - n
        pw = n
        steps = int(math.log2(ln)) - 1
        for _ in range(steps):
            pw = dot(pw, pw)
            x = x + dot(x, pw)
        s = s_ref[p]
        a_s_r_s = dotg(ar, s, lane_c)
        mv = dot(jnp.concatenate([m_ak, m_rk], axis=0), v_s)
        u = dot(x, a_s_r_s[:2 * ln] + mv[:2 * ln])
        ys = a_s_r_s[2 * ln:] + mv[2 * ln:] - dot(m_rb, u)
        y_ref[:, sl] = ys[:ln] + ys[ln:]
        ds = dotg(jnp.concatenate([v_s, -u], axis=0), jnp.concatenate([k_s, b_s], axis=0), row_c)
        s_ref[p] = (s + ds) * g_tot[:, sl]


def rwkv_scan(r, kk, v, lw, b, kt, chunk=CHUNK):
    bsz, t, c = r.shape
    chunk = min(chunk, t)
    nc = t // chunk
    cmap = lambda d, ci: ci + d * (nc - 1 - 2 * ci)
    one = pl.BlockSpec((None, chunk, c), lambda d, bi, ci: (bi, cmap(d, ci), 0))
    two = pl.BlockSpec((None, None, chunk, c), lambda d, bi, ci: (d, bi, cmap(d, ci), 0))
    return pl.pallas_call(
        functools.partial(_scan_kernel, chunk=chunk),
        grid=(2, bsz, nc),
        in_specs=[one, one, one, two, two, two],
        out_specs=two,
        out_shape=jax.ShapeDtypeStruct((2, bsz, t, c), F32),
        scratch_shapes=[pltpu.VMEM((c // LANES, LANES, LANES), F32)],
        compiler_params=_params(("parallel", "parallel", "arbitrary")),
        name="rwkv_scan",
    )(r, kk, v, lw, b, kt)


def _rwkv_post_kernel(y0_ref, y1_ref, bonus_ref, g_ref, lng_ref, lnb_ref, o_ref):
    y = y0_ref[...] + y1_ref[...]
    mu = _head_sum_wide(y) * (1.0 / R_HEAD)
    yc = y - mu
    var = _head_sum_wide(yc * yc) * (1.0 / R_HEAD)
    yn = yc * lax.rsqrt(var + GN_EPS) * lng_ref[...] + lnb_ref[...]
    o_ref[...] = ((yn + bonus_ref[...]) * g_ref[...]).astype(o_ref.dtype)


def rwkv_post(y, bonus, g, ln_g, ln_b, tt=256):
    _, b, t, c = y.shape
    tt = min(tt, t)
    row = lambda a: a.reshape(1, -1).astype(F32)
    one = pl.BlockSpec((None, tt, c), lambda bi, i: (bi, i, 0))
    const = pl.BlockSpec((1, c), lambda bi, i: (0, 0))
    return pl.pallas_call(
        _rwkv_post_kernel,
        grid=(b, t // tt),
        in_specs=[pl.BlockSpec((None, None, tt, c), lambda bi, i: (0, bi, i, 0)),
                  pl.BlockSpec((None, None, tt, c), lambda bi, i: (1, bi, i, 0)),
                  one, one, const, const],
        out_specs=one,
        out_shape=jax.ShapeDtypeStruct((b, t, c), BF16),
        compiler_params=_params(("parallel", "parallel")),
        name="rwkv_post",
    )(y, y, bonus, g, row(ln_g), row(ln_b))


def _merge_kernel(ya_ref, yr_ref, wa_ref, wr_ref, ga_ref, gr_ref, o_ref):
    a = jnp.dot(ya_ref[...], wa_ref[...], preferred_element_type=F32)
    r = jnp.dot(yr_ref[...], wr_ref[...], preferred_element_type=F32)
    o_ref[...] = (ga_ref[...] * a + gr_ref[...] * r).astype(o_ref.dtype)


def merge_branches(ya, yr, wa, wr, gates, tm=512, tn=512):
    m, ka = ya.shape
    kr = yr.shape[1]
    n = wa.shape[1]
    tm = min(tm, m)
    nj = n // tn
    return pl.pallas_call(
        _merge_kernel,
        grid=(m // tm, nj),
        in_specs=[pl.BlockSpec((tm, ka), lambda i, j: (i, 0)),
                  pl.BlockSpec((tm, kr), lambda i, j: (i, 0)),
                  pl.BlockSpec((ka, tn), lambda i, j: (0, j)),
                  pl.BlockSpec((kr, tn), lambda i, j: (0, j)),
                  pl.BlockSpec((tm, tn), lambda i, j: (i, j)),
                  pl.BlockSpec((tm, tn), lambda i, j: (i, j + nj))],
        out_specs=pl.BlockSpec((tm, tn), lambda i, j: (i, j)),
        out_shape=jax.ShapeDtypeStruct((m, n), BF16),
        compiler_params=_params(("parallel", "parallel")),
        name="merge_branches",
    )(ya, yr, wa, wr, gates, gates)


def _router_kernel(u_ref, w_ref, b_ref, idx_ref, gs_ref, rank_ref, cnt_ref, carry_ref, *, tm):
    i = pl.program_id(0)

    @pl.when(i == 0)
    def _():
        carry_ref[...] = jnp.zeros_like(carry_ref)

    logits = jnp.dot(u_ref[...], w_ref[...], precision=HI, preferred_element_type=F32)
    scores = _sigmoid(logits)
    cur = scores + b_ref[...]
    lane = lax.broadcasted_iota(jnp.int32, (tm, N_EXPERTS), 1).astype(F32)
    lane_out = lax.broadcasted_iota(jnp.int32, (tm, LANES), 1)
    picks = []
    sel_f = jnp.zeros((tm, N_EXPERTS), F32)
    idx_out = jnp.zeros((tm, LANES), F32)
    for k in range(TOP_K):
        best = jnp.max(cur, axis=-1, keepdims=True)
        ik = jnp.min(jnp.where(cur == best, lane, float(N_EXPERTS)), axis=-1, keepdims=True)
        onehot = lane == ik
        picks.append(onehot)
        sel_f = jnp.where(onehot, 1.0, sel_f)
        cur = jnp.where(onehot, -jnp.inf, cur)
        idx_out = jnp.where(lane_out == k, ik, idx_out)
    gsel = scores * sel_f
    gsel = gsel / jnp.sum(gsel, axis=-1, keepdims=True) * ROUTED_SCALE
    ri = lax.broadcasted_iota(jnp.int32, (tm, tm), 0)
    ci = lax.broadcasted_iota(jnp.int32, (tm, tm), 1)
    before = jnp.where(ri > ci, 1.0, 0.0).astype(BF16)
    rank =jnp.dot(before, sel_f.astype(BF16), preferred_element_type=F32) + carry_ref[...]
    carry_ref[...] = carry_ref[...] + jnp.sum(sel_f, axis=0, keepdims=True)
    cnt_ref[...] = carry_ref[...].astype(jnp.int32)
    gs_out = jnp.zeros((tm, LANES), F32)
    rank_out = jnp.zeros((tm, LANES), F32)
    for k in range(TOP_K):
        gk = jnp.sum(jnp.where(picks[k], gsel, 0.0), axis=-1, keepdims=True)
        rk = jnp.sum(jnp.where(picks[k], rank, 0.0), axis=-1, keepdims=True)
        gs_out = jnp.where(lane_out == k, gk, gs_out)
        rank_out = jnp.where(lane_out == k, rk, rank_out)
    idx_ref[...] = idx_out.astype(jnp.int32)
    gs_ref[...] = gs_out
    rank_ref[...] = rank_out.astype(jnp.int32)


def router(u, w_router, b_router, tm=256):
    n, d = u.shape
    tm = min(tm, n)
    tile = pl.BlockSpec((tm, LANES), lambda i: (i, 0))
    return pl.pallas_call(
        functools.partial(_router_kernel, tm=tm),
        grid=(n // tm,),
        in_specs=[pl.BlockSpec((tm, d), lambda i: (i, 0)),
                  pl.BlockSpec((d, N_EXPERTS), lambda i: (0, 0)),
                  pl.BlockSpec((1, N_EXPERTS), lambda i: (0, 0))],
        out_specs=[tile, tile, tile, pl.BlockSpec((1, N_EXPERTS), lambda i: (0, 0))],
        out_shape=[jax.ShapeDtypeStruct((n, LANES), jnp.int32),
                   jax.ShapeDtypeStruct((n, LANES), F32),
                   jax.ShapeDtypeStruct((n, LANES), jnp.int32),
                   jax.ShapeDtypeStruct((1, N_EXPERTS), jnp.int32)],
        scratch_shapes=[pltpu.VMEM((1, N_EXPERTS), F32)],
        compiler_params=_params(("arbitrary",)),
        name="router",
    )(u, w_router.astype(F32), b_router.reshape(1, -1).astype(F32))


def _dispatch_kernel(dest_ref, u_ref, xs_in_ref, xs_ref, sem, *, td):
    del xs_in_ref
    base = pl.program_id(0) * (td * TOP_K)

    def copy(j):
        return pltpu.make_async_copy(u_ref.at[pl.ds(j // TOP_K, 1)],
                                     xs_ref.at[pl.ds(dest_ref[base + j], 1)], sem)

    def start(j, carry):
        copy(j).start()
        return carry

    def wait(j, carry):
        copy(j).wait()
        return carry

    lax.fori_loop(0, td * TOP_K, start, 0)
    lax.fori_loop(0, td * TOP_K, wait, 0)


def dispatch(u, dest, cap, td=64):
    n, d = u.shape
    td = min(td, n)
    grid_spec = pltpu.PrefetchScalarGridSpec(
        num_scalar_prefetch=1,
        grid=(n // td,),
        in_specs=[pl.BlockSpec((td, d), lambda i, s: (i, 0)),
                  pl.BlockSpec(memory_space=pl.ANY)],
        out_specs=pl.BlockSpec(memory_space=pl.ANY),
        scratch_shapes=[pltpu.SemaphoreType.DMA(())],
    )
    return pl.pallas_call(
        functools.partial(_dispatch_kernel, td=td),
        grid_spec=grid_spec,
        out_shape=jax.ShapeDtypeStruct((cap, d), u.dtype),
        input_output_aliases={2: 0},
        compiler_params=_params(("arbitrary",)),
        name="moe_dispatch",
    )(dest.reshape(-1), u, jnp.zeros((cap, d), u.dtype))


def _expert_kernel(be_ref, nused_ref, x_ref, wg_ref, wu_ref, wd_ref, y_ref):
    del be_ref

    @pl.when(pl.program_id(0) < nused_ref[0])
    def _():
        x = x_ref[...].astype(BF16)
        hg = jnp.dot(x, wg_ref[...], preferred_element_type=F32)
        hu = jnp.dot(x, wu_ref[...], preferred_element_type=F32)
        hb = (hg * _sigmoid(hg) * hu).astype(BF16)
        y_ref[...] = jnp.dot(hb, wd_ref[...], preferred_element_type=F32).astype(y_ref.dtype)

    @pl.when(pl.program_id(0) >= nused_ref[0])
    def _():
        y_ref[...] = jnp.zeros_like(y_ref)


def expert_ffn(xs, block_e, n_used, wg, wu, wd, bm):
    cap, d = xs.shape
    f = wg.shape[2]
    row = lambda i, be, nu: (jnp.minimum(i, nu[0] - 1), 0)
    grid_spec = pltpu.PrefetchScalarGridSpec(
        num_scalar_prefetch=2,
        grid=(cap // bm,),
        in_specs=[pl.BlockSpec((bm, d), row),
                  pl.BlockSpec((None, d, f), lambda i, be, nu: (be[i], 0, 0)),
                  pl.BlockSpec((None, d, f), lambda i, be, nu: (be[i], 0, 0)),
                  pl.BlockSpec((None, f, d), lambda i, be, nu: (be[i], 0, 0))],
        out_specs=pl.BlockSpec((bm, d), lambda i, be, nu: (i, 0)),
    )
    return pl.pallas_call(
        _expert_kernel,
        grid_spec=grid_spec,
        out_shape=jax.ShapeDtypeStruct((cap, d), F32),
        compiler_params=_params(("arbitrary",)),
        name="moe_experts",
    )(block_e, n_used, xs, wg, wu, wd)


def _shared_kernel(u_ref, h_ref, wg_ref, wu_ref, wd_ref, o_ref):
    x = u_ref[...].astype(BF16)
    hg = jnp.dot(x, wg_ref[...], preferred_element_type=F32)
    hu = jnp.dot(x, wu_ref[...], preferred_element_type=F32)
    hb = (hg * _sigmoid(hg) * hu).astype(BF16)
    o_ref[...] = h_ref[...] + jnp.dot(hb, wd_ref[...], preferred_element_type=F32)


def shared_ffn(u, h, wg, wu, wd, tm=512):
    n, d = u.shape
    f = wg.shape[1]
    tm = min(tm, n)
    tile = pl.BlockSpec((tm, d), lambda i: (i, 0))
    return pl.pallas_call(
        _shared_kernel,
        grid=(n // tm,),
        in_specs=[tile, tile,
                  pl.BlockSpec((d, f), lambda i: (0, 0)),
                  pl.BlockSpec((d, f), lambda i: (0, 0)),
                  pl.BlockSpec((f, d), lambda i: (0, 0))],
        out_specs=tile,
        out_shape=jax.ShapeDtypeStruct((n, d), F32),
        compiler_params=_params(("parallel",)),
        name="shared_ffn",
    )(u, h, wg, wu, wd)


def _combine_kernel(dest_ref, gs_ref, hs_ref, g_ref, y_ref, h_out, u_out, buf, sem, *, tt):
    base = pl.program_id(0) * (tt * TOP_K)

    def copy(j):
        return pltpu.make_async_copy(y_ref.at[pl.ds(dest_ref[base + j], 1)],
                                     buf.at[j % TOP_K, pl.ds(j // TOP_K, 1)], sem)

    def start(j, carry):
        copy(j).start()
        return carry

    def wait(j, carry):
        copy(j).wait()
        return carry

    lax.fori_loop(0, tt * TOP_K, start, 0)
    lax.fori_loop(0, tt * TOP_K, wait, 0)
    gs = gs_ref[...]
    h = hs_ref[...]
    for k in range(TOP_K):
        h = h + gs[:, k:k + 1] * buf[k]
    h_out[...] = h
    ms = jnp.mean(h * h, axis=-1, keepdims=True)
    u_out[...] = (h * lax.rsqrt(ms + RMS_EPS) * g_ref[...]).astype(u_out.dtype)


def combine(y, dest, gsel, hs, g_next, tt=32):
    n, d = hs.shape
    tt = min(tt, n)
    tile = pl.BlockSpec((tt, d), lambda i, s: (i, 0))
    grid_spec = pltpu.PrefetchScalarGridSpec(
        num_scalar_prefetch=1,
        grid=(n // tt,),
        in_specs=[pl.BlockSpec((tt, LANES), lambda i, s: (i, 0)),
                  tile,
                  pl.BlockSpec((1, d), lambda i, s: (0, 0)),
                  pl.BlockSpec(memory_space=pl.ANY)],
        out_specs=[tile, tile],
        scratch_shapes=[pltpu.VMEM((TOP_K, tt, d), F32), pltpu.SemaphoreType.DMA(())],
    )
    return pl.pallas_call(
        functools.partial(_combine_kernel, tt=tt),
        grid_spec=grid_spec,
        out_shape=[jax.ShapeDtypeStruct((n, d), F32), jax.ShapeDtypeStruct((n, d), BF16)],
        compiler_params=_params(("arbitrary",)),
        name="moe_combine",
    )(dest.reshape(-1), gsel, hs, g_next.reshape(1, -1).astype(F32), y)


def moe_ffn(h, g_norm, w_router, b_router, w_e_gate, w_e_up, w_e_down, w_s_gate, w_s_up, w_s_down,
            g_next, bm=EXPERT_BLOCK):
    n, d = h.shape
    u = rmsnorm(h, g_norm, F32)
    u_bf = u.astype(BF16)
    idx128, gs128, rank128, counts = router(u, w_router, b_router)
    idx = idx128[:, :TOP_K]
    rank = rank128[:, :TOP_K]
    counts = counts.reshape(-1)
    n_blk = (counts + bm - 1) // bm
    blk_end = jnp.cumsum(n_blk)
    pstart = (blk_end - n_blk) * bm
    dest = pstart[idx] + rank
    cap = (n * TOP_K // bm + N_EXPERTS) * bm
    n_blocks = cap // bm
    block_e = jnp.minimum(jnp.searchsorted(blk_end, jnp.arange(n_blocks), side="right"),
                          N_EXPERTS - 1).astype(jnp.int32)
    n_used = blk_end[-1:].astype(jnp.int32)
    xs = dispatch(u, dest.astype(jnp.int32), cap)
    y = expert_ffn(xs, block_e, n_used, w_e_gate.astype(BF16), w_e_up.astype(BF16),
                   w_e_down.astype(BF16), bm)
    hs = shared_ffn(u_bf, h, w_s_gate.astype(BF16), w_s_up.astype(BF16), w_s_down.astype(BF16))
    return combine(y, dest.astype(jnp.int32), gs128, hs, g_next)


def _ple_kernel(u_ref, p_ref, h_ref, wg_ref, wp_ref, gf_ref, o_ref):
    gate = _sigmoid(jnp.dot(u_ref[...], wg_ref[...], preferred_element_type=F32))
    proj = jnp.dot(p_ref[...].astype(BF16), wp_ref[...], preferred_element_type=F32)
    h = h_ref[...] + gate * proj
    ms = jnp.mean(h * h, axis=-1, keepdims=True)
    o_ref[...] = h * lax.rsqrt(ms + RMS_EPS) * gf_ref[...]


def ple_final(u, p, h, w_gate, w_proj, g_final, tm=256):
    n, d = h.shape
    pd = p.shape[1]
    tm = min(tm, n)
    tile = pl.BlockSpec((tm, d), lambda i: (i, 0))
    return pl.pallas_call(
        _ple_kernel,
        grid=(n // tm,),
        in_specs=[tile, pl.BlockSpec((tm, pd), lambda i: (i, 0)), tile,
                  pl.BlockSpec((d, d), lambda i: (0, 0)),
                  pl.BlockSpec((pd, d), lambda i: (0, 0)),
                  pl.BlockSpec((1, d), lambda i: (0, 0))],
        out_specs=tile,
        out_shape=jax.ShapeDtypeStruct((n, d), F32),
        compiler_params=_params(("parallel",)),
        name="ple_final",
    )(u, p, h, w_gate, w_proj, g_final.reshape(1, -1).astype(F32))


def token_mixing(x2, bsz, seq, norm_g, w_in, w_branch_gate, lq1, lk1, lq2, lk2, subln_g, shift_prev,
                 shift_next, w0, w_decay_up, a0, w_iclr_up, w_gate_up, k_k, k_a, r_k, ln_x_g, ln_x_b,
                 w_br_attn, w_br_rwkv, w_out, lambda_init):
    d = x2.shape[1]
    u = rmsnorm(x2, norm_g, BF16)
    w_attn = w_in[:, :A_COLS].astype(BF16)
    w_rwkv = jnp.pad(w_in[:, A_COLS:], ((0, 0), (0, R_COLS_PAD - R_COLS))).astype(BF16)
    za = matmul(u, w_attn, out_dtype=BF16, name="proj_attn")
    zr = matmul(u, w_rwkv, out_dtype=F32, name="proj_rwkv")
    gates = matmul(u, w_branch_gate.astype(BF16), out_dtype=F32, epilogue=_sigmoid_epilogue,
                   name="branch_gates")
    ya = diff_attention(za.reshape(bsz, seq, A_COLS), lq1, lk1, lq2, lk2, subln_g, lambda_init)
    r, kk, v, lw, b, kt, bonus, g = rwkv_prep(
        zr.reshape(bsz, seq, R_COLS_PAD), shift_prev, shift_next, w0, w_decay_up, a0, w_iclr_up,
        w_gate_up, k_k, k_a, r_k)
    y = rwkv_scan(r, kk, v, lw, b, kt)
    yr = rwkv_post(y, bonus, g, ln_x_g, ln_x_b)
    merged = merge_branches(ya.reshape(-1, A_WIDTH), yr.reshape(-1, R_WIDTH), w_br_attn.astype(BF16),
                            w_br_rwkv.astype(BF16), gates)
    return matmul(merged, w_out.astype(BF16), out_dtype=F32, epilogue=_residual_add, extras=(x2,),
                  name="out_proj")


def kernel(x, p, norm_mix_g, w_in, w_branch_gate, lambda_q1, lambda_k1, lambda_q2, lambda_k2, subln_g, shift_prev, shift_next, w0, w_decay_up, a0, w_iclr_up, w_gate_up, k_k, k_a, r_k, ln_x_g, ln_x_b, w_br_attn, w_br_rwkv, w_out, norm_ffn_g, w_router, b_router, w_e_gate, w_e_up, w_e_down, w_s_gate, w_s_up, w_s_down, norm_ple_g, w_ple_gate, w_ple_proj, norm_final_g):
    bsz, seq, d = x.shape
    depth = w_in.shape[0]
    assert depth == 1, "the final norm is fused into the last layer's embedding kernel"
    h = x.reshape(bsz * seq, d)
    i = 0
    lambda_init = 0.8 - 0.6 * math.exp(-0.3 * i)
    h = token_mixing(h, bsz, seq, norm_mix_g[i], w_in[i], w_branch_gate[i], lambda_q1[i], lambda_k1[i],
                     lambda_q2[i], lambda_k2[i], subln_g[i], shift_prev[i], shift_next[i], w0[i],
                     w_decay_up[i], a0[i], w_iclr_up[i], w_gate_up[i], k_k[i], k_a[i], r_k[i],
                     ln_x_g[i], ln_x_b[i], w_br_attn[i], w_br_rwkv[i], w_out[i], lambda_init)
    h, u = moe_ffn(h, norm_ffn_g[i], w_router[i], b_router[i], w_e_gate[i], w_e_up[i], w_e_down[i],
                   w_s_gate[i], w_s_up[i], w_s_down[i], norm_ple_g[i])
    out = ple_final(u, p[i].reshape(bsz * seq, -1), h, w_ple_gate[i].astype(BF16),
                    w_ple_proj[i].astype(BF16), norm_final_g)
    return out.reshape(bsz, seq, d)
```

```python
import functools
import math

import jax
import jax.numpy as jnp
from jax import lax
from jax.experimental import pallas as pl
from jax.experimental.pallas import tpu as pltpu

F32 = jnp.float32
BF16 = jnp.bfloat16
HI = lax.Precision.HIGHEST

D_MODEL = 2048
PLE_DIM = 256
A_HEADS = 8
A_QK_DIM = 64
A_V_DIM = 2 * A_QK_DIM
A_WIDTH = A_HEADS * A_V_DIM
R_HEADS = 16
R_HEAD = 64
R_WIDTH = R_HEADS * R_HEAD
DECAY_LORA = 64
ICLR_LORA = 64
GATE_LORA = 160
N_EXPERTS = 64
TOP_K = 8
EXPERT_FF = 512
SHARED_FF = 512
ROUTED_SCALE = 2.5
RMS_EPS = 1e-6
GN_EPS = 64e-5
Q_COLS = A_HEADS * 2 * A_QK_DIM
A_COLS = 2 * Q_COLS + A_WIDTH
R_COLS = 3 * R_WIDTH + 2 * DECAY_LORA + 2 * ICLR_LORA + GATE_LORA
R_COLS_PAD = 3584
LANES = 128
CHUNK = 64
EXPERT_BLOCK = 256
VMEM_LIMIT = 56 * 1024 * 1024


def _params(sem):
    return pltpu.CompilerParams(dimension_semantics=sem, vmem_limit_bytes=VMEM_LIMIT)


def _sigmoid(x):
    return 1.0 / (1.0 + jnp.exp(-x))


def _rmsnorm_kernel(x_ref, g_ref, o_ref):
    x = x_ref[...].astype(F32)
    ms = jnp.mean(x * x, axis=-1, keepdims=True)
    o_ref[...] = (x * lax.rsqrt(ms + RMS_EPS) * g_ref[...]).astype(o_ref.dtype)


def rmsnorm(x, g, out_dtype, tm=512):
    m, d = x.shape
    tm = min(tm, m)
    return pl.pallas_call(
        _rmsnorm_kernel,
        grid=(m // tm,),
        in_specs=[pl.BlockSpec((tm, d), lambda i: (i, 0)),
                  pl.BlockSpec((1, d), lambda i: (0, 0))],
        out_specs=pl.BlockSpec((tm, d), lambda i: (i, 0)),
        out_shape=jax.ShapeDtypeStruct((m, d), out_dtype),
        compiler_params=_params(("parallel",)),
        name="rmsnorm",
    )(x, g.reshape(1, d).astype(F32))


def _mm_kernel(x_ref, w_ref, *rest, epilogue):
    o_ref = rest[-1]
    acc = jnp.dot(x_ref[...], w_ref[...], preferred_element_type=F32)
    if epilogue is not None:
        acc = epilogue(acc, *[e[...] for e in rest[:-1]])
    o_ref[...] = acc.astype(o_ref.dtype)


def matmul(x, w, *, out_dtype, epilogue=None, extras=(), tm=512, tn=512, name="matmul"):
    m, k = x.shape
    n = w.shape[1]
    tm = min(tm, m)
    tn = min(tn, n)
    in_specs = [pl.BlockSpec((tm, k), lambda i, j: (i, 0)),
                pl.BlockSpec((k, tn), lambda i, j: (0, j))]
    in_specs += [pl.BlockSpec((tm, tn), lambda i, j: (i, j)) for _ in extras]
    return pl.pallas_call(
        functools.partial(_mm_kernel, epilogue=epilogue),
        grid=(m // tm, pl.cdiv(n, tn)),
        in_specs=in_specs,
        out_specs=pl.BlockSpec((tm, tn), lambda i, j: (i, j)),
        out_shape=jax.ShapeDtypeStruct((m, n), out_dtype),
        compiler_params=_params(("parallel", "parallel")),
        name=name,
    )(x, w, *extras)


def _residual_add(acc, res):
    return res + acc


def _sigmoid_epilogue(acc):
    return _sigmoid(acc)


def _attn_kernel(slopes_ref, q_ref, k_ref, v_ref, lq1_ref, lk1_ref, lq2_ref, lk2_ref, sg_ref,
                 o_ref, *, tq, seq, lambda_init):
    h = pl.program_id(1)
    qi = pl.program_id(2)
    lane = lax.broadcasted_iota(jnp.int32, (tq, LANES), 1)
    q = q_ref[...] * jnp.asarray(A_QK_DIM ** -0.5, BF16)
    zero = jnp.zeros_like(q)
    q0 = jnp.where(lane < A_QK_DIM, q, zero)
    q1 = jnp.where(lane < A_QK_DIM, zero, q)
    k = k_ref[...]
    v = v_ref[...]
    dims = (((1,), (1,)), ((), ()))
    qpos = qi * tq + lax.broadcasted_iota(jnp.int32, (tq, seq), 0)
    kpos = lax.broadcasted_iota(jnp.int32, (tq, seq), 1)
    bias = -slopes_ref[h] * jnp.abs(qpos - kpos).astype(F32)

    def one_map(qm):
        s = lax.dot_general(qm, k, dims, preferred_element_type=F32) + bias
        e = jnp.exp(s - jnp.max(s, axis=-1, keepdims=True))
        l = jnp.sum(e, axis=-1, keepdims=True)
        return jnp.dot(e.astype(BF16), v, preferred_element_type=F32) / l

    lam = (jnp.exp(jnp.sum(lq1_ref[...] * lk1_ref[...], axis=-1, keepdims=True))
           - jnp.exp(jnp.sum(lq2_ref[...] * lk2_ref[...], axis=-1, keepdims=True))
           + lambda_init)
    o = one_map(q0) - lam * one_map(q1)
    ms = jnp.mean(o * o, axis=-1, keepdims=True)
    o = o * lax.rsqrt(ms + RMS_EPS) * sg_ref[...]
    o_ref[...] = (o * (1.0 - lambda_init)).astype(o_ref.dtype)


def diff_attention(za, lq1, lk1, lq2, lk2, subln_g, lambda_init, tq=256):
    b, t, _ = za.shape
    tq = min(tq, t)
    h = jnp.arange(1, A_HEADS + 1, dtype=F32)
    slopes = jnp.exp2(-8.0 * h / A_HEADS)
    vec = lambda a: a.reshape(1, -1).astype(F32)
    small = lambda n: pl.BlockSpec((1, n), lambda bi, hi, qi, s: (0, 0))
    grid_spec = pltpu.PrefetchScalarGridSpec(
        num_scalar_prefetch=1,
        grid=(b, A_HEADS, t // tq),
        in_specs=[
            pl.BlockSpec((None, tq, LANES), lambda bi, hi, qi, s: (bi, qi, hi)),
            pl.BlockSpec((None, t, LANES), lambda bi, hi, qi, s: (bi, 0, A_HEADS + hi)),
            pl.BlockSpec((None, t, LANES), lambda bi, hi, qi, s: (bi, 0, 2 * A_HEADS + hi)),
            small(A_QK_DIM), small(A_QK_DIM), small(A_QK_DIM), small(A_QK_DIM), small(A_V_DIM),
        ],
        out_specs=pl.BlockSpec((None, tq, LANES), lambda bi, hi, qi, s: (bi, qi, hi)),
    )
    return pl.pallas_call(
        functools.partial(_attn_kernel, tq=tq, seq=t, lambda_init=lambda_init),
        grid_spec=grid_spec,
        out_shape=jax.ShapeDtypeStruct((b, t, A_WIDTH), BF16),
        compiler_params=_params(("parallel", "parallel", "parallel")),
        name="diff_attention",
    )(slopes, za, za, za, vec(lq1), vec(lk1), vec(lq2), vec(lk2), vec(subln_g))


def _head_sum(x):
    ri = lax.broadcasted_iota(jnp.int32, (LANES, LANES), 0) // R_HEAD
    ci = lax.broadcasted_iota(jnp.int32, (LANES, LANES), 1) // R_HEAD
    ones = (ri == ci).astype(F32)
    return jnp.dot(x, ones, precision=HI, preferred_element_type=F32)


def _head_sum_wide(x):
    return jnp.concatenate(
        [_head_sum(x[:, p * LANES:(p + 1) * LANES]) for p in range(x.shape[1] // LANES)], axis=1)


def _rwkv_prep_kernel(z_ref, zprev_ref, znext_ref, mup_ref, mun_ref, w0_ref, wdu_ref, a0_ref,
                      wiu_ref, wgu_ref, kk_ref, ka_ref, rk_ref,
                      r_out, kkn_out, v_out, lw_out, b_out, kt_out, bonus_out, g_out, *, tt):
    i = pl.program_id(1)
    n_i = pl.num_programs(1)
    c = R_WIDTH
    z = z_ref[...]
    row = lax.broadcasted_iota(jnp.int32, (tt, 1), 0)
    prev_row = jnp.where(i > 0, zprev_ref[7:8, :], 0.0)
    next_row = jnp.where(i < n_i - 1, znext_ref[0:1, :], 0.0)
    zp = jnp.where(row == 0, prev_row, pltpu.roll(z, 1, 0))
    zn = jnp.where(row == tt - 1, next_row, pltpu.roll(z, tt - 1, 0))
    zs = z + mup_ref[...] * (zp - z) + mun_ref[...] * (zn - z)

    r = zs[:, :c]
    k = zs[:, c:2 * c]
    v = zs[:, 2 * c:3 * c]
    o3 = 3 * c
    lane = lax.broadcasted_iota(jnp.int32, (tt, LANES), 1)
    first = lane < DECAY_LORA
    lw = jnp.tanh(zs[:, o3:o3 + LANES])
    la = zs[:, o3 + LANES:o3 + 2 * LANES]
    lg = _sigmoid(zs[:, o3 + 2 * LANES:o3 + 4 * LANES])
    g_out[...] = jnp.dot(lg.astype(BF16), wgu_ref[...], preferred_element_type=F32)

    kk = k * kk_ref[...]
    ss = _head_sum_wide(kk * kk)
    kk = kk * lax.rsqrt(jnp.maximum(ss, 1e-12))
    r_out[...] = r
    kkn_out[...] = kk
    v_out[...] = v

    kt_sum = jnp.zeros_like(k)
    for d in range(2):
        keep = first if d == 0 else jnp.logical_not(first)
        lw_d = jnp.where(keep, lw, 0.0).astype(BF16)
        la_d = jnp.where(keep, la, 0.0).astype(BF16)
        zw = jnp.dot(lw_d, wdu_ref[...], preferred_element_type=F32) + w0_ref[d:d + 1, :]
        lw_out[d] = -math.exp(-0.5) * _sigmoid(zw)
        a = _sigmoid(jnp.dot(la_d, wiu_ref[...], preferred_element_type=F32) + a0_ref[d:d + 1, :])
        kt = k * (1.0 + (a - 1.0) * ka_ref[...])
        kt_out[d] = kt
        b_out[d] = kk * a
        kt_sum = kt_sum + kt
    coef = _head_sum_wide(r * kt_sum * rk_ref[...])
    bonus_out[...] = coef * v


def rwkv_prep(zr, shift_prev, shift_next, w0, w_decay_up, a0, w_iclr_up, w_gate_up, k_k, k_a, r_k,
              tt=128):
    b, t, cp = zr.shape
    tt = min(tt, t)
    c = R_WIDTH
    pad = cp - R_COLS
    row = lambda a: a.reshape(1, -1).astype(F32)
    mup = jnp.pad(row(shift_prev), ((0, 0), (0, pad)))
    mun = jnp.pad(row(shift_next), ((0, 0), (0, pad)))
    wdu = w_decay_up.reshape(2 * DECAY_LORA, c).astype(BF16)
    wiu = w_iclr_up.reshape(2 * ICLR_LORA, c).astype(BF16)
    wgu = jnp.pad(w_gate_up, ((0, 2 * LANES - GATE_LORA), (0, 0))).astype(BF16)
    nb8 = t // 8
    const = lambda shape: pl.BlockSpec(shape, lambda bi, i: (0,) * len(shape))
    in_specs = [
        pl.BlockSpec((None, tt, cp), lambda bi, i: (bi, i, 0)),
        pl.BlockSpec((None, 8, cp), lambda bi, i: (bi, jnp.maximum(i * (tt // 8) - 1, 0), 0)),
        pl.BlockSpec((None, 8, cp), lambda bi, i: (bi, jnp.minimum((i + 1) * (tt // 8), nb8 - 1), 0)),
        const((1, cp)), const((1, cp)), const((2, c)), const((2 * DECAY_LORA, c)), const((2, c)),
        const((2 * ICLR_LORA, c)), const((2 * LANES, c)), const((1, c)), const((1, c)), const((1, c)),
    ]
    one = pl.BlockSpec((None, tt, c), lambda bi, i: (bi, i, 0))
    two = pl.BlockSpec((2, None, tt, c), lambda bi, i: (0, bi, i, 0))
    s1 = jax.ShapeDtypeStruct((b, t, c), F32)
    s2 = jax.ShapeDtypeStruct((2, b, t, c), F32)
    return pl.pallas_call(
        functools.partial(_rwkv_prep_kernel, tt=tt),
        grid=(b, t // tt),
        in_specs=in_specs,
        out_specs=[one, one, one, two, two, two, one, one],
        out_shape=[s1, s1, s1, s2, s2, s2, s1, s1],
        compiler_params=_params(("parallel", "parallel")),
        name="rwkv_prep",
    )(zr, zr, zr, mup, mun, w0.astype(F32), wdu, a0.astype(F32), wiu, wgu, row(k_k), row(k_a), row(r_k))


def _scan_kernel(r_ref, kk_ref, v_ref, lw_ref, b_ref, kt_ref, y_ref, s_ref, *, chunk):
    d = pl.program_id(0)
    cidx = pl.program_id(2)
    ln = chunk

    @pl.when(cidx == 0)
    def _():
        s_ref[...] = jnp.zeros_like(s_ref)

    sgn = 1 - 2 * d
    ri = lax.broadcasted_iota(jnp.int32, (ln, ln), 0)
    ci = lax.broadcasted_iota(jnp.int32, (ln, ln), 1)
    cum_mask = ((ri - ci) * sgn >= 0).astype(F32)
    lw = lw_ref[...]
    cum = jnp.dot(cum_mask, lw, precision=HI, preferred_element_type=F32)
    g_in = jnp.exp(cum)
    g_ex = jnp.exp(cum - lw)
    g_inv = jnp.exp(-cum)
    g_tot = jnp.exp(jnp.sum(lw, axis=0, keepdims=True))
    rh = r_ref[...] * g_in
    ah = kk_ref[...] * g_ex
    bh = b_ref[...] * g_inv
    kh = kt_ref[...] * g_inv
    vv = v_ref[...]

    r2 = lax.broadcasted_iota(jnp.int32, (2 * ln, 2 * ln), 0)
    c2 = lax.broadcasted_iota(jnp.int32, (2 * ln, 2 * ln), 1)
    order = (r2 - c2) * sgn
    strict = order > 0
    incl = order >= 0
    eye = (r2 == c2).astype(F32)
    head0 = lax.broadcasted_iota(jnp.int32, (ln, LANES), 1) < R_HEAD
    lane_c = (((1,), (1,)), ((), ()))
    row_c = (((0,), (0,)), ((), ()))

    def dot(a, b):
        return jnp.dot(a.astype(BF16), b.astype(BF16), preferred_element_type=F32)

    def dotg(a, b, dims):
        return lax.dot_general(a.astype(BF16), b.astype(BF16), dims, preferred_element_type=F32)

    pairs = range(r_ref.shape[1] // LANES)
    sls = [slice(p * LANES, (p + 1) * LANES) for p in pairs]

    def stack(x, sl):
        xs = x[:, sl]
        return jnp.concatenate([jnp.where(head0, xs, 0.0), jnp.where(head0, 0.0, xs)], axis=0)

    ar = [jnp.concatenate([stack(ah, sl), stack(rh, sl)], axis=0).astype(BF16) for sl in sls]
    bs = [stack(bh, sl).astype(BF16) for sl in sls]
    ks = [stack(kh, sl).astype(BF16) for sl in sls]
    vs = [stack(vv, sl).astype(BF16) for sl in sls]
    g1 = [dotg(ar[p], jnp.concatenate([bs[p], ks[p]], axis=0), lane_c) for p in pairs]
    n = [jnp.where(strict, g1[p][:2 * ln, :2 * ln], 0.0) for p in pairs]
    m_akrk = [jnp.concatenate([jnp.where(strict, g1[p][:2 * ln, 2 * ln:], 0.0),
                               jnp.where(incl, g1[p][2 * ln:, 2 * ln:], 0.0)], axis=0).astype(BF16)
              for p in pairs]
    m_rb = [jnp.where(incl, g1[p][2 * ln:, :2 * ln], 0.0).astype(BF16) for p in pairs]
    x = [eye - n[p] for p in pairs]
    pw = n
    for _ in range(int(math.log2(ln)) - 1):
        pw = [dot(pw[p], pw[p]) for p in pairs]
        x = [x[p] + dot(x[p], pw[p]) for p in pairs]
    s = [s_ref[p] for p in pairs]
    asrs = [dotg(ar[p], s[p], lane_c) for p in pairs]
    mv = [dot(m_akrk[p], vs[p]) for p in pairs]
    u = [dot(x[p], asrs[p][:2 * ln] + mv[p][:2 * ln]) for p in pairs]
    ys = [asrs[p][2 * ln:] + mv[p][2 * ln:] - dot(m_rb[p], u[p]) for p in pairs]
    for p in pairs:
        y_ref[:, sls[p]] = ys[p][:ln] + ys[p][ln:]
    ds = [dotg(jnp.concatenate([vs[p], (-u[p]).astype(BF16)], axis=0),
               jnp.concatenate([ks[p], bs[p]], axis=0), row_c) for p in pairs]
    for p in pairs:
        s_ref[p] = (s[p] + ds[p]) * g_tot[:, sls[p]]


def rwkv_scan(r, kk, v, lw, b, kt, chunk=CHUNK):
    bsz, t, c = r.shape
    chunk = min(chunk, t)
    nc = t // chunk
    cmap = lambda d, ci: ci + d * (nc - 1 - 2 * ci)
    one = pl.BlockSpec((None, chunk, c), lambda d, bi, ci: (bi, cmap(d, ci), 0))
    two = pl.BlockSpec((None, None, chunk, c), lambda d, bi, ci: (d, bi, cmap(d, ci), 0))
    return pl.pallas_call(
        functools.partial(_scan_kernel, chunk=chunk),
        grid=(2, bsz, nc),
        in_specs=[one, one, one, two, two, two],
        out_specs=two,
        out_shape=jax.ShapeDtypeStruct((2, bsz, t, c), F32),
        scratch_shapes=[pltpu.VMEM((c // LANES, LANES, LANES), F32)],
        compiler_params=_params(("parallel", "parallel", "arbitrary")),
        name="rwkv_scan",
    )(r, kk, v, lw, b, kt)


def _rwkv_post_kernel(y0_ref, y1_ref, bonus_ref, g_ref, lng_ref, lnb_ref, o_ref):
    y = y0_ref[...] + y1_ref[...]
    mu = _head_sum_wide(y) * (1.0 / R_HEAD)
    yc = y - mu
    var = _head_sum_wide(yc * yc) * (1.0 / R_HEAD)
    yn = yc * lax.rsqrt(var + GN_EPS) * lng_ref[...] + lnb_ref[...]
    o_ref[...] = ((yn + bonus_ref[...]) * g_ref[...]).astype(o_ref.dtype)


def rwkv_post(y, bonus, g, ln_g, ln_b, tt=256):
    _, b, t, c = y.shape
    tt = min(tt, t)
    row = lambda a: a.reshape(1, -1).astype(F32)
    one = pl.BlockSpec((None, tt, c), lambda bi, i: (bi, i, 0))
    const = pl.BlockSpec((1, c), lambda bi, i: (0, 0))
    return pl.pallas_call(
        _rwkv_post_kernel,
        grid=(b, t // tt),
        in_specs=[pl.BlockSpec((None, None, tt, c), lambda bi, i: (0, bi, i, 0)),
                  pl.BlockSpec((None, None, tt, c), lambda bi, i: (1, bi, i, 0)),
                  one, one, const, const],
        out_specs=one,
        out_shape=jax.ShapeDtypeStruct((b, t, c), BF16),
        compiler_params=_params(("parallel", "parallel")),
        name="rwkv_post",
    )(y, y, bonus, g, row(ln_g), row(ln_b))


def _merge_kernel(ya_ref, yr_ref, wa_ref, wr_ref, ga_ref, gr_ref, o_ref):
    a = jnp.dot(ya_ref[...], wa_ref[...], preferred_element_type=F32)
    r = jnp.dot(yr_ref[...], wr_ref[...], preferred_element_type=F32)
    o_ref[...] = (ga_ref[...] * a + gr_ref[...] * r).astype(o_ref.dtype)


def merge_branches(ya, yr, wa, wr, gates, tm=512, tn=512):
    m, ka = ya.shape
    kr = yr.shape[1]
    n = wa.shape[1]
    tm = min(tm, m)
    nj = n // tn
    return pl.pallas_call(
        _merge_kernel,
        grid=(m // tm, nj),
        in_specs=[pl.BlockSpec((tm, ka), lambda i, j: (i, 0)),
                  pl.BlockSpec((tm, kr), lambda i, j: (i, 0)),
                  pl.BlockSpec((ka, tn), lambda i, j: (0, j)),
                  pl.BlockSpec((kr, tn), lambda i, j: (0, j)),
                  pl.BlockSpec((tm, tn), lambda i, j: (i, j)),
                  pl.BlockSpec((tm, tn), lambda i, j: (i, j + nj))],
        out_specs=pl.BlockSpec((tm, tn), lambda i, j: (i, j)),
        out_shape=jax.ShapeDtypeStruct((m, n), BF16),
        compiler_params=_params(("parallel", "parallel")),
        name="merge_branches",
    )(ya, yr, wa, wr, gates, gates)


def _router_kernel(u_ref, w_ref, b_ref, idx_ref, gs_ref, rank_ref, cnt_ref, carry_ref, *, tm):
    i = pl.program_id(0)

    @pl.when(i == 0)
    def _():
        carry_ref[...] = jnp.zeros_like(carry_ref)

    logits = jnp.dot(u_ref[...], w_ref[...], precision=HI, preferred_element_type=F32)
    scores = _sigmoid(logits)
    cur = scores + b_ref[...]
    lane = lax.broadcasted_iota(jnp.int32, (tm, N_EXPERTS), 1).astype(F32)
    lane_out = lax.broadcasted_iota(jnp.int32, (tm, LANES), 1)
    picks = []
    sel_f = jnp.zeros((tm, N_EXPERTS), F32)
    idx_out = jnp.zeros((tm, LANES), F32)
    for k in range(TOP_K):
        best = jnp.max(cur, axis=-1, keepdims=True)
        ik = jnp.min(jnp.where(cur == best, lane, float(N_EXPERTS)), axis=-1, keepdims=True)
        onehot = lane == ik
        picks.append(onehot)
        sel_f = jnp.where(onehot, 1.0, sel_f)
        cur = jnp.where(onehot, -jnp.inf, cur)
        idx_out = jnp.where(lane_out == k, ik, idx_out)
    gsel = scores * sel_f
    gsel = gsel / jnp.sum(gsel, axis=-1, keepdims=True) * ROUTED_SCALE
    ri = lax.broadcasted_iota(jnp.int32, (tm, tm), 0)
    ci = lax.broadcasted_iota(jnp.int32, (tm, tm), 1)
    before = jnp.where(ri > ci, 1.0, 0.0).astype(BF16)
    rank =jnp.dot(before, sel_f.astype(BF16), preferred_element_type=F32) + carry_ref[...]
    carry_ref[...] = carry_ref[...] + jnp.sum(sel_f, axis=0, keepdims=True)
    cnt_ref[...] = carry_ref[...].astype(jnp.int32)
    gs_out = jnp.zeros((tm, LANES), F32)
    rank_out = jnp.zeros((tm, LANES), F32)
    for k in range(TOP_K):
        gk = jnp.sum(jnp.where(picks[k], gsel, 0.0), axis=-1, keepdims=True)
        rk = jnp.sum(jnp.where(picks[k], rank, 0.0), axis=-1, keepdims=True)
        gs_out = jnp.where(lane_out == k, gk, gs_out)
        rank_out = jnp.where(lane_out == k, rk, rank_out)
    idx_ref[...] = idx_out.astype(jnp.int32)
    gs_ref[...] = gs_out
    rank_ref[...] = rank_out.astype(jnp.int32)


def router(u, w_router, b_router, tm=256):
    n, d = u.shape
    tm = min(tm, n)
    tile = pl.BlockSpec((tm, LANES), lambda i: (i, 0))
    return pl.pallas_call(
        functools.partial(_router_kernel, tm=tm),
        grid=(n // tm,),
        in_specs=[pl.BlockSpec((tm, d), lambda i: (i, 0)),
                  pl.BlockSpec((d, N_EXPERTS), lambda i: (0, 0)),
                  pl.BlockSpec((1, N_EXPERTS), lambda i: (0, 0))],
        out_specs=[tile, tile, tile, pl.BlockSpec((1, N_EXPERTS), lambda i: (0, 0))],
        out_shape=[jax.ShapeDtypeStruct((n, LANES), jnp.int32),
                   jax.ShapeDtypeStruct((n, LANES), F32),
                   jax.ShapeDtypeStruct((n, LANES), jnp.int32),
                   jax.ShapeDtypeStruct((1, N_EXPERTS), jnp.int32)],
        scratch_shapes=[pltpu.VMEM((1, N_EXPERTS), F32)],
        compiler_params=_params(("arbitrary",)),
        name="router",
    )(u, w_router.astype(F32), b_router.reshape(1, -1).astype(F32))


def _dispatch_kernel(dest_ref, tail_ref, nused_ref, u_ref, xs_ref, zbuf, zsem, sem, *, td, bm):
    step = pl.program_id(0)
    n_blocks = xs_ref.shape[0] // bm

    @pl.when(step == 0)
    def _():
        zbuf[...] = jnp.zeros_like(zbuf)

        def zero_block(blk):
            return pltpu.make_async_copy(zbuf, xs_ref.at[pl.ds(blk * bm, bm)], zsem)

        def tails(e, carry):
            @pl.when(tail_ref[e] >= 0)
            def _():
                zero_block(tail_ref[e]).start()
            return carry

        def tails_wait(e, carry):
            @pl.when(tail_ref[e] >= 0)
            def _():
                zero_block(tail_ref[e]).wait()
            return carry

        def unused(blk, carry):
            zero_block(blk).start()
            return carry

        def unused_wait(blk, carry):
            zero_block(blk).wait()
            return carry

        lax.fori_loop(0, N_EXPERTS, tails, 0)
        lax.fori_loop(nused_ref[0], n_blocks, unused, 0)
        lax.fori_loop(0, N_EXPERTS, tails_wait, 0)
        lax.fori_loop(nused_ref[0], n_blocks, unused_wait, 0)

    base = step * (td * TOP_K)

    def row(i, carry):
        src = u_ref.at[pl.ds(i, 1)]
        for k in range(TOP_K):
            pltpu.make_async_copy(src, xs_ref.at[pl.ds(dest_ref[base + i * TOP_K + k], 1)], sem).start()
        return carry

    lax.fori_loop(0, td, row, 0)
    tile = xs_ref.at[pl.ds(0, td * TOP_K)]
    pltpu.make_async_copy(tile, tile, sem).wait()


def dispatch(u, dest, tail_blk, n_used, cap, bm, td=64):
    n, d = u.shape
    td = min(td, n)
    grid_spec = pltpu.PrefetchScalarGridSpec(
        num_scalar_prefetch=3,
        grid=(n // td,),
        in_specs=[pl.BlockSpec((td, d), lambda i, *_: (i, 0))],
        out_specs=pl.BlockSpec(memory_space=pl.ANY),
        scratch_shapes=[pltpu.VMEM((bm, d), u.dtype), pltpu.SemaphoreType.DMA(()),
                        pltpu.SemaphoreType.DMA(())],
    )
    return pl.pallas_call(
        functools.partial(_dispatch_kernel, td=td, bm=bm),
        grid_spec=grid_spec,
        out_shape=jax.ShapeDtypeStruct((cap, d), u.dtype),
        compiler_params=_params(("arbitrary",)),
        name="moe_dispatch",
    )(dest.reshape(-1), tail_blk, n_used, u)


def _expert_kernel(be_ref, nused_ref, x_ref, wg_ref, wu_ref, wd_ref, y_ref):
    del be_ref

    @pl.when(pl.program_id(0) < nused_ref[0])
    def _():
        x = x_ref[...].astype(BF16)
        hg = jnp.dot(x, wg_ref[...], preferred_element_type=F32)
        hu = jnp.dot(x, wu_ref[...], preferred_element_type=F32)
        hb = (hg * _sigmoid(hg) * hu).astype(BF16)
        y_ref[...] = jnp.dot(hb, wd_ref[...], preferred_element_type=F32).astype(y_ref.dtype)

    @pl.when(pl.program_id(0) >= nused_ref[0])
    def _():
        y_ref[...] = jnp.zeros_like(y_ref)


def expert_ffn(xs, block_e, n_used, wg, wu, wd, bm):
    cap, d = xs.shape
    f = wg.shape[2]
    row = lambda i, be, nu: (jnp.minimum(i, nu[0] - 1), 0)
    grid_spec = pltpu.PrefetchScalarGridSpec(
        num_scalar_prefetch=2,
        grid=(cap // bm,),
        in_specs=[pl.BlockSpec((bm, d), row),
                  pl.BlockSpec((None, d, f), lambda i, be, nu: (be[i], 0, 0)),
                  pl.BlockSpec((None, d, f), lambda i, be, nu: (be[i], 0, 0)),
                  pl.BlockSpec((None, f, d), lambda i, be, nu: (be[i], 0, 0))],
        out_specs=pl.BlockSpec((bm, d), lambda i, be, nu: (i, 0)),
    )
    return pl.pallas_call(
        _expert_kernel,
        grid_spec=grid_spec,
        out_shape=jax.ShapeDtypeStruct((cap, d), F32),
        compiler_params=_params(("arbitrary",)),
        name="moe_experts",
    )(block_e, n_used, xs, wg, wu, wd)


def _shared_kernel(u_ref, h_ref, wg_ref, wu_ref, wd_ref, o_ref):
    x = u_ref[...].astype(BF16)
    hg = jnp.dot(x, wg_ref[...], preferred_element_type=F32)
    hu = jnp.dot(x, wu_ref[...], preferred_element_type=F32)
    hb = (hg * _sigmoid(hg) * hu).astype(BF16)
    o_ref[...] = h_ref[...] + jnp.dot(hb, wd_ref[...], preferred_element_type=F32)


def shared_ffn(u, h, wg, wu, wd, tm=512):
    n, d = u.shape
    f = wg.shape[1]
    tm = min(tm, n)
    tile = pl.BlockSpec((tm, d), lambda i: (i, 0))
    return pl.pallas_call(
        _shared_kernel,
        grid=(n // tm,),
        in_specs=[tile, tile,
                  pl.BlockSpec((d, f), lambda i: (0, 0)),
                  pl.BlockSpec((d, f), lambda i: (0, 0)),
                  pl.BlockSpec((f, d), lambda i: (0, 0))],
        out_specs=tile,
        out_shape=jax.ShapeDtypeStruct((n, d), F32),
        compiler_params=_params(("parallel",)),
        name="shared_ffn",
    )(u, h, wg, wu, wd)


def _combine_kernel(dest_ref, gs_ref, hs_ref, g_ref, y_ref, h_out, u_out, buf, sems, *, tt):
    step = pl.program_id(0)
    slot = step % 2

    def issue(tile, to_slot):
        base = tile * (tt * TOP_K)

        def row(r, carry):
            for k in range(TOP_K):
                pltpu.make_async_copy(y_ref.at[pl.ds(dest_ref[base + r * TOP_K + k], 1)],
                                      buf.at[to_slot, k, pl.ds(r, 1)], sems.at[to_slot]).start()
            return carry

        lax.fori_loop(0, tt, row, 0)

    @pl.when(step == 0)
    def _():
        issue(0, 0)

    @pl.when(step + 1 < pl.num_programs(0))
    def _():
        issue(step + 1, 1 - slot)

    pltpu.make_async_copy(buf.at[slot], buf.at[slot], sems.at[slot]).wait()
    gs = gs_ref[...]
    h = hs_ref[...]
    for k in range(TOP_K):
        h = h + gs[:, k:k + 1] * buf[slot, k]
    h_out[...] = h
    ms = jnp.mean(h * h, axis=-1, keepdims=True)
    u_out[...] = (h * lax.rsqrt(ms + RMS_EPS) * g_ref[...]).astype(u_out.dtype)


def combine(y, dest, gsel, hs, g_next, tt=32):
    n, d = hs.shape
    tt = min(tt, n)
    tile = pl.BlockSpec((tt, d), lambda i, s: (i, 0))
    grid_spec = pltpu.PrefetchScalarGridSpec(
        num_scalar_prefetch=1,
        grid=(n // tt,),
        in_specs=[pl.BlockSpec((tt, LANES), lambda i, s: (i, 0)),
                  tile,
                  pl.BlockSpec((1, d), lambda i, s: (0, 0)),
                  pl.BlockSpec(memory_space=pl.ANY)],
        out_specs=[tile, tile],
        scratch_shapes=[pltpu.VMEM((2, TOP_K, tt, d), F32), pltpu.SemaphoreType.DMA((2,))],
    )
    return pl.pallas_call(
        functools.partial(_combine_kernel, tt=tt),
        grid_spec=grid_spec,
        out_shape=[jax.ShapeDtypeStruct((n, d), F32), jax.ShapeDtypeStruct((n, d), BF16)],
        compiler_params=_params(("arbitrary",)),
        name="moe_combine",
    )(dest.reshape(-1), gsel, hs, g_next.reshape(1, -1).astype(F32), y)


def moe_ffn(h, g_norm, w_router, b_router, w_e_gate, w_e_up, w_e_down, w_s_gate, w_s_up, w_s_down,
            g_next, bm=EXPERT_BLOCK):
    n, d = h.shape
    u = rmsnorm(h, g_norm, F32)
    u_bf = u.astype(BF16)
    idx128, gs128, rank128, counts = router(u, w_router, b_router)
    idx = idx128[:, :TOP_K]
    rank = rank128[:, :TOP_K]
    counts = counts.reshape(-1)
    n_blk = (counts + bm - 1) // bm
    blk_end = jnp.cumsum(n_blk)
    pstart = (blk_end - n_blk) * bm
    experts = jnp.arange(N_EXPERTS, dtype=jnp.int32)
    dest = rank + jnp.sum(jnp.where(idx[:, :, None] == experts, pstart, 0), axis=-1)
    dest = dest.astype(jnp.int32)
    cap = (n * TOP_K // bm + N_EXPERTS) * bm
    blocks = jnp.arange(cap // bm, dtype=jnp.int32)
    block_e = jnp.minimum(jnp.sum(blk_end[None, :] <= blocks[:, None], axis=1), N_EXPERTS - 1)
    block_e = block_e.astype(jnp.int32)
    n_used = blk_end[-1:].astype(jnp.int32)
    tail_blk = jnp.where(n_blk > 0, blk_end - 1, -1).astype(jnp.int32)
    xs = dispatch(u, dest, tail_blk, n_used, cap, bm)
    y = expert_ffn(xs, block_e, n_used, w_e_gate.astype(BF16), w_e_up.astype(BF16),
                   w_e_down.astype(BF16), bm)
    hs = shared_ffn(u_bf, h, w_s_gate.astype(BF16), w_s_up.astype(BF16), w_s_down.astype(BF16))
    return combine(y, dest, gs128, hs, g_next)


def _ple_kernel(u_ref, p_ref, h_ref, wg_ref, wp_ref, gf_ref, o_ref):
    gate = _sigmoid(jnp.dot(u_ref[...], wg_ref[...], preferred_element_type=F32))
    proj = jnp.dot(p_ref[...].astype(BF16), wp_ref[...], preferred_element_type=F32)
    h = h_ref[...] + gate * proj
    ms = jnp.mean(h * h, axis=-1, keepdims=True)
    o_ref[...] = h * lax.rsqrt(ms + RMS_EPS) * gf_ref[...]


def ple_final(u, p, h, w_gate, w_proj, g_final, tm=256):
    n, d = h.shape
    pd = p.shape[1]
    tm = min(tm, n)
    tile = pl.BlockSpec((tm, d), lambda i: (i, 0))
    return pl.pallas_call(
        _ple_kernel,
        grid=(n // tm,),
        in_specs=[tile, pl.BlockSpec((tm, pd), lambda i: (i, 0)), tile,
                  pl.BlockSpec((d, d), lambda i: (0, 0)),
                  pl.BlockSpec((pd, d), lambda i: (0, 0)),
                  pl.BlockSpec((1, d), lambda i: (0, 0))],
        out_specs=tile,
        out_shape=jax.ShapeDtypeStruct((n, d), F32),
        compiler_params=_params(("parallel",)),
        name="ple_final",
    )(u, p, h, w_gate, w_proj, g_final.reshape(1, -1).astype(F32))


def token_mixing(x2, bsz, seq, norm_g, w_in, w_branch_gate, lq1, lk1, lq2, lk2, subln_g, shift_prev,
                 shift_next, w0, w_decay_up, a0, w_iclr_up, w_gate_up, k_k, k_a, r_k, ln_x_g, ln_x_b,
                 w_br_attn, w_br_rwkv, w_out, lambda_init):
    d = x2.shape[1]
    u = rmsnorm(x2, norm_g, BF16)
    w_attn = w_in[:, :A_COLS].astype(BF16)
    w_rwkv = jnp.pad(w_in[:, A_COLS:], ((0, 0), (0, R_COLS_PAD - R_COLS))).astype(BF16)
    za = matmul(u, w_attn, out_dtype=BF16, name="proj_attn")
    zr = matmul(u, w_rwkv, out_dtype=F32, name="proj_rwkv")
    gates = matmul(u, w_branch_gate.astype(BF16), out_dtype=F32, epilogue=_sigmoid_epilogue,
                   name="branch_gates")
    ya = diff_attention(za.reshape(bsz, seq, A_COLS), lq1, lk1, lq2, lk2, subln_g, lambda_init)
    r, kk, v, lw, b, kt, bonus, g = rwkv_prep(
        zr.reshape(bsz, seq, R_COLS_PAD), shift_prev, shift_next, w0, w_decay_up, a0, w_iclr_up,
        w_gate_up, k_k, k_a, r_k)
    y = rwkv_scan(r, kk, v, lw, b, kt)
    yr = rwkv_post(y, bonus, g, ln_x_g, ln_x_b)
    merged = merge_branches(ya.reshape(-1, A_WIDTH), yr.reshape(-1, R_WIDTH), w_br_attn.astype(BF16),
                            w_br_rwkv.astype(BF16), gates)
    return matmul(merged, w_out.astype(BF16), out_dtype=F32, epilogue=_residual_add, extras=(x2,),
                  name="out_proj")


def kernel(x, p, norm_mix_g, w_in, w_branch_gate, lambda_q1, lambda_k1, lambda_q2, lambda_k2, subln_g, shift_prev, shift_next, w0, w_decay_up, a0, w_iclr_up, w_gate_up, k_k, k_a, r_k, ln_x_g, ln_x_b, w_br_attn, w_br_rwkv, w_out, norm_ffn_g, w_router, b_router, w_e_gate, w_e_up, w_e_down, w_s_gate, w_s_up, w_s_down, norm_ple_g, w_ple_gate, w_ple_proj, norm_final_g):
    bsz, seq, d = x.shape
    depth = w_in.shape[0]
    assert depth == 1, "the final norm is fused into the last layer's embedding kernel"
    h = x.reshape(bsz * seq, d)
    i = 0
    lambda_init = 0.8 - 0.6 * math.exp(-0.3 * i)
    h = token_mixing(h, bsz, seq, norm_mix_g[i], w_in[i], w_branch_gate[i], lambda_q1[i], lambda_k1[i],
                     lambda_q2[i], lambda_k2[i], subln_g[i], shift_prev[i], shift_next[i], w0[i],
                     w_decay_up[i], a0[i], w_iclr_up[i], w_gate_up[i], k_k[i], k_a[i], r_k[i],
                     ln_x_g[i], ln_x_b[i], w_br_attn[i], w_br_rwkv[i], w_out[i], lambda_init)
    h, u = moe_ffn(h, norm_ffn_g[i], w_router[i], b_router[i], w_e_gate[i], w_e_up[i], w_e_down[i],
                   w_s_gate[i], w_s_up[i], w_s_down[i], norm_ple_g[i])
    out = ple_final(u, p[i].reshape(bsz * seq, -1), h, w_ple_gate[i].astype(BF16),
                    w_ple_proj[i].astype(BF16), norm_final_g)
    return out.reshape(bsz, seq, d)
```

```python
import functools
import math

import jax
import jax.numpy as jnp
from jax import lax
from jax.experimental import pallas as pl
from jax.experimental.pallas import tpu as pltpu

F32 = jnp.float32
BF16 = jnp.bfloat16
HI = lax.Precision.HIGHEST

D_MODEL = 2048
PLE_DIM = 256
A_HEADS = 8
A_QK_DIM = 64
A_V_DIM = 2 * A_QK_DIM
A_WIDTH = A_HEADS * A_V_DIM
R_HEADS = 16
R_HEAD = 64
R_WIDTH = R_HEADS * R_HEAD
DECAY_LORA = 64
ICLR_LORA = 64
GATE_LORA = 160
N_EXPERTS = 64
TOP_K = 8
EXPERT_FF = 512
SHARED_FF = 512
ROUTED_SCALE = 2.5
RMS_EPS = 1e-6
GN_EPS = 64e-5
LOG2E = math.log2(math.e)
Q_COLS = A_HEADS * 2 * A_QK_DIM
A_COLS = 2 * Q_COLS + A_WIDTH
R_COLS = 3 * R_WIDTH + 2 * DECAY_LORA + 2 * ICLR_LORA + GATE_LORA
R_COLS_PAD = 3584
LANES = 128
CHUNK = 64
EXPERT_BLOCK = 256
VMEM_LIMIT = 56 * 1024 * 1024


def _params(sem):
    return pltpu.CompilerParams(dimension_semantics=sem, vmem_limit_bytes=VMEM_LIMIT)


def _sigmoid(x):
    return 1.0 / (1.0 + jnp.exp(-x))


def _rmsnorm_kernel(x_ref, g_ref, o_ref):
    x = x_ref[...].astype(F32)
    ms = jnp.mean(x * x, axis=-1, keepdims=True)
    o_ref[...] = (x * lax.rsqrt(ms + RMS_EPS) * g_ref[...]).astype(o_ref.dtype)


def rmsnorm(x, g, out_dtype, tm=512):
    m, d = x.shape
    tm = min(tm, m)
    return pl.pallas_call(
        _rmsnorm_kernel,
        grid=(m // tm,),
        in_specs=[pl.BlockSpec((tm, d), lambda i: (i, 0)),
                  pl.BlockSpec((1, d), lambda i: (0, 0))],
        out_specs=pl.BlockSpec((tm, d), lambda i: (i, 0)),
        out_shape=jax.ShapeDtypeStruct((m, d), out_dtype),
        compiler_params=_params(("parallel",)),
        name="rmsnorm",
    )(x, g.reshape(1, d).astype(F32))


def _mm_kernel(x_ref, w_ref, *rest, epilogue):
    o_ref = rest[-1]
    acc = jnp.dot(x_ref[...], w_ref[...], preferred_element_type=F32)
    if epilogue is not None:
        acc = epilogue(acc, *[e[...] for e in rest[:-1]])
    o_ref[...] = acc.astype(o_ref.dtype)


def matmul(x, w, *, out_dtype, epilogue=None, extras=(), tm=512, tn=512, name="matmul"):
    m, k = x.shape
    n = w.shape[1]
    tm = min(tm, m)
    tn = min(tn, n)
    in_specs = [pl.BlockSpec((tm, k), lambda i, j: (i, 0)),
                pl.BlockSpec((k, tn), lambda i, j: (0, j))]
    in_specs += [pl.BlockSpec((tm, tn), lambda i, j: (i, j)) for _ in extras]
    return pl.pallas_call(
        functools.partial(_mm_kernel, epilogue=epilogue),
        grid=(m // tm, pl.cdiv(n, tn)),
        in_specs=in_specs,
        out_specs=pl.BlockSpec((tm, tn), lambda i, j: (i, j)),
        out_shape=jax.ShapeDtypeStruct((m, n), out_dtype),
        compiler_params=_params(("parallel", "parallel")),
        name=name,
    )(x, w, *extras)


def _residual_add(acc, res):
    return res + acc


def _scale_q_epilogue(acc, *, tn):
    is_q = pl.program_id(1) < Q_COLS // tn
    return acc * jnp.where(is_q, LOG2E * A_QK_DIM ** -0.5, 1.0)


def _attn_kernel(slopes_ref, q_ref, k_ref, v_ref, lq1_ref, lk1_ref, lq2_ref, lk2_ref, sg_ref,
                 o_ref, bias_ref, s_ref, *, tq, seq, kc, lambda_init):
    h = pl.program_id(0)
    qi = pl.program_id(1)

    @pl.when(pl.program_id(2) == 0)
    def _():
        qpos = qi * tq + lax.broadcasted_iota(jnp.int32, (tq, seq), 0)
        kpos = lax.broadcasted_iota(jnp.int32, (tq, seq), 1)
        bias_ref[...] = (-LOG2E * slopes_ref[h]) * jnp.abs(qpos - kpos).astype(F32)

    lane = lax.broadcasted_iota(jnp.int32, (tq, LANES), 1)
    q = q_ref[...]
    zero = jnp.zeros_like(q)
    qm = [jnp.where(lane < A_QK_DIM, q, zero), jnp.where(lane < A_QK_DIM, zero, q)]
    chunks = [slice(c * kc, (c + 1) * kc) for c in range(seq // kc)]
    dims = (((1,), (1,)), ((), ()))

    def scores(m, row_max):
        for sl in chunks:
            sc = lax.dot_general(qm[m], k_ref[sl, :], dims, preferred_element_type=F32) + bias_ref[:, sl]
            s_ref[m, :, sl] = sc
            cmax = jnp.max(sc, axis=-1, keepdims=True)
            row_max = cmax if row_max is None else jnp.maximum(row_max, cmax)
        return row_max

    def weighted_values(m, row_max):
        l = jnp.zeros((tq, 1), F32)
        acc = jnp.zeros((tq, LANES), F32)
        for sl in chunks:
            e = jnp.exp2(s_ref[m, :, sl] - row_max)
            l = l + jnp.sum(e, axis=-1, keepdims=True)
            acc = acc + jnp.dot(e.astype(BF16), v_ref[sl, :], preferred_element_type=F32)
        return acc / l

    max0 = scores(0, None)
    max1 = scores(1, None)
    o0 = weighted_values(0, max0)
    o1 = weighted_values(1, max1)
    lam = (jnp.exp(jnp.sum(lq1_ref[...] * lk1_ref[...], axis=-1, keepdims=True))
           - jnp.exp(jnp.sum(lq2_ref[...] * lk2_ref[...], axis=-1, keepdims=True))
           + lambda_init)
    o = o0 - lam * o1
    ms = jnp.mean(o * o, axis=-1, keepdims=True)
    o = o * lax.rsqrt(ms + RMS_EPS) * sg_ref[...]
    o_ref[...] = (o * (1.0 - lambda_init)).astype(o_ref.dtype)


def diff_attention(za, lq1, lk1, lq2, lk2, subln_g, lambda_init, tq=256):
    b, t, _ = za.shape
    tq = min(tq, t)
    h = jnp.arange(1, A_HEADS + 1, dtype=F32)
    slopes = jnp.exp2(-8.0 * h / A_HEADS)
    vec = lambda a: a.reshape(1, -1).astype(F32)
    small = lambda n: pl.BlockSpec((1, n), lambda hi, qi, bi, s: (0, 0))
    grid_spec = pltpu.PrefetchScalarGridSpec(
        num_scalar_prefetch=1,
        grid=(A_HEADS, t // tq, b),
        in_specs=[
            pl.BlockSpec((None, tq, LANES), lambda hi, qi, bi, s: (bi, qi, hi)),
            pl.BlockSpec((None, t, LANES), lambda hi, qi, bi, s: (bi, 0, A_HEADS + hi)),
            pl.BlockSpec((None, t, LANES), lambda hi, qi, bi, s: (bi, 0, 2 * A_HEADS + hi)),
            small(A_QK_DIM), small(A_QK_DIM), small(A_QK_DIM), small(A_QK_DIM), small(A_V_DIM),
        ],
        out_specs=pl.BlockSpec((None, tq, LANES), lambda hi, qi, bi, s: (bi, qi, hi)),
        scratch_shapes=[pltpu.VMEM((tq, t), F32), pltpu.VMEM((2, tq, t), F32)],
    )
    return pl.pallas_call(
        functools.partial(_attn_kernel, tq=tq, seq=t, kc=min(512, t), lambda_init=lambda_init),
        grid_spec=grid_spec,
        out_shape=jax.ShapeDtypeStruct((b, t, A_WIDTH), BF16),
        compiler_params=_params(("parallel", "parallel", "arbitrary")),
        name="diff_attention",
    )(slopes, za, za, za, vec(lq1), vec(lk1), vec(lq2), vec(lk2), vec(subln_g))


def _head_sum(x):
    ri = lax.broadcasted_iota(jnp.int32, (LANES, LANES), 0) // R_HEAD
    ci = lax.broadcasted_iota(jnp.int32, (LANES, LANES), 1) // R_HEAD
    ones = (ri == ci).astype(F32)
    return jnp.dot(x, ones, precision=HI, preferred_element_type=F32)


def _head_sum_wide(x):
    return jnp.concatenate(
        [_head_sum(x[:, p * LANES:(p + 1) * LANES]) for p in range(x.shape[1] // LANES)], axis=1)


def _rwkv_prep_kernel(z_ref, zprev_ref, znext_ref, mup_ref, mun_ref, w0_ref, wdu_ref, a0_ref,
                      wiu_ref, wgu_ref, kk_ref, ka_ref, rk_ref,
                      r_out, kkn_out, v_out, lw_out, b_out, kt_out, bonus_out, g_out, *, tt):
    i = pl.program_id(1)
    n_i = pl.num_programs(1)
    c = R_WIDTH
    z = z_ref[...]
    row = lax.broadcasted_iota(jnp.int32, (tt, 1), 0)
    prev_row = jnp.where(i > 0, zprev_ref[7:8, :], 0.0)
    next_row = jnp.where(i < n_i - 1, znext_ref[0:1, :], 0.0)
    zp = jnp.where(row == 0, prev_row, pltpu.roll(z, 1, 0))
    zn = jnp.where(row == tt - 1, next_row, pltpu.roll(z, tt - 1, 0))
    zs = z + mup_ref[...] * (zp - z) + mun_ref[...] * (zn - z)

    r = zs[:, :c]
    k = zs[:, c:2 * c]
    v = zs[:, 2 * c:3 * c]
    o3 = 3 * c
    lane = lax.broadcasted_iota(jnp.int32, (tt, LANES), 1)
    first = lane < DECAY_LORA
    lw = jnp.tanh(zs[:, o3:o3 + LANES])
    la = zs[:, o3 + LANES:o3 + 2 * LANES]
    lg = _sigmoid(zs[:, o3 + 2 * LANES:o3 + 4 * LANES])
    g_out[...] = jnp.dot(lg.astype(BF16), wgu_ref[...], preferred_element_type=F32)

    kk = k * kk_ref[...]
    ss = _head_sum_wide(kk * kk)
    kk = kk * lax.rsqrt(jnp.maximum(ss, 1e-12))
    r_out[...] = r
    kkn_out[...] = kk
    v_out[...] = v

    kt_sum = jnp.zeros_like(k)
    for d in range(2):
        keep = first if d == 0 else jnp.logical_not(first)
        lw_d = jnp.where(keep, lw, 0.0).astype(BF16)
        la_d = jnp.where(keep, la, 0.0).astype(BF16)
        zw = jnp.dot(lw_d, wdu_ref[...], preferred_element_type=F32) + w0_ref[d:d + 1, :]
        lw_out[d] = -math.exp(-0.5) * _sigmoid(zw)
        a = _sigmoid(jnp.dot(la_d, wiu_ref[...], preferred_element_type=F32) + a0_ref[d:d + 1, :])
        kt = k * (1.0 + (a - 1.0) * ka_ref[...])
        kt_out[d] = kt
        b_out[d] = kk * a
        kt_sum = kt_sum + kt
    coef = _head_sum_wide(r * kt_sum * rk_ref[...])
    bonus_out[...] = coef * v


def rwkv_prep(zr, shift_prev, shift_next, w0, w_decay_up, a0, w_iclr_up, w_gate_up, k_k, k_a, r_k,
              tt=128):
    b, t, cp = zr.shape
    tt = min(tt, t)
    c = R_WIDTH
    pad = cp - R_COLS
    row = lambda a: a.reshape(1, -1).astype(F32)
    mup = jnp.pad(row(shift_prev), ((0, 0), (0, pad)))
    mun = jnp.pad(row(shift_next), ((0, 0), (0, pad)))
    wdu = w_decay_up.reshape(2 * DECAY_LORA, c).astype(BF16)
    wiu = w_iclr_up.reshape(2 * ICLR_LORA, c).astype(BF16)
    wgu = jnp.pad(w_gate_up, ((0, 2 * LANES - GATE_LORA), (0, 0))).astype(BF16)
    nb8 = t // 8
    const = lambda shape: pl.BlockSpec(shape, lambda bi, i: (0,) * len(shape))
    in_specs = [
        pl.BlockSpec((None, tt, cp), lambda bi, i: (bi, i, 0)),
        pl.BlockSpec((None, 8, cp), lambda bi, i: (bi, jnp.maximum(i * (tt // 8) - 1, 0), 0)),
        pl.BlockSpec((None, 8, cp), lambda bi, i: (bi, jnp.minimum((i + 1) * (tt // 8), nb8 - 1), 0)),
        const((1, cp)), const((1, cp)), const((2, c)), const((2 * DECAY_LORA, c)), const((2, c)),
        const((2 * ICLR_LORA, c)), const((2 * LANES, c)), const((1, c)), const((1, c)), const((1, c)),
    ]
    one = pl.BlockSpec((None, tt, c), lambda bi, i: (bi, i, 0))
    two = pl.BlockSpec((2, None, tt, c), lambda bi, i: (0, bi, i, 0))
    s1 = jax.ShapeDtypeStruct((b, t, c), F32)
    s2 = jax.ShapeDtypeStruct((2, b, t, c), F32)
    return pl.pallas_call(
        functools.partial(_rwkv_prep_kernel, tt=tt),
        grid=(b, t // tt),
        in_specs=in_specs,
        out_specs=[one, one, one, two, two, two, one, one],
        out_shape=[s1, s1, s1, s2, s2, s2, s1, s1],
        compiler_params=_params(("parallel", "parallel")),
        name="rwkv_prep",
    )(zr, zr, zr, mup, mun, w0.astype(F32), wdu, a0.astype(F32), wiu, wgu, row(k_k), row(k_a), row(r_k))


def _scan_kernel(r_ref, kk_ref, v_ref, lw_ref, b_ref, kt_ref, y_ref, s_ref, *, chunk):
    d = pl.program_id(0)
    cidx = pl.program_id(2)
    ln = chunk

    @pl.when(cidx == 0)
    def _():
        s_ref[...] = jnp.zeros_like(s_ref)

    sgn = 1 - 2 * d
    ri = lax.broadcasted_iota(jnp.int32, (ln, ln), 0)
    ci = lax.broadcasted_iota(jnp.int32, (ln, ln), 1)
    cum_mask = ((ri - ci) * sgn >= 0).astype(F32)
    lw = lw_ref[...]
    cum = jnp.dot(cum_mask, lw, precision=HI, preferred_element_type=F32)
    g_in = jnp.exp(cum)
    g_ex = jnp.exp(cum - lw)
    g_inv = jnp.exp(-cum)
    g_tot = jnp.exp(jnp.sum(lw, axis=0, keepdims=True))
    rh = r_ref[...] * g_in
    ah = kk_ref[...] * g_ex
    bh = b_ref[...] * g_inv
    kh = kt_ref[...] * g_inv
    vv = v_ref[...]

    r2 = lax.broadcasted_iota(jnp.int32, (2 * ln, 2 * ln), 0)
    c2 = lax.broadcasted_iota(jnp.int32, (2 * ln, 2 * ln), 1)
    order = (r2 - c2) * sgn
    strict = order > 0
    incl = order >= 0
    eye = (r2 == c2).astype(F32)
    head0 = lax.broadcasted_iota(jnp.int32, (ln, LANES), 1) < R_HEAD
    lane_c = (((1,), (1,)), ((), ()))
    row_c = (((0,), (0,)), ((), ()))

    def dot(a, b):
        return jnp.dot(a.astype(BF16), b.astype(BF16), preferred_element_type=F32)

    def dotg(a, b, dims):
        return lax.dot_general(a.astype(BF16), b.astype(BF16), dims, preferred_element_type=F32)

    pairs = range(r_ref.shape[1] // LANES)
    sls = [slice(p * LANES, (p + 1) * LANES) for p in pairs]

    def stack(x, sl):
        xs = x[:, sl]
        return jnp.concatenate([jnp.where(head0, xs, 0.0), jnp.where(head0, 0.0, xs)], axis=0)

    ar = [jnp.concatenate([stack(ah, sl), stack(rh, sl)], axis=0).astype(BF16) for sl in sls]
    bs = [stack(bh, sl).astype(BF16) for sl in sls]
    ks = [stack(kh, sl).astype(BF16) for sl in sls]
    vs = [stack(vv, sl).astype(BF16) for sl in sls]
    g1 = [dotg(ar[p], jnp.concatenate([bs[p], ks[p]], axis=0), lane_c) for p in pairs]
    n = [jnp.where(strict, g1[p][:2 * ln, :2 * ln], 0.0) for p in pairs]
    m_akrk = [jnp.concatenate([jnp.where(strict, g1[p][:2 * ln, 2 * ln:], 0.0),
                               jnp.where(incl, g1[p][2 * ln:, 2 * ln:], 0.0)], axis=0).astype(BF16)
              for p in pairs]
    m_rb = [jnp.where(incl, g1[p][2 * ln:, :2 * ln], 0.0).astype(BF16) for p in pairs]
    x = [eye - n[p] for p in pairs]
    pw = n
    for _ in range(int(math.log2(ln)) - 1):
        pw = [dot(pw[p], pw[p]) for p in pairs]
        x = [x[p] + dot(x[p], pw[p]) for p in pairs]
    s = [s_ref[p] for p in pairs]
    asrs = [dotg(ar[p], s[p], lane_c) for p in pairs]
    mv = [dot(m_akrk[p], vs[p]) for p in pairs]
    u = [dot(x[p], asrs[p][:2 * ln] + mv[p][:2 * ln]) for p in pairs]
    ys = [asrs[p][2 * ln:] + mv[p][2 * ln:] - dot(m_rb[p], u[p]) for p in pairs]
    for p in pairs:
        y_ref[:, sls[p]] = ys[p][:ln] + ys[p][ln:]
    ds = [dotg(jnp.concatenate([vs[p], (-u[p]).astype(BF16)], axis=0),
               jnp.concatenate([ks[p], bs[p]], axis=0), row_c) for p in pairs]
    for p in pairs:
        s_ref[p] = (s[p] + ds[p]) * g_tot[:, sls[p]]


def rwkv_scan(r, kk, v, lw, b, kt, chunk=CHUNK):
    bsz, t, c = r.shape
    chunk = min(chunk, t)
    nc = t // chunk
    cmap = lambda d, ci: ci + d * (nc - 1 - 2 * ci)
    one = pl.BlockSpec((None, chunk, c), lambda d, bi, ci: (bi, cmap(d, ci), 0))
    two = pl.BlockSpec((None, None, chunk, c), lambda d, bi, ci: (d, bi, cmap(d, ci), 0))
    return pl.pallas_call(
        functools.partial(_scan_kernel, chunk=chunk),
        grid=(2, bsz, nc),
        in_specs=[one, one, one, two, two, two],
        out_specs=two,
        out_shape=jax.ShapeDtypeStruct((2, bsz, t, c), F32),
        scratch_shapes=[pltpu.VMEM((c // LANES, LANES, LANES), F32)],
        compiler_params=_params(("parallel", "parallel", "arbitrary")),
        name="rwkv_scan",
    )(r, kk, v, lw, b, kt)


def _rwkv_post_kernel(y0_ref, y1_ref, bonus_ref, g_ref, lng_ref, lnb_ref, o_ref):
    y = y0_ref[...] + y1_ref[...]
    mu = _head_sum_wide(y) * (1.0 / R_HEAD)
    yc = y - mu
    var = _head_sum_wide(yc * yc) * (1.0 / R_HEAD)
    yn = yc * lax.rsqrt(var + GN_EPS) * lng_ref[...] + lnb_ref[...]
    o_ref[...] = ((yn + bonus_ref[...]) * g_ref[...]).astype(o_ref.dtype)


def rwkv_post(y, bonus, g, ln_g, ln_b, tt=256):
    _, b, t, c = y.shape
    tt = min(tt, t)
    row = lambda a: a.reshape(1, -1).astype(F32)
    one = pl.BlockSpec((None, tt, c), lambda bi, i: (bi, i, 0))
    const = pl.BlockSpec((1, c), lambda bi, i: (0, 0))
    return pl.pallas_call(
        _rwkv_post_kernel,
        grid=(b, t // tt),
        in_specs=[pl.BlockSpec((None, None, tt, c), lambda bi, i: (0, bi, i, 0)),
                  pl.BlockSpec((None, None, tt, c), lambda bi, i: (1, bi, i, 0)),
                  one, one, const, const],
        out_specs=one,
        out_shape=jax.ShapeDtypeStruct((b, t, c), BF16),
        compiler_params=_params(("parallel", "parallel")),
        name="rwkv_post",
    )(y, y, bonus, g, row(ln_g), row(ln_b))


def _merge_kernel(u_ref, ya_ref, yr_ref, wga_ref, wgr_ref, wa_ref, wr_ref, o_ref):
    u = u_ref[...]
    ga = _sigmoid(jnp.dot(u, wga_ref[...], preferred_element_type=F32))
    gr = _sigmoid(jnp.dot(u, wgr_ref[...], preferred_element_type=F32))
    a = jnp.dot(ya_ref[...], wa_ref[...], preferred_element_type=F32)
    r = jnp.dot(yr_ref[...], wr_ref[...], preferred_element_type=F32)
    o_ref[...] = (ga * a + gr * r).astype(o_ref.dtype)


def merge_branches(u, ya, yr, w_gate, wa, wr, tm=1024, tn=512):
    m, d = u.shape
    ka = ya.shape[1]
    kr = yr.shape[1]
    n = wa.shape[1]
    tm = min(tm, m)
    nj = n // tn
    return pl.pallas_call(
        _merge_kernel,
        grid=(m // tm, nj),
        in_specs=[pl.BlockSpec((tm, d), lambda i, j: (i, 0)),
                  pl.BlockSpec((tm, ka), lambda i, j: (i, 0)),
                  pl.BlockSpec((tm, kr), lambda i, j: (i, 0)),
                  pl.BlockSpec((d, tn), lambda i, j: (0, j)),
                  pl.BlockSpec((d, tn), lambda i, j: (0, j + nj)),
                  pl.BlockSpec((ka, tn), lambda i, j: (0, j)),
                  pl.BlockSpec((kr, tn), lambda i, j: (0, j))],
        out_specs=pl.BlockSpec((tm, tn), lambda i, j: (i, j)),
        out_shape=jax.ShapeDtypeStruct((m, n), BF16),
        compiler_params=_params(("parallel", "parallel")),
        name="merge_branches",
    )(u, ya, yr, w_gate, w_gate, wa, wr)


def _router_kernel(u_ref, w_ref, b_ref, idx_ref, gs_ref, rank_ref, cnt_ref, carry_ref, *, tm):
    i = pl.program_id(0)

    @pl.when(i == 0)
    def _():
        carry_ref[...] = jnp.zeros_like(carry_ref)

    logits = jnp.dot(u_ref[...], w_ref[...], precision=HI, preferred_element_type=F32)
    scores = _sigmoid(logits)
    cur = scores + b_ref[...]
    lane = lax.broadcasted_iota(jnp.int32, (tm, N_EXPERTS), 1).astype(F32)
    lane_out = lax.broadcasted_iota(jnp.int32, (tm, LANES), 1)
    picks = []
    sel_f = jnp.zeros((tm, N_EXPERTS), F32)
    idx_out = jnp.zeros((tm, LANES), F32)
    for k in range(TOP_K):
        best = jnp.max(cur, axis=-1, keepdims=True)
        ik = jnp.min(jnp.where(cur == best, lane, float(N_EXPERTS)), axis=-1, keepdims=True)
        onehot = lane == ik
        picks.append(onehot)
        sel_f = jnp.where(onehot, 1.0, sel_f)
        cur = jnp.where(onehot, -jnp.inf, cur)
        idx_out = jnp.where(lane_out == k, ik, idx_out)
    gsel = scores * sel_f
    gsel = gsel / jnp.sum(gsel, axis=-1, keepdims=True) * ROUTED_SCALE
    ri = lax.broadcasted_iota(jnp.int32, (tm, tm), 0)
    ci = lax.broadcasted_iota(jnp.int32, (tm, tm), 1)
    before = jnp.where(ri > ci, 1.0, 0.0).astype(BF16)
    rank =jnp.dot(before, sel_f.astype(BF16), preferred_element_type=F32) + carry_ref[...]
    carry_ref[...] = carry_ref[...] + jnp.sum(sel_f, axis=0, keepdims=True)
    cnt_ref[...] = carry_ref[...].astype(jnp.int32)
    gs_out = jnp.zeros((tm, LANES), F32)
    rank_out = jnp.zeros((tm, LANES), F32)
    for k in range(TOP_K):
        gk = jnp.sum(jnp.where(picks[k], gsel, 0.0), axis=-1, keepdims=True)
        rk = jnp.sum(jnp.where(picks[k], rank, 0.0), axis=-1, keepdims=True)
        gs_out = jnp.where(lane_out == k, gk, gs_out)
        rank_out = jnp.where(lane_out == k, rk, rank_out)
    idx_ref[...] = idx_out.astype(jnp.int32)
    gs_ref[...] = gs_out
    rank_ref[...] = rank_out.astype(jnp.int32)


def router(u, w_router, b_router, tm=256):
    n, d = u.shape
    tm = min(tm, n)
    tile = pl.BlockSpec((tm, LANES), lambda i: (i, 0))
    return pl.pallas_call(
        functools.partial(_router_kernel, tm=tm),
        grid=(n // tm,),
        in_specs=[pl.BlockSpec((tm, d), lambda i: (i, 0)),
                  pl.BlockSpec((d, N_EXPERTS), lambda i: (0, 0)),
                  pl.BlockSpec((1, N_EXPERTS), lambda i: (0, 0))],
        out_specs=[tile, tile, tile, pl.BlockSpec((1, N_EXPERTS), lambda i: (0, 0))],
        out_shape=[jax.ShapeDtypeStruct((n, LANES), jnp.int32),
                   jax.ShapeDtypeStruct((n, LANES), F32),
                   jax.ShapeDtypeStruct((n, LANES), jnp.int32),
                   jax.ShapeDtypeStruct((1, N_EXPERTS), jnp.int32)],
        scratch_shapes=[pltpu.VMEM((1, N_EXPERTS), F32)],
        compiler_params=_params(("arbitrary",)),
        name="router",
    )(u, w_router.astype(F32), b_router.reshape(1, -1).astype(F32))


def _dispatch_kernel(dest_ref, tail_ref, nused_ref, u_ref, xs_ref, zbuf, zsem, sem, *, td, bm):
    step = pl.program_id(0)
    n_blocks = xs_ref.shape[0] // bm

    @pl.when(step == 0)
    def _():
        zbuf[...] = jnp.zeros_like(zbuf)

        def zero_block(blk):
            return pltpu.make_async_copy(zbuf, xs_ref.at[pl.ds(blk * bm, bm)], zsem)

        def tails(e, carry):
            @pl.when(tail_ref[e] >= 0)
            def _():
                zero_block(tail_ref[e]).start()
            return carry

        def tails_wait(e, carry):
            @pl.when(tail_ref[e] >= 0)
            def _():
                zero_block(tail_ref[e]).wait()
            return carry

        def unused(blk, carry):
            zero_block(blk).start()
            return carry

        def unused_wait(blk, carry):
            zero_block(blk).wait()
            return carry

        lax.fori_loop(0, N_EXPERTS, tails, 0)
        lax.fori_loop(nused_ref[0], n_blocks, unused, 0)
        lax.fori_loop(0, N_EXPERTS, tails_wait, 0)
        lax.fori_loop(nused_ref[0], n_blocks, unused_wait, 0)

    base = step * (td * TOP_K)

    def row(i, carry):
        src = u_ref.at[pl.ds(i, 1)]
        for k in range(TOP_K):
            pltpu.make_async_copy(src, xs_ref.at[pl.ds(dest_ref[base + i * TOP_K + k], 1)], sem).start()
        return carry

    lax.fori_loop(0, td, row, 0)
    tile = xs_ref.at[pl.ds(0, td * TOP_K)]
    pltpu.make_async_copy(tile, tile, sem).wait()


def dispatch(u, dest, tail_blk, n_used, cap, bm, td=64):
    n, d = u.shape
    td = min(td, n)
    grid_spec = pltpu.PrefetchScalarGridSpec(
        num_scalar_prefetch=3,
        grid=(n // td,),
        in_specs=[pl.BlockSpec((td, d), lambda i, *_: (i, 0))],
        out_specs=pl.BlockSpec(memory_space=pl.ANY),
        scratch_shapes=[pltpu.VMEM((bm, d), u.dtype), pltpu.SemaphoreType.DMA(()),
                        pltpu.SemaphoreType.DMA(())],
    )
    return pl.pallas_call(
        functools.partial(_dispatch_kernel, td=td, bm=bm),
        grid_spec=grid_spec,
        out_shape=jax.ShapeDtypeStruct((cap, d), u.dtype),
        compiler_params=_params(("arbitrary",)),
        name="moe_dispatch",
    )(dest.reshape(-1), tail_blk, n_used, u)


def _expert_kernel(be_ref, nused_ref, x_ref, wg_ref, wu_ref, wd_ref, y_ref, wg_bf, wu_bf, wd_bf):
    i = pl.program_id(0)

    @pl.when(jnp.logical_or(i == 0, be_ref[i] != be_ref[jnp.maximum(i - 1, 0)]))
    def _():
        wg_bf[...] = wg_ref[...].astype(BF16)
        wu_bf[...] = wu_ref[...].astype(BF16)
        wd_bf[...] = wd_ref[...].astype(BF16)

    @pl.when(i < nused_ref[0])
    def _():
        x = x_ref[...].astype(BF16)
        hg = jnp.dot(x, wg_bf[...], preferred_element_type=F32)
        hu = jnp.dot(x, wu_bf[...], preferred_element_type=F32)
        hb = (hg * _sigmoid(hg) * hu).astype(BF16)
        y_ref[...] = jnp.dot(hb, wd_bf[...], preferred_element_type=F32)

    @pl.when(i >= nused_ref[0])
    def _():
        y_ref[...] = jnp.zeros_like(y_ref)


def expert_ffn(xs, block_e, n_used, wg, wu, wd, bm):
    cap, d = xs.shape
    f = wg.shape[2]
    row = lambda i, be, nu: (jnp.maximum(jnp.minimum(i, nu[0] - 1), 0), 0)
    grid_spec = pltpu.PrefetchScalarGridSpec(
        num_scalar_prefetch=2,
        grid=(cap // bm,),
        in_specs=[pl.BlockSpec((bm, d), row),
                  pl.BlockSpec((None, d, f), lambda i, be, nu: (be[i], 0, 0)),
                  pl.BlockSpec((None, d, f), lambda i, be, nu: (be[i], 0, 0)),
                  pl.BlockSpec((None, f, d), lambda i, be, nu: (be[i], 0, 0))],
        out_specs=pl.BlockSpec((bm, d), lambda i, be, nu: (i, 0)),
        scratch_shapes=[pltpu.VMEM((d, f), BF16), pltpu.VMEM((d, f), BF16), pltpu.VMEM((f, d), BF16)],
    )
    return pl.pallas_call(
        _expert_kernel,
        grid_spec=grid_spec,
        out_shape=jax.ShapeDtypeStruct((cap, d), F32),
        compiler_params=_params(("arbitrary",)),
        name="moe_experts",
    )(block_e, n_used, xs, wg, wu, wd)


def _shared_kernel(u_ref, h_ref, wg_ref, wu_ref, wd_ref, o_ref):
    x = u_ref[...].astype(BF16)
    hg = jnp.dot(x, wg_ref[...], preferred_element_type=F32)
    hu = jnp.dot(x, wu_ref[...], preferred_element_type=F32)
    hb = (hg * _sigmoid(hg) * hu).astype(BF16)
    o_ref[...] = h_ref[...] + jnp.dot(hb, wd_ref[...], preferred_element_type=F32)


def shared_ffn(u, h, wg, wu, wd, tm=512):
    n, d = u.shape
    f = wg.shape[1]
    tm = min(tm, n)
    tile = pl.BlockSpec((tm, d), lambda i: (i, 0))
    return pl.pallas_call(
        _shared_kernel,
        grid=(n // tm,),
        in_specs=[tile, tile,
                  pl.BlockSpec((d, f), lambda i: (0, 0)),
                  pl.BlockSpec((d, f), lambda i: (0, 0)),
                  pl.BlockSpec((f, d), lambda i: (0, 0))],
        out_specs=tile,
        out_shape=jax.ShapeDtypeStruct((n, d), F32),
        compiler_params=_params(("parallel",)),
        name="shared_ffn",
    )(u, h, wg, wu, wd)


def _combine_kernel(dest_ref, gs_ref, hs_ref, g_ref, y_ref, h_out, u_out, buf, sems, *, tt):
    step = pl.program_id(0)
    slot = step % 2

    def issue(tile, to_slot):
        base = tile * (tt * TOP_K)

        def row(r, carry):
            for k in range(TOP_K):
                pltpu.make_async_copy(y_ref.at[pl.ds(dest_ref[base + r * TOP_K + k], 1)],
                                      buf.at[to_slot, k, pl.ds(r, 1)], sems.at[to_slot]).start()
            return carry

        lax.fori_loop(0, tt, row, 0)

    @pl.when(step == 0)
    def _():
        issue(0, 0)

    @pl.when(step + 1 < pl.num_programs(0))
    def _():
        issue(step + 1, 1 - slot)

    pltpu.make_async_copy(buf.at[slot], buf.at[slot], sems.at[slot]).wait()
    gs = gs_ref[...]
    h = hs_ref[...]
    for k in range(TOP_K):
        h = h + gs[:, k:k + 1] * buf[slot, k]
    h_out[...] = h
    ms = jnp.mean(h * h, axis=-1, keepdims=True)
    u_out[...] = (h * lax.rsqrt(ms + RMS_EPS) * g_ref[...]).astype(u_out.dtype)


def combine(y, dest, gsel, hs, g_next, tt=32):
    n, d = hs.shape
    tt = min(tt, n)
    tile = pl.BlockSpec((tt, d), lambda i, s: (i, 0))
    grid_spec = pltpu.PrefetchScalarGridSpec(
        num_scalar_prefetch=1,
        grid=(n // tt,),
        in_specs=[pl.BlockSpec((tt, LANES), lambda i, s: (i, 0)),
                  tile,
                  pl.BlockSpec((1, d), lambda i, s: (0, 0)),
                  pl.BlockSpec(memory_space=pl.ANY)],
        out_specs=[tile, tile],
        scratch_shapes=[pltpu.VMEM((2, TOP_K, tt, d), F32), pltpu.SemaphoreType.DMA((2,))],
    )
    return pl.pallas_call(
        functools.partial(_combine_kernel, tt=tt),
        grid_spec=grid_spec,
        out_shape=[jax.ShapeDtypeStruct((n, d), F32), jax.ShapeDtypeStruct((n, d), BF16)],
        compiler_params=_params(("arbitrary",)),
        name="moe_combine",
    )(dest.reshape(-1), gsel, hs, g_next.reshape(1, -1).astype(F32), y)


def moe_ffn(h, g_norm, w_router, b_router, w_e_gate, w_e_up, w_e_down, w_s_gate, w_s_up, w_s_down,
            g_next, bm=EXPERT_BLOCK):
    n, d = h.shape
    u = rmsnorm(h, g_norm, F32)
    u_bf = u.astype(BF16)
    idx128, gs128, rank128, counts = router(u, w_router, b_router)
    idx = idx128[:, :TOP_K]
    rank = rank128[:, :TOP_K]
    counts = counts.reshape(-1)
    n_blk = (counts + bm - 1) // bm
    blk_end = jnp.cumsum(n_blk)
    pstart = (blk_end - n_blk) * bm
    experts = jnp.arange(N_EXPERTS, dtype=jnp.int32)
    dest = rank + jnp.sum(jnp.where(idx[:, :, None] == experts, pstart, 0), axis=-1)
    dest = dest.astype(jnp.int32)
    cap = (n * TOP_K // bm + N_EXPERTS) * bm
    blocks = jnp.arange(cap // bm, dtype=jnp.int32)
    block_e = jnp.minimum(jnp.sum(blk_end[None, :] <= blocks[:, None], axis=1), N_EXPERTS - 1)
    block_e = block_e.astype(jnp.int32)
    n_used = blk_end[-1:].astype(jnp.int32)
    tail_blk = jnp.where(n_blk > 0, blk_end - 1, -1).astype(jnp.int32)
    xs = dispatch(u, dest, tail_blk, n_used, cap, bm)
    y = expert_ffn(xs, block_e, n_used, w_e_gate.astype(F32), w_e_up.astype(F32), w_e_down.astype(F32), bm)
    hs = shared_ffn(u_bf, h, w_s_gate.astype(BF16), w_s_up.astype(BF16), w_s_down.astype(BF16))
    return combine(y, dest, gs128, hs, g_next)


def _ple_kernel(u_ref, p_ref, h_ref, wg_ref, wp_ref, gf_ref, o_ref):
    gate = _sigmoid(jnp.dot(u_ref[...], wg_ref[...], preferred_element_type=F32))
    proj = jnp.dot(p_ref[...].astype(BF16), wp_ref[...], preferred_element_type=F32)
    h = h_ref[...] + gate * proj
    ms = jnp.mean(h * h, axis=-1, keepdims=True)
    o_ref[...] = h * lax.rsqrt(ms + RMS_EPS) * gf_ref[...]


def ple_final(u, p, h, w_gate, w_proj, g_final, tm=256):
    n, d = h.shape
    pd = p.shape[1]
    tm = min(tm, n)
    tile = pl.BlockSpec((tm, d), lambda i: (i, 0))
    return pl.pallas_call(
        _ple_kernel,
        grid=(n // tm,),
        in_specs=[tile, pl.BlockSpec((tm, pd), lambda i: (i, 0)), tile,
                  pl.BlockSpec((d, d), lambda i: (0, 0)),
                  pl.BlockSpec((pd, d), lambda i: (0, 0)),
                  pl.BlockSpec((1, d), lambda i: (0, 0))],
        out_specs=tile,
        out_shape=jax.ShapeDtypeStruct((n, d), F32),
        compiler_params=_params(("parallel",)),
        name="ple_final",
    )(u, p, h, w_gate, w_proj, g_final.reshape(1, -1).astype(F32))


def token_mixing(x2, bsz, seq, norm_g, w_in, w_branch_gate, lq1, lk1, lq2, lk2, subln_g, shift_prev,
                 shift_next, w0, w_decay_up, a0, w_iclr_up, w_gate_up, k_k, k_a, r_k, ln_x_g, ln_x_b,
                 w_br_attn, w_br_rwkv, w_out, lambda_init):
    d = x2.shape[1]
    u = rmsnorm(x2, norm_g, BF16)
    w_attn = w_in[:, :A_COLS].astype(BF16)
    w_rwkv = jnp.pad(w_in[:, A_COLS:], ((0, 0), (0, R_COLS_PAD - R_COLS))).astype(BF16)
    za = matmul(u, w_attn, out_dtype=BF16, epilogue=functools.partial(_scale_q_epilogue, tn=512),
                tm=2048, tn=512, name="proj_attn")
    zr = matmul(u, w_rwkv, out_dtype=F32, tm=2048, tn=512, name="proj_rwkv")
    ya = diff_attention(za.reshape(bsz, seq, A_COLS), lq1, lk1, lq2, lk2, subln_g, lambda_init)
    r, kk, v, lw, b, kt, bonus, g = rwkv_prep(
        zr.reshape(bsz, seq, R_COLS_PAD), shift_prev, shift_next, w0, w_decay_up, a0, w_iclr_up,
        w_gate_up, k_k, k_a, r_k)
    y = rwkv_scan(r, kk, v, lw, b, kt)
    yr = rwkv_post(y, bonus, g, ln_x_g, ln_x_b)
    merged = merge_branches(u, ya.reshape(-1, A_WIDTH), yr.reshape(-1, R_WIDTH),
                            w_branch_gate.astype(BF16), w_br_attn.astype(BF16), w_br_rwkv.astype(BF16))
    return matmul(merged, w_out.astype(BF16), out_dtype=F32, epilogue=_residual_add, extras=(x2,),
                  tm=2048, tn=512, name="out_proj")


def kernel(x, p, norm_mix_g, w_in, w_branch_gate, lambda_q1, lambda_k1, lambda_q2, lambda_k2, subln_g, shift_prev, shift_next, w0, w_decay_up, a0, w_iclr_up, w_gate_up, k_k, k_a, r_k, ln_x_g, ln_x_b, w_br_attn, w_br_rwkv, w_out, norm_ffn_g, w_router, b_router, w_e_gate, w_e_up, w_e_down, w_s_gate, w_s_up, w_s_down, norm_ple_g, w_ple_gate, w_ple_proj, norm_final_g):
    bsz, seq, d = x.shape
    depth = w_in.shape[0]
    assert depth == 1, "the final norm is fused into the last layer's embedding kernel"
    h = x.reshape(bsz * seq, d)
    i = 0
    lambda_init = 0.8 - 0.6 * math.exp(-0.3 * i)
    h = token_mixing(h, bsz, seq, norm_mix_g[i], w_in[i], w_branch_gate[i], lambda_q1[i], lambda_k1[i],
                     lambda_q2[i], lambda_k2[i], subln_g[i], shift_prev[i], shift_next[i], w0[i],
                     w_decay_up[i], a0[i], w_iclr_up[i], w_gate_up[i], k_k[i], k_a[i], r_k[i],
                     ln_x_g[i], ln_x_b[i], w_br_attn[i], w_br_rwkv[i], w_out[i], lambda_init)
    h, u = moe_ffn(h, norm_ffn_g[i], w_router[i], b_router[i], w_e_gate[i], w_e_up[i], w_e_down[i],
                   w_s_gate[i], w_s_up[i], w_s_down[i], norm_ple_g[i])
    out = ple_final(u, p[i].reshape(bsz * seq, -1), h, w_ple_gate[i].astype(BF16),
                    w_ple_proj[i].astype(BF16), norm_final_g)
    return out.reshape(bsz, seq, d)
```

```python
import functools
import math

import jax
import jax.numpy as jnp
from jax import lax
from jax.experimental import pallas as pl
from jax.experimental.pallas import tpu as pltpu

F32 = jnp.float32
BF16 = jnp.bfloat16
HI = lax.Precision.HIGHEST

D_MODEL = 2048
PLE_DIM = 256
A_HEADS = 8
A_QK_DIM = 64
A_V_DIM = 2 * A_QK_DIM
A_WIDTH = A_HEADS * A_V_DIM
R_HEADS = 16
R_HEAD = 64
R_WIDTH = R_HEADS * R_HEAD
DECAY_LORA = 64
ICLR_LORA = 64
GATE_LORA = 160
N_EXPERTS = 64
TOP_K = 8
EXPERT_FF = 512
SHARED_FF = 512
ROUTED_SCALE = 2.5
RMS_EPS = 1e-6
GN_EPS = 64e-5
LOG2E = math.log2(math.e)
Q_COLS = A_HEADS * 2 * A_QK_DIM
A_COLS = 2 * Q_COLS + A_WIDTH
R_COLS = 3 * R_WIDTH + 2 * DECAY_LORA + 2 * ICLR_LORA + GATE_LORA
R_COLS_PAD = 3584
LANES = 128
CHUNK = 64
EXPERT_BLOCK = 256
VMEM_LIMIT = 56 * 1024 * 1024


def _params(sem):
    return pltpu.CompilerParams(dimension_semantics=sem, vmem_limit_bytes=VMEM_LIMIT)


def _sigmoid(x):
    return 1.0 / (1.0 + jnp.exp(-x))


def _rmsnorm_kernel(x_ref, g_ref, o_ref):
    x = x_ref[...].astype(F32)
    ms = jnp.mean(x * x, axis=-1, keepdims=True)
    o_ref[...] = (x * lax.rsqrt(ms + RMS_EPS) * g_ref[...]).astype(o_ref.dtype)


def rmsnorm(x, g, out_dtype, tm=512):
    m, d = x.shape
    tm = min(tm, m)
    return pl.pallas_call(
        _rmsnorm_kernel,
        grid=(m // tm,),
        in_specs=[pl.BlockSpec((tm, d), lambda i: (i, 0)),
                  pl.BlockSpec((1, d), lambda i: (0, 0))],
        out_specs=pl.BlockSpec((tm, d), lambda i: (i, 0)),
        out_shape=jax.ShapeDtypeStruct((m, d), out_dtype),
        compiler_params=_params(("parallel",)),
        name="rmsnorm",
    )(x, g.reshape(1, d).astype(F32))


def _mm_kernel(x_ref, w_ref, *rest, epilogue):
    o_ref = rest[-1]
    acc = jnp.dot(x_ref[...], w_ref[...], preferred_element_type=F32)
    if epilogue is not None:
        acc = epilogue(acc, *[e[...] for e in rest[:-1]])
    o_ref[...] = acc.astype(o_ref.dtype)


def matmul(x, w, *, out_dtype, epilogue=None, extras=(), tm=512, tn=512, name="matmul"):
    m, k = x.shape
    n = w.shape[1]
    tm = min(tm, m)
    tn = min(tn, n)
    in_specs = [pl.BlockSpec((tm, k), lambda i, j: (i, 0)),
                pl.BlockSpec((k, tn), lambda i, j: (0, j))]
    in_specs += [pl.BlockSpec((tm, tn), lambda i, j: (i, j)) for _ in extras]
    return pl.pallas_call(
        functools.partial(_mm_kernel, epilogue=epilogue),
        grid=(m // tm, pl.cdiv(n, tn)),
        in_specs=in_specs,
        out_specs=pl.BlockSpec((tm, tn), lambda i, j: (i, j)),
        out_shape=jax.ShapeDtypeStruct((m, n), out_dtype),
        compiler_params=_params(("parallel", "parallel")),
        name=name,
    )(x, w, *extras)


def _residual_add(acc, res):
    return res + acc


def _scale_q_epilogue(acc, *, tn):
    is_q = pl.program_id(1) < Q_COLS // tn
    return acc * jnp.where(is_q, LOG2E * A_QK_DIM ** -0.5, 1.0)


def _attn_kernel(slopes_ref, q_ref, k_ref, v_ref, lq1_ref, lk1_ref, lq2_ref, lk2_ref, sg_ref,
                 o_ref, bias_ref, s_ref, *, tq, seq, kc, lambda_init):
    h = pl.program_id(0)
    qi = pl.program_id(1)

    @pl.when(pl.program_id(2) == 0)
    def _():
        qpos = qi * tq + lax.broadcasted_iota(jnp.int32, (tq, seq), 0)
        kpos = lax.broadcasted_iota(jnp.int32, (tq, seq), 1)
        bias_ref[...] = (-LOG2E * slopes_ref[h]) * jnp.abs(qpos - kpos).astype(F32)

    lane = lax.broadcasted_iota(jnp.int32, (tq, LANES), 1)
    q = q_ref[...]
    zero = jnp.zeros_like(q)
    qm = [jnp.where(lane < A_QK_DIM, q, zero), jnp.where(lane < A_QK_DIM, zero, q)]
    chunks = [slice(c * kc, (c + 1) * kc) for c in range(seq // kc)]
    dims = (((1,), (1,)), ((), ()))

    def scores(m, row_max):
        for sl in chunks:
            sc = lax.dot_general(qm[m], k_ref[sl, :], dims, preferred_element_type=F32) + bias_ref[:, sl]
            s_ref[m, :, sl] = sc
            cmax = jnp.max(sc, axis=-1, keepdims=True)
            row_max = cmax if row_max is None else jnp.maximum(row_max, cmax)
        return row_max

    def weighted_values(m, row_max):
        l = jnp.zeros((tq, 1), F32)
        acc = jnp.zeros((tq, LANES), F32)
        for sl in chunks:
            e = jnp.exp2(s_ref[m, :, sl] - row_max)
            l = l + jnp.sum(e, axis=-1, keepdims=True)
            acc = acc + jnp.dot(e.astype(BF16), v_ref[sl, :], preferred_element_type=F32)
        return acc / l

    max0 = scores(0, None)
    max1 = scores(1, None)
    o0 = weighted_values(0, max0)
    o1 = weighted_values(1, max1)
    lam = (jnp.exp(jnp.sum(lq1_ref[...] * lk1_ref[...], axis=-1, keepdims=True))
           - jnp.exp(jnp.sum(lq2_ref[...] * lk2_ref[...], axis=-1, keepdims=True))
           + lambda_init)
    o = o0 - lam * o1
    ms = jnp.mean(o * o, axis=-1, keepdims=True)
    o = o * lax.rsqrt(ms + RMS_EPS) * sg_ref[...]
    o_ref[...] = (o * (1.0 - lambda_init)).astype(o_ref.dtype)


def diff_attention(za, lq1, lk1, lq2, lk2, subln_g, lambda_init, tq=256):
    b, t, _ = za.shape
    tq = min(tq, t)
    h = jnp.arange(1, A_HEADS + 1, dtype=F32)
    slopes = jnp.exp2(-8.0 * h / A_HEADS)
    vec = lambda a: a.reshape(1, -1).astype(F32)
    small = lambda n: pl.BlockSpec((1, n), lambda hi, qi, bi, s: (0, 0))
    grid_spec = pltpu.PrefetchScalarGridSpec(
        num_scalar_prefetch=1,
        grid=(A_HEADS, t // tq, b),
        in_specs=[
            pl.BlockSpec((None, tq, LANES), lambda hi, qi, bi, s: (bi, qi, hi)),
            pl.BlockSpec((None, t, LANES), lambda hi, qi, bi, s: (bi, 0, A_HEADS + hi)),
            pl.BlockSpec((None, t, LANES), lambda hi, qi, bi, s: (bi, 0, 2 * A_HEADS + hi)),
            small(A_QK_DIM), small(A_QK_DIM), small(A_QK_DIM), small(A_QK_DIM), small(A_V_DIM),
        ],
        out_specs=pl.BlockSpec((None, tq, LANES), lambda hi, qi, bi, s: (bi, qi, hi)),
        scratch_shapes=[pltpu.VMEM((tq, t), F32), pltpu.VMEM((2, tq, t), F32)],
    )
    return pl.pallas_call(
        functools.partial(_attn_kernel, tq=tq, seq=t, kc=min(512, t), lambda_init=lambda_init),
        grid_spec=grid_spec,
        out_shape=jax.ShapeDtypeStruct((b, t, A_WIDTH), BF16),
        compiler_params=_params(("parallel", "parallel", "arbitrary")),
        name="diff_attention",
    )(slopes, za, za, za, vec(lq1), vec(lk1), vec(lq2), vec(lk2), vec(subln_g))


def _head_sum(x):
    ri = lax.broadcasted_iota(jnp.int32, (LANES, LANES), 0) // R_HEAD
    ci = lax.broadcasted_iota(jnp.int32, (LANES, LANES), 1) // R_HEAD
    ones = (ri == ci).astype(F32)
    return jnp.dot(x, ones, precision=HI, preferred_element_type=F32)


def _head_sum_wide(x):
    return jnp.concatenate(
        [_head_sum(x[:, p * LANES:(p + 1) * LANES]) for p in range(x.shape[1] // LANES)], axis=1)


def _rwkv_prep_kernel(z_ref, zprev_ref, znext_ref, mup_ref, mun_ref, w0_ref, wdu_ref, a0_ref,
                      wiu_ref, wgu_ref, kk_ref, ka_ref, rk_ref,
                      r_out, kkn_out, v_out, lw_out, b_out, kt_out, bonus_out, g_out, *, tt):
    i = pl.program_id(1)
    n_i = pl.num_programs(1)
    c = R_WIDTH
    z = z_ref[...]
    row = lax.broadcasted_iota(jnp.int32, (tt, 1), 0)
    prev_row = jnp.where(i > 0, zprev_ref[7:8, :], 0.0)
    next_row = jnp.where(i < n_i - 1, znext_ref[0:1, :], 0.0)
    zp = jnp.where(row == 0, prev_row, pltpu.roll(z, 1, 0))
    zn = jnp.where(row == tt - 1, next_row, pltpu.roll(z, tt - 1, 0))
    zs = z + mup_ref[...] * (zp - z) + mun_ref[...] * (zn - z)

    r = zs[:, :c]
    k = zs[:, c:2 * c]
    v = zs[:, 2 * c:3 * c]
    o3 = 3 * c
    lane = lax.broadcasted_iota(jnp.int32, (tt, LANES), 1)
    first = lane < DECAY_LORA
    lw = jnp.tanh(zs[:, o3:o3 + LANES])
    la = zs[:, o3 + LANES:o3 + 2 * LANES]
    lg = _sigmoid(zs[:, o3 + 2 * LANES:o3 + 4 * LANES])
    g_out[...] = jnp.dot(lg.astype(BF16), wgu_ref[...], preferred_element_type=F32)

    kk = k * kk_ref[...]
    ss = _head_sum_wide(kk * kk)
    kk = kk * lax.rsqrt(jnp.maximum(ss, 1e-12))
    r_out[...] = r
    kkn_out[...] = kk
    v_out[...] = v

    kt_sum = jnp.zeros_like(k)
    for d in range(2):
        keep = first if d == 0 else jnp.logical_not(first)
        lw_d = jnp.where(keep, lw, 0.0).astype(BF16)
        la_d = jnp.where(keep, la, 0.0).astype(BF16)
        zw = jnp.dot(lw_d, wdu_ref[...], preferred_element_type=F32) + w0_ref[d:d + 1, :]
        lw_out[d] = -math.exp(-0.5) * _sigmoid(zw)
        a = _sigmoid(jnp.dot(la_d, wiu_ref[...], preferred_element_type=F32) + a0_ref[d:d + 1, :])
        kt = k * (1.0 + (a - 1.0) * ka_ref[...])
        kt_out[d] = kt
        b_out[d] = kk * a
        kt_sum = kt_sum + kt
    coef = _head_sum_wide(r * kt_sum * rk_ref[...])
    bonus_out[...] = coef * v


def rwkv_prep(zr, shift_prev, shift_next, w0, w_decay_up, a0, w_iclr_up, w_gate_up, k_k, k_a, r_k,
              tt=128):
    b, t, cp = zr.shape
    tt = min(tt, t)
    c = R_WIDTH
    pad = cp - R_COLS
    row = lambda a: a.reshape(1, -1).astype(F32)
    mup = jnp.pad(row(shift_prev), ((0, 0), (0, pad)))
    mun = jnp.pad(row(shift_next), ((0, 0), (0, pad)))
    wdu = w_decay_up.reshape(2 * DECAY_LORA, c).astype(BF16)
    wiu = w_iclr_up.reshape(2 * ICLR_LORA, c).astype(BF16)
    wgu = jnp.pad(w_gate_up, ((0, 2 * LANES - GATE_LORA), (0, 0))).astype(BF16)
    nb8 = t // 8
    const = lambda shape: pl.BlockSpec(shape, lambda bi, i: (0,) * len(shape))
    in_specs = [
        pl.BlockSpec((None, tt, cp), lambda bi, i: (bi, i, 0)),
        pl.BlockSpec((None, 8, cp), lambda bi, i: (bi, jnp.maximum(i * (tt // 8) - 1, 0), 0)),
        pl.BlockSpec((None, 8, cp), lambda bi, i: (bi, jnp.minimum((i + 1) * (tt // 8), nb8 - 1), 0)),
        const((1, cp)), const((1, cp)), const((2, c)), const((2 * DECAY_LORA, c)), const((2, c)),
        const((2 * ICLR_LORA, c)), const((2 * LANES, c)), const((1, c)), const((1, c)), const((1, c)),
    ]
    one = pl.BlockSpec((None, tt, c), lambda bi, i: (bi, i, 0))
    two = pl.BlockSpec((2, None, tt, c), lambda bi, i: (0, bi, i, 0))
    s1 = jax.ShapeDtypeStruct((b, t, c), F32)
    s2 = jax.ShapeDtypeStruct((2, b, t, c), F32)
    return pl.pallas_call(
        functools.partial(_rwkv_prep_kernel, tt=tt),
        grid=(b, t // tt),
        in_specs=in_specs,
        out_specs=[one, one, one, two, two, two, one, one],
        out_shape=[s1, s1, s1, s2, s2, s2, s1, s1],
        compiler_params=_params(("parallel", "parallel")),
        name="rwkv_prep",
    )(zr, zr, zr, mup, mun, w0.astype(F32), wdu, a0.astype(F32), wiu, wgu, row(k_k), row(k_a), row(r_k))


def _scan_kernel(r0_ref, kk0_ref, v0_ref, r1_ref, kk1_ref, v1_ref, lw0_ref, b0_ref, kt0_ref,
                 lw1_ref, b1_ref, kt1_ref, y0_ref, y1_ref, s_ref, *, chunk):
    ln = chunk

    @pl.when(pl.program_id(1) == 0)
    def _():
        s_ref[...] = jnp.zeros_like(s_ref)

    ri = lax.broadcasted_iota(jnp.int32, (ln, ln), 0)
    ci = lax.broadcasted_iota(jnp.int32, (ln, ln), 1)
    r2 = lax.broadcasted_iota(jnp.int32, (2 * ln, 2 * ln), 0)
    c2 = lax.broadcasted_iota(jnp.int32, (2 * ln, 2 * ln), 1)
    eye = (r2 == c2).astype(F32)
    head0 = lax.broadcasted_iota(jnp.int32, (ln, LANES), 1) < R_HEAD
    lane_c = (((1,), (1,)), ((), ()))
    row_c = (((0,), (0,)), ((), ()))
    n_pairs = r0_ref.shape[1] // LANES

    def dot(a, b):
        return jnp.dot(a.astype(BF16), b.astype(BF16), preferred_element_type=F32)

    def dotg(a, b, dims):
        return lax.dot_general(a.astype(BF16), b.astype(BF16), dims, preferred_element_type=F32)

    def stack(x, sl):
        xs = x[:, sl]
        return jnp.concatenate([jnp.where(head0, xs, 0.0), jnp.where(head0, 0.0, xs)], axis=0)

    ar, bs, ks, vs, gt, strict, incl, out = [], [], [], [], [], [], [], []
    for d, (r_ref, kk_ref, v_ref, lw_ref, b_ref, kt_ref, y_ref) in enumerate((
            (r0_ref, kk0_ref, v0_ref, lw0_ref, b0_ref, kt0_ref, y0_ref),
            (r1_ref, kk1_ref, v1_ref, lw1_ref, b1_ref, kt1_ref, y1_ref))):
        earlier = (ri >= ci) if d == 0 else (ri <= ci)
        lw = lw_ref[...]
        cum = jnp.dot(earlier.astype(F32), lw, precision=HI, preferred_element_type=F32)
        g_inv = jnp.exp(-cum)
        rh = r_ref[...] * jnp.exp(cum)
        ah = kk_ref[...] * jnp.exp(cum - lw)
        bh = b_ref[...] * g_inv
        kh = kt_ref[...] * g_inv
        vv = v_ref[...]
        g_tot = jnp.exp(jnp.sum(lw, axis=0, keepdims=True))
        for p in range(n_pairs):
            sl = slice(p * LANES, (p + 1) * LANES)
            ar.append(jnp.concatenate([stack(ah, sl), stack(rh, sl)], axis=0).astype(BF16))
            bs.append(stack(bh, sl).astype(BF16))
            ks.append(stack(kh, sl).astype(BF16))
            vs.append(stack(vv, sl).astype(BF16))
            gt.append(g_tot[:, sl])
            strict.append((r2 > c2) if d == 0 else (r2 < c2))
            incl.append((r2 >= c2) if d == 0 else (r2 <= c2))
            out.append((y_ref, sl))
    jobs = range(len(ar))
    g1 = [dotg(ar[j], jnp.concatenate([bs[j], ks[j]], axis=0), lane_c) for j in jobs]
    n = [jnp.where(strict[j], g1[j][:2 * ln, :2 * ln], 0.0) for j in jobs]
    m_akrk = [jnp.concatenate([jnp.where(strict[j], g1[j][:2 * ln, 2 * ln:], 0.0),
                               jnp.where(incl[j], g1[j][2 * ln:, 2 * ln:], 0.0)], axis=0).astype(BF16)
              for j in jobs]
    m_rb = [jnp.where(incl[j], g1[j][2 * ln:, :2 * ln], 0.0).astype(BF16) for j in jobs]
    x = [eye - n[j] for j in jobs]
    pw = n
    for _ in range(int(math.log2(ln)) - 1):
        pw = [dot(pw[j], pw[j]) for j in jobs]
        x = [x[j] + dot(x[j], pw[j]) for j in jobs]
    s = [s_ref[j] for j in jobs]
    asrs = [dotg(ar[j], s[j], lane_c) for j in jobs]
    mv = [dot(m_akrk[j], vs[j]) for j in jobs]
    u = [dot(x[j], asrs[j][:2 * ln] + mv[j][:2 * ln]) for j in jobs]
    ys = [asrs[j][2 * ln:] + mv[j][2 * ln:] - dot(m_rb[j], u[j]) for j in jobs]
    for j in jobs:
        y_ref, sl = out[j]
        y_ref[:, sl] = ys[j][:ln] + ys[j][ln:]
    ds = [dotg(jnp.concatenate([vs[j], (-u[j]).astype(BF16)], axis=0),
               jnp.concatenate([ks[j], bs[j]], axis=0), row_c) for j in jobs]
    for j in jobs:
        s_ref[j] = (s[j] + ds[j]) * gt[j]


def rwkv_scan(r, kk, v, lw, b, kt, chunk=CHUNK):
    bsz, t, c = r.shape
    chunk = min(chunk, t)
    nc = t // chunk
    fwd = pl.BlockSpec((None, chunk, c), lambda bi, ci: (bi, ci, 0))
    bwd = pl.BlockSpec((None, chunk, c), lambda bi, ci: (bi, nc - 1 - ci, 0))
    fwd_d = pl.BlockSpec((None, None, chunk, c), lambda bi, ci: (0, bi, ci, 0))
    bwd_d = pl.BlockSpec((None, None, chunk, c), lambda bi, ci: (1, bi, nc - 1 - ci, 0))
    shape = jax.ShapeDtypeStruct((bsz, t, c), F32)
    return pl.pallas_call(
        functools.partial(_scan_kernel, chunk=chunk),
        grid=(bsz, nc),
        in_specs=[fwd, fwd, fwd, bwd, bwd, bwd, fwd_d, fwd_d, fwd_d, bwd_d, bwd_d, bwd_d],
        out_specs=[fwd, bwd],
        out_shape=[shape, shape],
        scratch_shapes=[pltpu.VMEM((2 * c // LANES, LANES, LANES), F32)],
        compiler_params=_params(("parallel", "arbitrary")),
        name="rwkv_scan",
    )(r, kk, v, r, kk, v, lw, b, kt, lw, b, kt)


def _rwkv_post_kernel(y0_ref, y1_ref, bonus_ref, g_ref, lng_ref, lnb_ref, o_ref):
    y = y0_ref[...] + y1_ref[...]
    mu = _head_sum_wide(y) * (1.0 / R_HEAD)
    yc = y - mu
    var = _head_sum_wide(yc * yc) * (1.0 / R_HEAD)
    yn = yc * lax.rsqrt(var + GN_EPS) * lng_ref[...] + lnb_ref[...]
    o_ref[...] = ((yn + bonus_ref[...]) * g_ref[...]).astype(o_ref.dtype)


def rwkv_post(y0, y1, bonus, g, ln_g, ln_b, tt=256):
    b, t, c = y0.shape
    tt = min(tt, t)
    row = lambda a: a.reshape(1, -1).astype(F32)
    one = pl.BlockSpec((None, tt, c), lambda bi, i: (bi, i, 0))
    const = pl.BlockSpec((1, c), lambda bi, i: (0, 0))
    return pl.pallas_call(
        _rwkv_post_kernel,
        grid=(b, t // tt),
        in_specs=[one, one, one, one, const, const],
        out_specs=one,
        out_shape=jax.ShapeDtypeStruct((b, t, c), BF16),
        compiler_params=_params(("parallel", "parallel")),
        name="rwkv_post",
    )(y0, y1, bonus, g, row(ln_g), row(ln_b))


def _merge_kernel(u_ref, ya_ref, yr_ref, wga_ref, wgr_ref, wa_ref, wr_ref, o_ref):
    u = u_ref[...]
    ga = _sigmoid(jnp.dot(u, wga_ref[...], preferred_element_type=F32))
    gr = _sigmoid(jnp.dot(u, wgr_ref[...], preferred_element_type=F32))
    a = jnp.dot(ya_ref[...], wa_ref[...], preferred_element_type=F32)
    r = jnp.dot(yr_ref[...], wr_ref[...], preferred_element_type=F32)
    o_ref[...] = (ga * a + gr * r).astype(o_ref.dtype)


def merge_branches(u, ya, yr, w_gate, wa, wr, tm=1024, tn=512):
    m, d = u.shape
    ka = ya.shape[1]
    kr = yr.shape[1]
    n = wa.shape[1]
    tm = min(tm, m)
    nj = n // tn
    return pl.pallas_call(
        _merge_kernel,
        grid=(m // tm, nj),
        in_specs=[pl.BlockSpec((tm, d), lambda i, j: (i, 0)),
                  pl.BlockSpec((tm, ka), lambda i, j: (i, 0)),
                  pl.BlockSpec((tm, kr), lambda i, j: (i, 0)),
                  pl.BlockSpec((d, tn), lambda i, j: (0, j)),
                  pl.BlockSpec((d, tn), lambda i, j: (0, j + nj)),
                  pl.BlockSpec((ka, tn), lambda i, j: (0, j)),
                  pl.BlockSpec((kr, tn), lambda i, j: (0, j))],
        out_specs=pl.BlockSpec((tm, tn), lambda i, j: (i, j)),
        out_shape=jax.ShapeDtypeStruct((m, n), BF16),
        compiler_params=_params(("parallel", "parallel")),
        name="merge_branches",
    )(u, ya, yr, w_gate, w_gate, wa, wr)


def _router_kernel(u_ref, w_ref, b_ref, idx_ref, gs_ref, rank_ref, cnt_ref, carry_ref, *, tm):
    i = pl.program_id(0)

    @pl.when(i == 0)
    def _():
        carry_ref[...] = jnp.zeros_like(carry_ref)

    logits = jnp.dot(u_ref[...], w_ref[...], precision=HI, preferred_element_type=F32)
    scores = _sigmoid(logits)
    cur = scores + b_ref[...]
    lane = lax.broadcasted_iota(jnp.int32, (tm, N_EXPERTS), 1).astype(F32)
    lane_out = lax.broadcasted_iota(jnp.int32, (tm, LANES), 1)
    picks = []
    sel_f = jnp.zeros((tm, N_EXPERTS), F32)
    idx_out = jnp.zeros((tm, LANES), F32)
    for k in range(TOP_K):
        best = jnp.max(cur, axis=-1, keepdims=True)
        ik = jnp.min(jnp.where(cur == best, lane, float(N_EXPERTS)), axis=-1, keepdims=True)
        onehot = lane == ik
        picks.append(onehot)
        sel_f = jnp.where(onehot, 1.0, sel_f)
        cur = jnp.where(onehot, -jnp.inf, cur)
        idx_out = jnp.where(lane_out == k, ik, idx_out)
    gsel = scores * sel_f
    gsel = gsel / jnp.sum(gsel, axis=-1, keepdims=True) * ROUTED_SCALE
    ri = lax.broadcasted_iota(jnp.int32, (tm, tm), 0)
    ci = lax.broadcasted_iota(jnp.int32, (tm, tm), 1)
    before = jnp.where(ri > ci, 1.0, 0.0).astype(BF16)
    rank =jnp.dot(before, sel_f.astype(BF16), preferred_element_type=F32) + carry_ref[...]
    carry_ref[...] = carry_ref[...] + jnp.sum(sel_f, axis=0, keepdims=True)
    cnt_ref[...] = carry_ref[...].astype(jnp.int32)
    gs_out = jnp.zeros((tm, LANES), F32)
    rank_out = jnp.zeros((tm, LANES), F32)
    for k in range(TOP_K):
        gk = jnp.sum(jnp.where(picks[k], gsel, 0.0), axis=-1, keepdims=True)
        rk = jnp.sum(jnp.where(picks[k], rank, 0.0), axis=-1, keepdims=True)
        gs_out = jnp.where(lane_out == k, gk, gs_out)
        rank_out = jnp.where(lane_out == k, rk, rank_out)
    idx_ref[...] = idx_out.astype(jnp.int32)
    gs_ref[...] = gs_out
    rank_ref[...] = rank_out.astype(jnp.int32)


def router(u, w_router, b_router, tm=256):
    n, d = u.shape
    tm = min(tm, n)
    tile = pl.BlockSpec((tm, LANES), lambda i: (i, 0))
    return pl.pallas_call(
        functools.partial(_router_kernel, tm=tm),
        grid=(n // tm,),
        in_specs=[pl.BlockSpec((tm, d), lambda i: (i, 0)),
                  pl.BlockSpec((d, N_EXPERTS), lambda i: (0, 0)),
                  pl.BlockSpec((1, N_EXPERTS), lambda i: (0, 0))],
        out_specs=[tile, tile, tile, pl.BlockSpec((1, N_EXPERTS), lambda i: (0, 0))],
        out_shape=[jax.ShapeDtypeStruct((n, LANES), jnp.int32),
                   jax.ShapeDtypeStruct((n, LANES), F32),
                   jax.ShapeDtypeStruct((n, LANES), jnp.int32),
                   jax.ShapeDtypeStruct((1, N_EXPERTS), jnp.int32)],
        scratch_shapes=[pltpu.VMEM((1, N_EXPERTS), F32)],
        compiler_params=_params(("arbitrary",)),
        name="router",
    )(u, w_router.astype(F32), b_router.reshape(1, -1).astype(F32))


def _dispatch_kernel(dest_ref, tail_ref, nused_ref, u_ref, xs_ref, zbuf, zsem, sem, *, td, bm):
    step = pl.program_id(0)
    n_blocks = xs_ref.shape[0] // bm

    @pl.when(step == 0)
    def _():
        zbuf[...] = jnp.zeros_like(zbuf)

        def zero_block(blk):
            return pltpu.make_async_copy(zbuf, xs_ref.at[pl.ds(blk * bm, bm)], zsem)

        def tails(e, carry):
            @pl.when(tail_ref[e] >= 0)
            def _():
                zero_block(tail_ref[e]).start()
            return carry

        def tails_wait(e, carry):
            @pl.when(tail_ref[e] >= 0)
            def _():
                zero_block(tail_ref[e]).wait()
            return carry

        def unused(blk, carry):
            zero_block(blk).start()
            return carry

        def unused_wait(blk, carry):
            zero_block(blk).wait()
            return carry

        lax.fori_loop(0, N_EXPERTS, tails, 0)
        lax.fori_loop(nused_ref[0], n_blocks, unused, 0)
        lax.fori_loop(0, N_EXPERTS, tails_wait, 0)
        lax.fori_loop(nused_ref[0], n_blocks, unused_wait, 0)

    base = step * (td * TOP_K)

    for i in range(td):
        src = u_ref.at[pl.ds(i, 1)]
        for k in range(TOP_K):
            pltpu.make_async_copy(src, xs_ref.at[pl.ds(dest_ref[base + i * TOP_K + k], 1)], sem).start()

    tile = xs_ref.at[pl.ds(0, td * TOP_K)]
    pltpu.make_async_copy(tile, tile, sem).wait()


def dispatch(u, dest, tail_blk, n_used, cap, bm, td=64):
    n, d = u.shape
    td = min(td, n)
    grid_spec = pltpu.PrefetchScalarGridSpec(
        num_scalar_prefetch=3,
        grid=(n // td,),
        in_specs=[pl.BlockSpec((td, d), lambda i, *_: (i, 0))],
        out_specs=pl.BlockSpec(memory_space=pl.ANY),
        scratch_shapes=[pltpu.VMEM((bm, d), u.dtype), pltpu.SemaphoreType.DMA(()),
                        pltpu.SemaphoreType.DMA(())],
    )
    return pl.pallas_call(
        functools.partial(_dispatch_kernel, td=td, bm=bm),
        grid_spec=grid_spec,
        out_shape=jax.ShapeDtypeStruct((cap, d), u.dtype),
        compiler_params=_params(("arbitrary",)),
        name="moe_dispatch",
    )(dest.reshape(-1), tail_blk, n_used, u)


def _expert_kernel(be_ref, nused_ref, x_ref, wg_ref, wu_ref, wd_ref, y_ref, wg_bf, wu_bf, wd_bf):
    i = pl.program_id(0)

    @pl.when(jnp.logical_or(i == 0, be_ref[i] != be_ref[jnp.maximum(i - 1, 0)]))
    def _():
        wg_bf[...] = wg_ref[...].astype(BF16)
        wu_bf[...] = wu_ref[...].astype(BF16)
        wd_bf[...] = wd_ref[...].astype(BF16)

    @pl.when(i < nused_ref[0])
    def _():
        x = x_ref[...].astype(BF16)
        hg = jnp.dot(x, wg_bf[...], preferred_element_type=F32)
        hu = jnp.dot(x, wu_bf[...], preferred_element_type=F32)
        hb = (hg * _sigmoid(hg) * hu).astype(BF16)
        y_ref[...] = jnp.dot(hb, wd_bf[...], preferred_element_type=F32)

    @pl.when(i >= nused_ref[0])
    def _():
        y_ref[...] = jnp.zeros_like(y_ref)


def expert_ffn(xs, block_e, n_used, wg, wu, wd, bm):
    cap, d = xs.shape
    f = wg.shape[2]
    row = lambda i, be, nu: (jnp.maximum(jnp.minimum(i, nu[0] - 1), 0), 0)
    grid_spec = pltpu.PrefetchScalarGridSpec(
        num_scalar_prefetch=2,
        grid=(cap // bm,),
        in_specs=[pl.BlockSpec((bm, d), row),
                  pl.BlockSpec((None, d, f), lambda i, be, nu: (be[i], 0, 0)),
                  pl.BlockSpec((None, d, f), lambda i, be, nu: (be[i], 0, 0)),
                  pl.BlockSpec((None, f, d), lambda i, be, nu: (be[i], 0, 0))],
        out_specs=pl.BlockSpec((bm, d), lambda i, be, nu: (i, 0)),
        scratch_shapes=[pltpu.VMEM((d, f), BF16), pltpu.VMEM((d, f), BF16), pltpu.VMEM((f, d), BF16)],
    )
    return pl.pallas_call(
        _expert_kernel,
        grid_spec=grid_spec,
        out_shape=jax.ShapeDtypeStruct((cap, d), F32),
        compiler_params=_params(("arbitrary",)),
        name="moe_experts",
    )(block_e, n_used, xs, wg, wu, wd)


def _shared_kernel(u_ref, h_ref, wg_ref, wu_ref, wd_ref, o_ref):
    x = u_ref[...].astype(BF16)
    hg = jnp.dot(x, wg_ref[...], preferred_element_type=F32)
    hu = jnp.dot(x, wu_ref[...], preferred_element_type=F32)
    hb = (hg * _sigmoid(hg) * hu).astype(BF16)
    o_ref[...] = h_ref[...] + jnp.dot(hb, wd_ref[...], preferred_element_type=F32)


def shared_ffn(u, h, wg, wu, wd, tm=512):
    n, d = u.shape
    f = wg.shape[1]
    tm = min(tm, n)
    tile = pl.BlockSpec((tm, d), lambda i: (i, 0))
    return pl.pallas_call(
        _shared_kernel,
        grid=(n // tm,),
        in_specs=[tile, tile,
                  pl.BlockSpec((d, f), lambda i: (0, 0)),
                  pl.BlockSpec((d, f), lambda i: (0, 0)),
                  pl.BlockSpec((f, d), lambda i: (0, 0))],
        out_specs=tile,
        out_shape=jax.ShapeDtypeStruct((n, d), F32),
        compiler_params=_params(("parallel",)),
        name="shared_ffn",
    )(u, h, wg, wu, wd)


def _combine_kernel(dest_ref, gs_ref, hs_ref, g_ref, y_ref, h_out, u_out, buf_a, buf_b, sems, *, tt):
    step = pl.program_id(0)
    last = pl.num_programs(0) - 1

    def row_copy(tile, r, k, buf, sem):
        return pltpu.make_async_copy(y_ref.at[pl.ds(dest_ref[(tile * tt + r) * TOP_K + k], 1)],
                                     buf.at[k, pl.ds(r, 1)], sem)

    def wait_rows(buf, sem):
        pltpu.make_async_copy(buf, buf, sem).wait()

    @pl.when(step == 0)
    def _():
        def row(r, carry):
            for k in range(TOP_K):
                row_copy(0, r, k, buf_a, sems.at[0]).start()
            return carry

        lax.fori_loop(0, tt, row, 0)

    def phase(cur, cur_sem, nxt, nxt_sem):
        wait_rows(cur, cur_sem)
        nxt_tile = jnp.minimum(step + 1, last)
        for r in range(tt):
            for k in range(TOP_K):
                row_copy(nxt_tile, r, k, nxt, nxt_sem).start()
        gs = gs_ref[...]
        h = hs_ref[...]
        for k in range(TOP_K):
            h = h + gs[:, k:k + 1] * cur[k]
        h_out[...] = h
        ms = jnp.mean(h * h, axis=-1, keepdims=True)
        u_out[...] = (h * lax.rsqrt(ms + RMS_EPS) * g_ref[...]).astype(u_out.dtype)

        @pl.when(step == last)
        def _():
            wait_rows(nxt, nxt_sem)

    @pl.when(step % 2 == 0)
    def _():
        phase(buf_a, sems.at[0], buf_b, sems.at[1])

    @pl.when(step % 2 == 1)
    def _():
        phase(buf_b, sems.at[1], buf_a, sems.at[0])


def combine(y, dest, gsel, hs, g_next, tt=32):
    n, d = hs.shape
    tt = min(tt, n)
    tile = pl.BlockSpec((tt, d), lambda i, s: (i, 0))
    grid_spec = pltpu.PrefetchScalarGridSpec(
        num_scalar_prefetch=1,
        grid=(n // tt,),
        in_specs=[pl.BlockSpec((tt, LANES), lambda i, s: (i, 0)),
                  tile,
                  pl.BlockSpec((1, d), lambda i, s: (0, 0)),
                  pl.BlockSpec(memory_space=pl.ANY)],
        out_specs=[tile, tile],
        scratch_shapes=[pltpu.VMEM((TOP_K, tt, d), F32), pltpu.VMEM((TOP_K, tt, d), F32),
                        pltpu.SemaphoreType.DMA((2,))],
    )
    return pl.pallas_call(
        functools.partial(_combine_kernel, tt=tt),
        grid_spec=grid_spec,
        out_shape=[jax.ShapeDtypeStruct((n, d), F32), jax.ShapeDtypeStruct((n, d), BF16)],
        compiler_params=_params(("arbitrary",)),
        name="moe_combine",
    )(dest.reshape(-1), gsel, hs, g_next.reshape(1, -1).astype(F32), y)


def moe_ffn(h, g_norm, w_router, b_router, w_e_gate, w_e_up, w_e_down, w_s_gate, w_s_up, w_s_down,
            g_next, bm=EXPERT_BLOCK):
    n, d = h.shape
    u = rmsnorm(h, g_norm, F32)
    u_bf = u.astype(BF16)
    idx128, gs128, rank128, counts = router(u, w_router, b_router)
    idx = idx128[:, :TOP_K]
    rank = rank128[:, :TOP_K]
    counts = counts.reshape(-1)
    n_blk = (counts + bm - 1) // bm
    blk_end = jnp.cumsum(n_blk)
    pstart = (blk_end - n_blk) * bm
    experts = jnp.arange(N_EXPERTS, dtype=jnp.int32)
    dest = rank + jnp.sum(jnp.where(idx[:, :, None] == experts, pstart, 0), axis=-1)
    dest = dest.astype(jnp.int32)
    cap = (n * TOP_K // bm + N_EXPERTS) * bm
    blocks = jnp.arange(cap // bm, dtype=jnp.int32)
    block_e = jnp.minimum(jnp.sum(blk_end[None, :] <= blocks[:, None], axis=1), N_EXPERTS - 1)
    block_e = block_e.astype(jnp.int32)
    n_used = blk_end[-1:].astype(jnp.int32)
    tail_blk = jnp.where(n_blk > 0, blk_end - 1, -1).astype(jnp.int32)
    xs = dispatch(u, dest, tail_blk, n_used, cap, bm)
    y = expert_ffn(xs, block_e, n_used, w_e_gate.astype(F32), w_e_up.astype(F32), w_e_down.astype(F32), bm)
    hs = shared_ffn(u_bf, h, w_s_gate.astype(BF16), w_s_up.astype(BF16), w_s_down.astype(BF16))
    return combine(y, dest, gs128, hs, g_next)


def _ple_kernel(u_ref, p_ref, h_ref, wg_ref, wp_ref, gf_ref, o_ref):
    gate = _sigmoid(jnp.dot(u_ref[...], wg_ref[...], preferred_element_type=F32))
    proj = jnp.dot(p_ref[...].astype(BF16), wp_ref[...], preferred_element_type=F32)
    h = h_ref[...] + gate * proj
    ms = jnp.mean(h * h, axis=-1, keepdims=True)
    o_ref[...] = h * lax.rsqrt(ms + RMS_EPS) * gf_ref[...]


def ple_final(u, p, h, w_gate, w_proj, g_final, tm=256):
    n, d = h.shape
    pd = p.shape[1]
    tm = min(tm, n)
    tile = pl.BlockSpec((tm, d), lambda i: (i, 0))
    return pl.pallas_call(
        _ple_kernel,
        grid=(n // tm,),
        in_specs=[tile, pl.BlockSpec((tm, pd), lambda i: (i, 0)), tile,
                  pl.BlockSpec((d, d), lambda i: (0, 0)),
                  pl.BlockSpec((pd, d), lambda i: (0, 0)),
                  pl.BlockSpec((1, d), lambda i: (0, 0))],
        out_specs=tile,
        out_shape=jax.ShapeDtypeStruct((n, d), F32),
        compiler_params=_params(("parallel",)),
        name="ple_final",
    )(u, p, h, w_gate, w_proj, g_final.reshape(1, -1).astype(F32))


def token_mixing(x2, bsz, seq, norm_g, w_in, w_branch_gate, lq1, lk1, lq2, lk2, subln_g, shift_prev,
                 shift_next, w0, w_decay_up, a0, w_iclr_up, w_gate_up, k_k, k_a, r_k, ln_x_g, ln_x_b,
                 w_br_attn, w_br_rwkv, w_out, lambda_init):
    d = x2.shape[1]
    u = rmsnorm(x2, norm_g, BF16)
    w_attn = w_in[:, :A_COLS].astype(BF16)
    w_rwkv = jnp.pad(w_in[:, A_COLS:], ((0, 0), (0, R_COLS_PAD - R_COLS))).astype(BF16)
    za = matmul(u, w_attn, out_dtype=BF16, epilogue=functools.partial(_scale_q_epilogue, tn=512),
                tm=2048, tn=512, name="proj_attn")
    zr = matmul(u, w_rwkv, out_dtype=F32, tm=2048, tn=512, name="proj_rwkv")
    ya = diff_attention(za.reshape(bsz, seq, A_COLS), lq1, lk1, lq2, lk2, subln_g, lambda_init)
    r, kk, v, lw, b, kt, bonus, g = rwkv_prep(
        zr.reshape(bsz, seq, R_COLS_PAD), shift_prev, shift_next, w0, w_decay_up, a0, w_iclr_up,
        w_gate_up, k_k, k_a, r_k)
    y0, y1 = rwkv_scan(r, kk, v, lw, b, kt)
    yr = rwkv_post(y0, y1, bonus, g, ln_x_g, ln_x_b)
    merged = merge_branches(u, ya.reshape(-1, A_WIDTH), yr.reshape(-1, R_WIDTH),
                            w_branch_gate.astype(BF16), w_br_attn.astype(BF16), w_br_rwkv.astype(BF16))
    return matmul(merged, w_out.astype(BF16), out_dtype=F32, epilogue=_residual_add, extras=(x2,),
                  tm=2048, tn=512, name="out_proj")


def kernel(x, p, norm_mix_g, w_in, w_branch_gate, lambda_q1, lambda_k1, lambda_q2, lambda_k2, subln_g, shift_prev, shift_next, w0, w_decay_up, a0, w_iclr_up, w_gate_up, k_k, k_a, r_k, ln_x_g, ln_x_b, w_br_attn, w_br_rwkv, w_out, norm_ffn_g, w_router, b_router, w_e_gate, w_e_up, w_e_down, w_s_gate, w_s_up, w_s_down, norm_ple_g, w_ple_gate, w_ple_proj, norm_final_g):
    bsz, seq, d = x.shape
    depth = w_in.shape[0]
    assert depth == 1, "the final norm is fused into the last layer's embedding kernel"
    h = x.reshape(bsz * seq, d)
    i = 0
    lambda_init = 0.8 - 0.6 * math.exp(-0.3 * i)
    h = token_mixing(h, bsz, seq, norm_mix_g[i], w_in[i], w_branch_gate[i], lambda_q1[i], lambda_k1[i],
                     lambda_q2[i], lambda_k2[i], subln_g[i], shift_prev[i], shift_next[i], w0[i],
                     w_decay_up[i], a0[i], w_iclr_up[i], w_gate_up[i], k_k[i], k_a[i], r_k[i],
                     ln_x_g[i], ln_x_b[i], w_br_attn[i], w_br_rwkv[i], w_out[i], lambda_init)
    h, u = moe_ffn(h, norm_ffn_g[i], w_router[i], b_router[i], w_e_gate[i], w_e_up[i], w_e_down[i],
                   w_s_gate[i], w_s_up[i], w_s_down[i], norm_ple_g[i])
    out = ple_final(u, p[i].reshape(bsz * seq, -1), h, w_ple_gate[i].astype(BF16),
                    w_ple_proj[i].astype(BF16), norm_final_g)
    return out.reshape(bsz, seq, d)
```

```python
import functools
import math

import jax
import jax.numpy as jnp
from jax import lax
from jax.experimental import pallas as pl
from jax.experimental.pallas import tpu as pltpu

F32 = jnp.float32
BF16 = jnp.bfloat16
HI = lax.Precision.HIGHEST

D_MODEL = 2048
PLE_DIM = 256
A_HEADS = 8
A_QK_DIM = 64
A_V_DIM = 2 * A_QK_DIM
A_WIDTH = A_HEADS * A_V_DIM
R_HEADS = 16
R_HEAD = 64
R_WIDTH = R_HEADS * R_HEAD
DECAY_LORA = 64
ICLR_LORA = 64
GATE_LORA = 160
N_EXPERTS = 64
TOP_K = 8
EXPERT_FF = 512
SHARED_FF = 512
ROUTED_SCALE = 2.5
RMS_EPS = 1e-6
GN_EPS = 64e-5
LOG2E = math.log2(math.e)
Q_COLS = A_HEADS * 2 * A_QK_DIM
A_COLS = 2 * Q_COLS + A_WIDTH
R_COLS = 3 * R_WIDTH + 2 * DECAY_LORA + 2 * ICLR_LORA + GATE_LORA
R_COLS_PAD = 3584
LANES = 128
CHUNK = 64
EXPERT_BLOCK = 256
VMEM_LIMIT = 56 * 1024 * 1024


def _params(sem):
    return pltpu.CompilerParams(dimension_semantics=sem, vmem_limit_bytes=VMEM_LIMIT)


def _sigmoid(x):
    return 1.0 / (1.0 + jnp.exp(-x))


def _store_row_major(rows_ref, x):
    n = x.shape[0]
    pieces = x.shape[1] // LANES
    for c in range(pieces):
        rows_ref[pl.ds(c, n, stride=pieces), :] = x[:, c * LANES:(c + 1) * LANES]


def _load_row_major(rows_ref, n, pieces):
    return jnp.concatenate([rows_ref[pl.ds(c, n, stride=pieces), :] for c in range(pieces)], axis=1)


def _rmsnorm_kernel(x_ref, g_ref, o_ref):
    x = x_ref[...].astype(F32)
    ms = jnp.mean(x * x, axis=-1, keepdims=True)
    o_ref[...] = (x * lax.rsqrt(ms + RMS_EPS) * g_ref[...]).astype(o_ref.dtype)


def rmsnorm(x, g, out_dtype, tm=512):
    m, d = x.shape
    tm = min(tm, m)
    return pl.pallas_call(
        _rmsnorm_kernel,
        grid=(m // tm,),
        in_specs=[pl.BlockSpec((tm, d), lambda i: (i, 0)),
                  pl.BlockSpec((1, d), lambda i: (0, 0))],
        out_specs=pl.BlockSpec((tm, d), lambda i: (i, 0)),
        out_shape=jax.ShapeDtypeStruct((m, d), out_dtype),
        compiler_params=_params(("parallel",)),
        name="rmsnorm",
    )(x, g.reshape(1, d).astype(F32))


def _rmsnorm_rows_kernel(x_ref, g_ref, o_ref, rows_ref):
    x = x_ref[...].astype(F32)
    ms = jnp.mean(x * x, axis=-1, keepdims=True)
    u = x * lax.rsqrt(ms + RMS_EPS) * g_ref[...]
    o_ref[...] = u
    _store_row_major(rows_ref, u)


def rmsnorm_with_rows(x, g, tm=512):
    m, d = x.shape
    tm = min(tm, m)
    pieces = d // LANES
    return pl.pallas_call(
        _rmsnorm_rows_kernel,
        grid=(m // tm,),
        in_specs=[pl.BlockSpec((tm, d), lambda i: (i, 0)),
                  pl.BlockSpec((1, d), lambda i: (0, 0))],
        out_specs=[pl.BlockSpec((tm, d), lambda i: (i, 0)),
                   pl.BlockSpec((tm * pieces, LANES), lambda i: (i, 0))],
        out_shape=[jax.ShapeDtypeStruct((m, d), F32),
                   jax.ShapeDtypeStruct((m * pieces, LANES), F32)],
        compiler_params=_params(("parallel",)),
        name="rmsnorm_rows",
    )(x, g.reshape(1, d).astype(F32))


def _mm_kernel(x_ref, w_ref, *rest, epilogue):
    o_ref = rest[-1]
    acc = jnp.dot(x_ref[...], w_ref[...], preferred_element_type=F32)
    if epilogue is not None:
        acc = epilogue(acc, *[e[...] for e in rest[:-1]])
    o_ref[...] = acc.astype(o_ref.dtype)


def matmul(x, w, *, out_dtype, epilogue=None, extras=(), tm=512, tn=512, name="matmul"):
    m, k = x.shape
    n = w.shape[1]
    tm = min(tm, m)
    tn = min(tn, n)
    in_specs = [pl.BlockSpec((tm, k), lambda i, j: (i, 0)),
                pl.BlockSpec((k, tn), lambda i, j: (0, j))]
    in_specs += [pl.BlockSpec((tm, tn), lambda i, j: (i, j)) for _ in extras]
    return pl.pallas_call(
        functools.partial(_mm_kernel, epilogue=epilogue),
        grid=(m // tm, pl.cdiv(n, tn)),
        in_specs=in_specs,
        out_specs=pl.BlockSpec((tm, tn), lambda i, j: (i, j)),
        out_shape=jax.ShapeDtypeStruct((m, n), out_dtype),
        compiler_params=_params(("parallel", "parallel")),
        name=name,
    )(x, w, *extras)


def _residual_add(acc, res):
    return res + acc


def _scale_q_epilogue(acc, *, tn):
    is_q = pl.program_id(1) < Q_COLS // tn
    return acc * jnp.where(is_q, LOG2E * A_QK_DIM ** -0.5, 1.0)


def _attn_kernel(slopes_ref, q_ref, k_ref, v_ref, lq1_ref, lk1_ref, lq2_ref, lk2_ref, sg_ref,
                 o_ref, bias_ref, s_ref, *, tq, seq, kc, lambda_init):
    h = pl.program_id(0)
    qi = pl.program_id(1)

    @pl.when(pl.program_id(2) == 0)
    def _():
        qpos = qi * tq + lax.broadcasted_iota(jnp.int32, (tq, seq), 0)
        kpos = lax.broadcasted_iota(jnp.int32, (tq, seq), 1)
        bias_ref[...] = (-LOG2E * slopes_ref[h]) * jnp.abs(qpos - kpos).astype(F32)

    lane = lax.broadcasted_iota(jnp.int32, (tq, LANES), 1)
    q = q_ref[...]
    zero = jnp.zeros_like(q)
    qm = [jnp.where(lane < A_QK_DIM, q, zero), jnp.where(lane < A_QK_DIM, zero, q)]
    chunks = [slice(c * kc, (c + 1) * kc) for c in range(seq // kc)]
    dims = (((1,), (1,)), ((), ()))

    def scores(m, row_max):
        for sl in chunks:
            sc = lax.dot_general(qm[m], k_ref[sl, :], dims, preferred_element_type=F32) + bias_ref[:, sl]
            s_ref[m, :, sl] = sc
            cmax = jnp.max(sc, axis=-1, keepdims=True)
            row_max = cmax if row_max is None else jnp.maximum(row_max, cmax)
        return row_max

    def weighted_values(m, row_max):
        l = jnp.zeros((tq, 1), F32)
        acc = jnp.zeros((tq, LANES), F32)
        for sl in chunks:
            e = jnp.exp2(s_ref[m, :, sl] - row_max)
            l = l + jnp.sum(e, axis=-1, keepdims=True)
            acc = acc + jnp.dot(e.astype(BF16), v_ref[sl, :], preferred_element_type=F32)
        return acc / l

    max0 = scores(0, None)
    max1 = scores(1, None)
    o0 = weighted_values(0, max0)
    o1 = weighted_values(1, max1)
    lam = (jnp.exp(jnp.sum(lq1_ref[...] * lk1_ref[...], axis=-1, keepdims=True))
           - jnp.exp(jnp.sum(lq2_ref[...] * lk2_ref[...], axis=-1, keepdims=True))
           + lambda_init)
    o = o0 - lam * o1
    ms = jnp.mean(o * o, axis=-1, keepdims=True)
    o = o * lax.rsqrt(ms + RMS_EPS) * sg_ref[...]
    o_ref[...] = (o * (1.0 - lambda_init)).astype(o_ref.dtype)


def diff_attention(za, lq1, lk1, lq2, lk2, subln_g, lambda_init, tq=256):
    b, t, _ = za.shape
    tq = min(tq, t)
    h = jnp.arange(1, A_HEADS + 1, dtype=F32)
    slopes = jnp.exp2(-8.0 * h / A_HEADS)
    vec = lambda a: a.reshape(1, -1).astype(F32)
    small = lambda n: pl.BlockSpec((1, n), lambda hi, qi, bi, s: (0, 0))
    grid_spec = pltpu.PrefetchScalarGridSpec(
        num_scalar_prefetch=1,
        grid=(A_HEADS, t // tq, b),
        in_specs=[
            pl.BlockSpec((None, tq, LANES), lambda hi, qi, bi, s: (bi, qi, hi)),
            pl.BlockSpec((None, t, LANES), lambda hi, qi, bi, s: (bi, 0, A_HEADS + hi)),
            pl.BlockSpec((None, t, LANES), lambda hi, qi, bi, s: (bi, 0, 2 * A_HEADS + hi)),
            small(A_QK_DIM), small(A_QK_DIM), small(A_QK_DIM), small(A_QK_DIM), small(A_V_DIM),
        ],
        out_specs=pl.BlockSpec((None, tq, LANES), lambda hi, qi, bi, s: (bi, qi, hi)),
        scratch_shapes=[pltpu.VMEM((tq, t), F32), pltpu.VMEM((2, tq, t), F32)],
    )
    return pl.pallas_call(
        functools.partial(_attn_kernel, tq=tq, seq=t, kc=min(512, t), lambda_init=lambda_init),
        grid_spec=grid_spec,
        out_shape=jax.ShapeDtypeStruct((b, t, A_WIDTH), BF16),
        compiler_params=_params(("parallel", "parallel", "arbitrary")),
        name="diff_attention",
    )(slopes, za, za, za, vec(lq1), vec(lk1), vec(lq2), vec(lk2), vec(subln_g))


def _head_sum(x):
    ri = lax.broadcasted_iota(jnp.int32, (LANES, LANES), 0) // R_HEAD
    ci = lax.broadcasted_iota(jnp.int32, (LANES, LANES), 1) // R_HEAD
    ones = (ri == ci).astype(F32)
    return jnp.dot(x, ones, precision=HI, preferred_element_type=F32)


def _head_sum_wide(x):
    return jnp.concatenate(
        [_head_sum(x[:, p * LANES:(p + 1) * LANES]) for p in range(x.shape[1] // LANES)], axis=1)


def _rwkv_prep_kernel(z_ref, zprev_ref, znext_ref, mup_ref, mun_ref, w0_ref, wdu_ref, a0_ref,
                      wiu_ref, wgu_ref, kk_ref, ka_ref, rk_ref,
                      r_out, kkn_out, v_out, lw_out, b_out, kt_out, bonus_out, g_out, *, tt):
    i = pl.program_id(1)
    n_i = pl.num_programs(1)
    c = R_WIDTH
    z = z_ref[...]
    row = lax.broadcasted_iota(jnp.int32, (tt, 1), 0)
    prev_row = jnp.where(i > 0, zprev_ref[7:8, :], 0.0)
    next_row = jnp.where(i < n_i - 1, znext_ref[0:1, :], 0.0)
    zp = jnp.where(row == 0, prev_row, pltpu.roll(z, 1, 0))
    zn = jnp.where(row == tt - 1, next_row, pltpu.roll(z, tt - 1, 0))
    zs = z + mup_ref[...] * (zp - z) + mun_ref[...] * (zn - z)

    r = zs[:, :c]
    k = zs[:, c:2 * c]
    v = zs[:, 2 * c:3 * c]
    o3 = 3 * c
    lane = lax.broadcasted_iota(jnp.int32, (tt, LANES), 1)
    first = lane < DECAY_LORA
    lw = jnp.tanh(zs[:, o3:o3 + LANES])
    la = zs[:, o3 + LANES:o3 + 2 * LANES]
    lg = _sigmoid(zs[:, o3 + 2 * LANES:o3 + 4 * LANES])
    g_out[...] = jnp.dot(lg.astype(BF16), wgu_ref[...], preferred_element_type=F32)

    kk = k * kk_ref[...]
    ss = _head_sum_wide(kk * kk)
    kk = kk * lax.rsqrt(jnp.maximum(ss, 1e-12))
    r_out[...] = r
    kkn_out[...] = kk
    v_out[...] = v

    kt_sum = jnp.zeros_like(k)
    for d in range(2):
        keep = first if d == 0 else jnp.logical_not(first)
        lw_d = jnp.where(keep, lw, 0.0).astype(BF16)
        la_d = jnp.where(keep, la, 0.0).astype(BF16)
        zw = jnp.dot(lw_d, wdu_ref[...], preferred_element_type=F32) + w0_ref[d:d + 1, :]
        lw_out[d] = -math.exp(-0.5) * _sigmoid(zw)
        a = _sigmoid(jnp.dot(la_d, wiu_ref[...], preferred_element_type=F32) + a0_ref[d:d + 1, :])
        kt = k * (1.0 + (a - 1.0) * ka_ref[...])
        kt_out[d] = kt
        b_out[d] = kk * a
        kt_sum = kt_sum + kt
    coef = _head_sum_wide(r * kt_sum * rk_ref[...])
    bonus_out[...] = coef * v


def rwkv_prep(zr, shift_prev, shift_next, w0, w_decay_up, a0, w_iclr_up, w_gate_up, k_k, k_a, r_k,
              tt=128):
    b, t, cp = zr.shape
    tt = min(tt, t)
    c = R_WIDTH
    pad = cp - R_COLS
    row = lambda a: a.reshape(1, -1).astype(F32)
    mup = jnp.pad(row(shift_prev), ((0, 0), (0, pad)))
    mun = jnp.pad(row(shift_next), ((0, 0), (0, pad)))
    wdu = w_decay_up.reshape(2 * DECAY_LORA, c).astype(BF16)
    wiu = w_iclr_up.reshape(2 * ICLR_LORA, c).astype(BF16)
    wgu = jnp.pad(w_gate_up, ((0, 2 * LANES - GATE_LORA), (0, 0))).astype(BF16)
    nb8 = t // 8
    const = lambda shape: pl.BlockSpec(shape, lambda bi, i: (0,) * len(shape))
    in_specs = [
        pl.BlockSpec((None, tt, cp), lambda bi, i: (bi, i, 0)),
        pl.BlockSpec((None, 8, cp), lambda bi, i: (bi, jnp.maximum(i * (tt // 8) - 1, 0), 0)),
        pl.BlockSpec((None, 8, cp), lambda bi, i: (bi, jnp.minimum((i + 1) * (tt // 8), nb8 - 1), 0)),
        const((1, cp)), const((1, cp)), const((2, c)), const((2 * DECAY_LORA, c)), const((2, c)),
        const((2 * ICLR_LORA, c)), const((2 * LANES, c)), const((1, c)), const((1, c)), const((1, c)),
    ]
    one = pl.BlockSpec((None, tt, c), lambda bi, i: (bi, i, 0))
    two = pl.BlockSpec((2, None, tt, c), lambda bi, i: (0, bi, i, 0))
    s1 = jax.ShapeDtypeStruct((b, t, c), F32)
    s2 = jax.ShapeDtypeStruct((2, b, t, c), F32)
    return pl.pallas_call(
        functools.partial(_rwkv_prep_kernel, tt=tt),
        grid=(b, t // tt),
        in_specs=in_specs,
        out_specs=[one, one, one, two, two, two, one, one],
        out_shape=[s1, s1, s1, s2, s2, s2, s1, s1],
        compiler_params=_params(("parallel", "parallel")),
        name="rwkv_prep",
    )(zr, zr, zr, mup, mun, w0.astype(F32), wdu, a0.astype(F32), wiu, wgu, row(k_k), row(k_a), row(r_k))


def _scan_kernel(r0_ref, kk0_ref, v0_ref, r1_ref, kk1_ref, v1_ref, lw0_ref, b0_ref, kt0_ref,
                 lw1_ref, b1_ref, kt1_ref, y0_ref, y1_ref, s_ref, *, chunk):
    ln = chunk

    @pl.when(pl.program_id(1) == 0)
    def _():
        s_ref[...] = jnp.zeros_like(s_ref)

    ri = lax.broadcasted_iota(jnp.int32, (ln, ln), 0)
    ci = lax.broadcasted_iota(jnp.int32, (ln, ln), 1)
    r2 = lax.broadcasted_iota(jnp.int32, (2 * ln, 2 * ln), 0)
    c2 = lax.broadcasted_iota(jnp.int32, (2 * ln, 2 * ln), 1)
    eye = (r2 == c2).astype(F32)
    head0 = lax.broadcasted_iota(jnp.int32, (ln, LANES), 1) < R_HEAD
    lane_c = (((1,), (1,)), ((), ()))
    row_c = (((0,), (0,)), ((), ()))
    n_pairs = r0_ref.shape[1] // LANES

    def dot(a, b):
        return jnp.dot(a.astype(BF16), b.astype(BF16), preferred_element_type=F32)

    def dotg(a, b, dims):
        return lax.dot_general(a.astype(BF16), b.astype(BF16), dims, preferred_element_type=F32)

    def stack(x, sl):
        xs = x[:, sl]
        return jnp.concatenate([jnp.where(head0, xs, 0.0), jnp.where(head0, 0.0, xs)], axis=0)

    ar, bs, ks, vs, gt, strict, incl, out = [], [], [], [], [], [], [], []
    for d, (r_ref, kk_ref, v_ref, lw_ref, b_ref, kt_ref, y_ref) in enumerate((
            (r0_ref, kk0_ref, v0_ref, lw0_ref, b0_ref, kt0_ref, y0_ref),
            (r1_ref, kk1_ref, v1_ref, lw1_ref, b1_ref, kt1_ref, y1_ref))):
        earlier = (ri >= ci) if d == 0 else (ri <= ci)
        lw = lw_ref[...]
        cum = jnp.dot(earlier.astype(F32), lw, precision=HI, preferred_element_type=F32)
        g_inv = jnp.exp(-cum)
        rh = r_ref[...] * jnp.exp(cum)
        ah = kk_ref[...] * jnp.exp(cum - lw)
        bh = b_ref[...] * g_inv
        kh = kt_ref[...] * g_inv
        vv = v_ref[...]
        g_tot = jnp.exp(jnp.sum(lw, axis=0, keepdims=True))
        for p in range(n_pairs):
            sl = slice(p * LANES, (p + 1) * LANES)
            ar.append(jnp.concatenate([stack(ah, sl), stack(rh, sl)], axis=0).astype(BF16))
            bs.append(stack(bh, sl).astype(BF16))
            ks.append(stack(kh, sl).astype(BF16))
            vs.append(stack(vv, sl).astype(BF16))
            gt.append(g_tot[:, sl])
            strict.append((r2 > c2) if d == 0 else (r2 < c2))
            incl.append((r2 >= c2) if d == 0 else (r2 <= c2))
            out.append((y_ref, sl))
    jobs = range(len(ar))
    g1 = [dotg(ar[j], jnp.concatenate([bs[j], ks[j]], axis=0), lane_c) for j in jobs]
    n = [jnp.where(strict[j], g1[j][:2 * ln, :2 * ln], 0.0) for j in jobs]
    m_akrk = [jnp.concatenate([jnp.where(strict[j], g1[j][:2 * ln, 2 * ln:], 0.0),
                               jnp.where(incl[j], g1[j][2 * ln:, 2 * ln:], 0.0)], axis=0).astype(BF16)
              for j in jobs]
    m_rb = [jnp.where(incl[j], g1[j][2 * ln:, :2 * ln], 0.0).astype(BF16) for j in jobs]
    x = [eye - n[j] for j in jobs]
    pw = n
    for _ in range(int(math.log2(ln)) - 1):
        pw = [dot(pw[j], pw[j]) for j in jobs]
        x = [x[j] + dot(x[j], pw[j]) for j in jobs]
    s = [s_ref[j] for j in jobs]
    asrs = [dotg(ar[j], s[j], lane_c) for j in jobs]
    mv = [dot(m_akrk[j], vs[j]) for j in jobs]
    u = [dot(x[j], asrs[j][:2 * ln] + mv[j][:2 * ln]) for j in jobs]
    ys = [asrs[j][2 * ln:] + mv[j][2 * ln:] - dot(m_rb[j], u[j]) for j in jobs]
    for j in jobs:
        y_ref, sl = out[j]
        y_ref[:, sl] = ys[j][:ln] + ys[j][ln:]
    ds = [dotg(jnp.concatenate([vs[j], (-u[j]).astype(BF16)], axis=0),
               jnp.concatenate([ks[j], bs[j]], axis=0), row_c) for j in jobs]
    for j in jobs:
        s_ref[j] = (s[j] + ds[j]) * gt[j]


def rwkv_scan(r, kk, v, lw, b, kt, chunk=CHUNK):
    bsz, t, c = r.shape
    chunk = min(chunk, t)
    nc = t // chunk
    fwd = pl.BlockSpec((None, chunk, c), lambda bi, ci: (bi, ci, 0))
    bwd = pl.BlockSpec((None, chunk, c), lambda bi, ci: (bi, nc - 1 - ci, 0))
    fwd_d = pl.BlockSpec((None, None, chunk, c), lambda bi, ci: (0, bi, ci, 0))
    bwd_d = pl.BlockSpec((None, None, chunk, c), lambda bi, ci: (1, bi, nc - 1 - ci, 0))
    shape = jax.ShapeDtypeStruct((bsz, t, c), F32)
    return pl.pallas_call(
        functools.partial(_scan_kernel, chunk=chunk),
        grid=(bsz, nc),
        in_specs=[fwd, fwd, fwd, bwd, bwd, bwd, fwd_d, fwd_d, fwd_d, bwd_d, bwd_d, bwd_d],
        out_specs=[fwd, bwd],
        out_shape=[shape, shape],
        scratch_shapes=[pltpu.VMEM((2 * c // LANES, LANES, LANES), F32)],
        compiler_params=_params(("parallel", "arbitrary")),
        name="rwkv_scan",
    )(r, kk, v, r, kk, v, lw, b, kt, lw, b, kt)


def _rwkv_post_kernel(y0_ref, y1_ref, bonus_ref, g_ref, lng_ref, lnb_ref, o_ref):
    y = y0_ref[...] + y1_ref[...]
    mu = _head_sum_wide(y) * (1.0 / R_HEAD)
    yc = y - mu
    var = _head_sum_wide(yc * yc) * (1.0 / R_HEAD)
    yn = yc * lax.rsqrt(var + GN_EPS) * lng_ref[...] + lnb_ref[...]
    o_ref[...] = ((yn + bonus_ref[...]) * g_ref[...]).astype(o_ref.dtype)


def rwkv_post(y0, y1, bonus, g, ln_g, ln_b, tt=256):
    b, t, c = y0.shape
    tt = min(tt, t)
    row = lambda a: a.reshape(1, -1).astype(F32)
    one = pl.BlockSpec((None, tt, c), lambda bi, i: (bi, i, 0))
    const = pl.BlockSpec((1, c), lambda bi, i: (0, 0))
    return pl.pallas_call(
        _rwkv_post_kernel,
        grid=(b, t // tt),
        in_specs=[one, one, one, one, const, const],
        out_specs=one,
        out_shape=jax.ShapeDtypeStruct((b, t, c), BF16),
        compiler_params=_params(("parallel", "parallel")),
        name="rwkv_post",
    )(y0, y1, bonus, g, row(ln_g), row(ln_b))


def _merge_kernel(u_ref, ya_ref, yr_ref, wga_ref, wgr_ref, wa_ref, wr_ref, o_ref):
    u = u_ref[...]
    ga = _sigmoid(jnp.dot(u, wga_ref[...], preferred_element_type=F32))
    gr = _sigmoid(jnp.dot(u, wgr_ref[...], preferred_element_type=F32))
    a = jnp.dot(ya_ref[...], wa_ref[...], preferred_element_type=F32)
    r = jnp.dot(yr_ref[...], wr_ref[...], preferred_element_type=F32)
    o_ref[...] = (ga * a + gr * r).astype(o_ref.dtype)


def merge_branches(u, ya, yr, w_gate, wa, wr, tm=1024, tn=512):
    m, d = u.shape
    ka = ya.shape[1]
    kr = yr.shape[1]
    n = wa.shape[1]
    tm = min(tm, m)
    nj = n // tn
    return pl.pallas_call(
        _merge_kernel,
        grid=(m // tm, nj),
        in_specs=[pl.BlockSpec((tm, d), lambda i, j: (i, 0)),
                  pl.BlockSpec((tm, ka), lambda i, j: (i, 0)),
                  pl.BlockSpec((tm, kr), lambda i, j: (i, 0)),
                  pl.BlockSpec((d, tn), lambda i, j: (0, j)),
                  pl.BlockSpec((d, tn), lambda i, j: (0, j + nj)),
                  pl.BlockSpec((ka, tn), lambda i, j: (0, j)),
                  pl.BlockSpec((kr, tn), lambda i, j: (0, j))],
        out_specs=pl.BlockSpec((tm, tn), lambda i, j: (i, j)),
        out_shape=jax.ShapeDtypeStruct((m, n), BF16),
        compiler_params=_params(("parallel", "parallel")),
        name="merge_branches",
    )(u, ya, yr, w_gate, w_gate, wa, wr)


def _router_kernel(u_ref, w_ref, b_ref, idx_ref, gs_ref, rank_ref, cnt_ref, carry_ref, *, tm):
    i = pl.program_id(0)

    @pl.when(i == 0)
    def _():
        carry_ref[...] = jnp.zeros_like(carry_ref)

    logits = jnp.dot(u_ref[...], w_ref[...], precision=HI, preferred_element_type=F32)
    scores = _sigmoid(logits)
    cur = scores + b_ref[...]
    lane = lax.broadcasted_iota(jnp.int32, (tm, N_EXPERTS), 1).astype(F32)
    lane_out = lax.broadcasted_iota(jnp.int32, (tm, LANES), 1)
    picks = []
    sel_f = jnp.zeros((tm, N_EXPERTS), F32)
    idx_out = jnp.zeros((tm, LANES), F32)
    for k in range(TOP_K):
        best = jnp.max(cur, axis=-1, keepdims=True)
        ik = jnp.min(jnp.where(cur == best, lane, float(N_EXPERTS)), axis=-1, keepdims=True)
        onehot = lane == ik
        picks.append(onehot)
        sel_f = jnp.where(onehot, 1.0, sel_f)
        cur = jnp.where(onehot, -jnp.inf, cur)
        idx_out = jnp.where(lane_out == k, ik, idx_out)
    gsel = scores * sel_f
    gsel = gsel / jnp.sum(gsel, axis=-1, keepdims=True) * ROUTED_SCALE
    ri = lax.broadcasted_iota(jnp.int32, (tm, tm), 0)
    ci = lax.broadcasted_iota(jnp.int32, (tm, tm), 1)
    before = jnp.where(ri > ci, 1.0, 0.0).astype(BF16)
    rank =jnp.dot(before, sel_f.astype(BF16), preferred_element_type=F32) + carry_ref[...]
    carry_ref[...] = carry_ref[...] + jnp.sum(sel_f, axis=0, keepdims=True)
    cnt_ref[...] = carry_ref[...].astype(jnp.int32)
    gs_out = jnp.zeros((tm, LANES), F32)
    rank_out = jnp.zeros((tm, LANES), F32)
    for k in range(TOP_K):
        gk = jnp.sum(jnp.where(picks[k], gsel, 0.0), axis=-1, keepdims=True)
        rk = jnp.sum(jnp.where(picks[k], rank, 0.0), axis=-1, keepdims=True)
        gs_out = jnp.where(lane_out == k, gk, gs_out)
        rank_out = jnp.where(lane_out == k, rk, rank_out)
    idx_ref[...] = idx_out.astype(jnp.int32)
    gs_ref[...] = gs_out
    rank_ref[...] = rank_out.astype(jnp.int32)


def router(u, w_router, b_router, tm=256):
    n, d = u.shape
    tm = min(tm, n)
    tile = pl.BlockSpec((tm, LANES), lambda i: (i, 0))
    return pl.pallas_call(
        functools.partial(_router_kernel, tm=tm),
        grid=(n // tm,),
        in_specs=[pl.BlockSpec((tm, d), lambda i: (i, 0)),
                  pl.BlockSpec((d, N_EXPERTS), lambda i: (0, 0)),
                  pl.BlockSpec((1, N_EXPERTS), lambda i: (0, 0))],
        out_specs=[tile, tile, tile, pl.BlockSpec((1, N_EXPERTS), lambda i: (0, 0))],
        out_shape=[jax.ShapeDtypeStruct((n, LANES), jnp.int32),
                   jax.ShapeDtypeStruct((n, LANES), F32),
                   jax.ShapeDtypeStruct((n, LANES), jnp.int32),
                   jax.ShapeDtypeStruct((1, N_EXPERTS), jnp.int32)],
        scratch_shapes=[pltpu.VMEM((1, N_EXPERTS), F32)],
        compiler_params=_params(("arbitrary",)),
        name="router",
    )(u, w_router.astype(F32), b_router.reshape(1, -1).astype(F32))


def _dispatch_kernel(dest_ref, tail_ref, nused_ref, u_ref, xs_ref, zbuf, zsem, sem, *, td, bm, pieces):
    step = pl.program_id(0)
    blk_rows = bm * pieces
    n_blocks = xs_ref.shape[0] // blk_rows

    @pl.when(step == 0)
    def _():
        zbuf[...] = jnp.zeros_like(zbuf)

        def zero_block(blk):
            return pltpu.make_async_copy(zbuf, xs_ref.at[pl.ds(blk * blk_rows, blk_rows)], zsem)

        def tails(e, carry):
            @pl.when(tail_ref[e] >= 0)
            def _():
                zero_block(tail_ref[e]).start()
            return carry

        def tails_wait(e, carry):
            @pl.when(tail_ref[e] >= 0)
            def _():
                zero_block(tail_ref[e]).wait()
            return carry

        def unused(blk, carry):
            zero_block(blk).start()
            return carry

        def unused_wait(blk, carry):
            zero_block(blk).wait()
            return carry

        lax.fori_loop(0, N_EXPERTS, tails, 0)
        lax.fori_loop(nused_ref[0], n_blocks, unused, 0)
        lax.fori_loop(0, N_EXPERTS, tails_wait, 0)
        lax.fori_loop(nused_ref[0], n_blocks, unused_wait, 0)

    base = step * (td * TOP_K)

    for i in range(td):
        src = u_ref.at[pl.ds(i * pieces, pieces)]
        for k in range(TOP_K):
            slot = dest_ref[base + i * TOP_K + k]
            pltpu.make_async_copy(src, xs_ref.at[pl.ds(slot * pieces, pieces)], sem).start()

    tile = xs_ref.at[pl.ds(0, td * TOP_K * pieces)]
    pltpu.make_async_copy(tile, tile, sem).wait()


def dispatch(u_rows, dest, tail_blk, n_used, cap, bm, pieces, td=64):
    n = u_rows.shape[0] // pieces
    td = min(td, n)
    grid_spec = pltpu.PrefetchScalarGridSpec(
        num_scalar_prefetch=3,
        grid=(n // td,),
        in_specs=[pl.BlockSpec((td * pieces, LANES), lambda i, *_: (i, 0))],
        out_specs=pl.BlockSpec(memory_space=pl.ANY),
        scratch_shapes=[pltpu.VMEM((bm * pieces, LANES), u_rows.dtype), pltpu.SemaphoreType.DMA(()),
                        pltpu.SemaphoreType.DMA(())],
    )
    return pl.pallas_call(
        functools.partial(_dispatch_kernel, td=td, bm=bm, pieces=pieces),
        grid_spec=grid_spec,
        out_shape=jax.ShapeDtypeStruct((cap * pieces, LANES), u_rows.dtype),
        compiler_params=_params(("arbitrary",)),
        name="moe_dispatch",
    )(dest.reshape(-1), tail_blk, n_used, u_rows)


def _expert_kernel(be_ref, nused_ref, x_ref, wg_ref, wu_ref, wd_ref, y_ref, wg_bf, wu_bf, wd_bf, *, bm):
    i = pl.program_id(0)
    pieces = wg_bf.shape[0] // LANES

    @pl.when(jnp.logical_or(i == 0, be_ref[i] != be_ref[jnp.maximum(i - 1, 0)]))
    def _():
        wg_bf[...] = wg_ref[...].astype(BF16)
        wu_bf[...] = wu_ref[...].astype(BF16)
        wd_bf[...] = wd_ref[...].astype(BF16)

    @pl.when(i < nused_ref[0])
    def _():
        x = _load_row_major(x_ref, bm, pieces).astype(BF16)
        hg = jnp.dot(x, wg_bf[...], preferred_element_type=F32)
        hu = jnp.dot(x, wu_bf[...], preferred_element_type=F32)
        hb = (hg * _sigmoid(hg) * hu).astype(BF16)
        _store_row_major(y_ref, jnp.dot(hb, wd_bf[...], preferred_element_type=F32))

    @pl.when(i >= nused_ref[0])
    def _():
        y_ref[...] = jnp.zeros_like(y_ref)


def expert_ffn(xs, block_e, n_used, wg, wu, wd, bm):
    d = wg.shape[1]
    pieces = d // LANES
    cap = xs.shape[0] // pieces
    f = wg.shape[2]
    row = lambda i, be, nu: (jnp.maximum(jnp.minimum(i, nu[0] - 1), 0), 0)
    grid_spec = pltpu.PrefetchScalarGridSpec(
        num_scalar_prefetch=2,
        grid=(cap // bm,),
        in_specs=[pl.BlockSpec((bm * pieces, LANES), row),
                  pl.BlockSpec((None, d, f), lambda i, be, nu: (be[i], 0, 0)),
                  pl.BlockSpec((None, d, f), lambda i, be, nu: (be[i], 0, 0)),
                  pl.BlockSpec((None, f, d), lambda i, be, nu: (be[i], 0, 0))],
        out_specs=pl.BlockSpec((bm * pieces, LANES), lambda i, be, nu: (i, 0)),
        scratch_shapes=[pltpu.VMEM((d, f), BF16), pltpu.VMEM((d, f), BF16), pltpu.VMEM((f, d), BF16)],
    )
    return pl.pallas_call(
        functools.partial(_expert_kernel, bm=bm),
        grid_spec=grid_spec,
        out_shape=jax.ShapeDtypeStruct((cap * pieces, LANES), F32),
        compiler_params=_params(("arbitrary",)),
        name="moe_experts",
    )(block_e, n_used, xs, wg, wu, wd)


def _shared_kernel(u_ref, h_ref, wg_ref, wu_ref, wd_ref, o_ref):
    x = u_ref[...].astype(BF16)
    hg = jnp.dot(x, wg_ref[...], preferred_element_type=F32)
    hu = jnp.dot(x, wu_ref[...], preferred_element_type=F32)
    hb = (hg * _sigmoid(hg) * hu).astype(BF16)
    o_ref[...] = h_ref[...] + jnp.dot(hb, wd_ref[...], preferred_element_type=F32)


def shared_ffn(u, h, wg, wu, wd, tm=512):
    n, d = u.shape
    f = wg.shape[1]
    tm = min(tm, n)
    tile = pl.BlockSpec((tm, d), lambda i: (i, 0))
    return pl.pallas_call(
        _shared_kernel,
        grid=(n // tm,),
        in_specs=[tile, tile,
                  pl.BlockSpec((d, f), lambda i: (0, 0)),
                  pl.BlockSpec((d, f), lambda i: (0, 0)),
                  pl.BlockSpec((f, d), lambda i: (0, 0))],
        out_specs=tile,
        out_shape=jax.ShapeDtypeStruct((n, d), F32),
        compiler_params=_params(("parallel",)),
        name="shared_ffn",
    )(u, h, wg, wu, wd)


def _combine_kernel(dest_ref, gs_ref, hs_ref, g_ref, y_ref, h_out, u_out, buf_a, buf_b, sems, *, tt):
    step = pl.program_id(0)
    last = pl.num_programs(0) - 1

    pieces = hs_ref.shape[1] // LANES

    def row_copy(tile, r, k, buf, sem):
        slot = dest_ref[(tile * tt + r) * TOP_K + k]
        return pltpu.make_async_copy(y_ref.at[pl.ds(slot * pieces, pieces)],
                                     buf.at[k, pl.ds(r * pieces, pieces)], sem)

    def wait_rows(buf, sem):
        pltpu.make_async_copy(buf, buf, sem).wait()

    @pl.when(step == 0)
    def _():
        def row(r, carry):
            for k in range(TOP_K):
                row_copy(0, r, k, buf_a, sems.at[0]).start()
            return carry

        lax.fori_loop(0, tt, row, 0)

    def phase(cur, cur_sem, nxt, nxt_sem):
        wait_rows(cur, cur_sem)
        nxt_tile = jnp.minimum(step + 1, last)
        for r in range(tt):
            for k in range(TOP_K):
                row_copy(nxt_tile, r, k, nxt, nxt_sem).start()
        gs = gs_ref[...]
        h = hs_ref[...]
        for k in range(TOP_K):
            h = h + gs[:, k:k + 1] * _load_row_major(cur.at[k], tt, pieces)
        h_out[...] = h
        ms = jnp.mean(h * h, axis=-1, keepdims=True)
        u_out[...] = (h * lax.rsqrt(ms + RMS_EPS) * g_ref[...]).astype(u_out.dtype)

        @pl.when(step == last)
        def _():
            wait_rows(nxt, nxt_sem)

    @pl.when(step % 2 == 0)
    def _():
        phase(buf_a, sems.at[0], buf_b, sems.at[1])

    @pl.when(step % 2 == 1)
    def _():
        phase(buf_b, sems.at[1], buf_a, sems.at[0])


def combine(y, dest, gsel, hs, g_next, tt=32):
    n, d = hs.shape
    pieces = d // LANES
    tt = min(tt, n)
    tile = pl.BlockSpec((tt, d), lambda i, s: (i, 0))
    grid_spec = pltpu.PrefetchScalarGridSpec(
        num_scalar_prefetch=1,
        grid=(n // tt,),
        in_specs=[pl.BlockSpec((tt, LANES), lambda i, s: (i, 0)),
                  tile,
                  pl.BlockSpec((1, d), lambda i, s: (0, 0)),
                  pl.BlockSpec(memory_space=pl.ANY)],
        out_specs=[tile, tile],
        scratch_shapes=[pltpu.VMEM((TOP_K, tt * pieces, LANES), F32),
                        pltpu.VMEM((TOP_K, tt * pieces, LANES), F32), pltpu.SemaphoreType.DMA((2,))],
    )
    return pl.pallas_call(
        functools.partial(_combine_kernel, tt=tt),
        grid_spec=grid_spec,
        out_shape=[jax.ShapeDtypeStruct((n, d), F32), jax.ShapeDtypeStruct((n, d), BF16)],
        compiler_params=_params(("arbitrary",)),
        name="moe_combine",
    )(dest.reshape(-1), gsel, hs, g_next.reshape(1, -1).astype(F32), y)


def moe_ffn(h, g_norm, w_router, b_router, w_e_gate, w_e_up, w_e_down, w_s_gate, w_s_up, w_s_down,
            g_next, bm=EXPERT_BLOCK):
    n, d = h.shape
    u, u_rows = rmsnorm_with_rows(h, g_norm)
    u_bf = u.astype(BF16)
    idx128, gs128, rank128, counts = router(u, w_router, b_router)
    idx = idx128[:, :TOP_K]
    rank = rank128[:, :TOP_K]
    counts = counts.reshape(-1)
    n_blk = (counts + bm - 1) // bm
    blk_end = jnp.cumsum(n_blk)
    pstart = (blk_end - n_blk) * bm
    experts = jnp.arange(N_EXPERTS, dtype=jnp.int32)
    dest = rank + jnp.sum(jnp.where(idx[:, :, None] == experts, pstart, 0), axis=-1)
    dest = dest.astype(jnp.int32)
    cap = (n * TOP_K // bm + N_EXPERTS) * bm
    blocks = jnp.arange(cap // bm, dtype=jnp.int32)
    block_e = jnp.minimum(jnp.sum(blk_end[None, :] <= blocks[:, None], axis=1), N_EXPERTS - 1)
    block_e = block_e.astype(jnp.int32)
    n_used = blk_end[-1:].astype(jnp.int32)
    tail_blk = jnp.where(n_blk > 0, blk_end - 1, -1).astype(jnp.int32)
    xs = dispatch(u_rows, dest, tail_blk, n_used, cap, bm, d // LANES)
    y = expert_ffn(xs, block_e, n_used, w_e_gate.astype(F32), w_e_up.astype(F32), w_e_down.astype(F32), bm)
    hs = shared_ffn(u_bf, h, w_s_gate.astype(BF16), w_s_up.astype(BF16), w_s_down.astype(BF16))
    return combine(y, dest, gs128, hs, g_next)


def _ple_kernel(u_ref, p_ref, h_ref, wg_ref, wp_ref, gf_ref, o_ref):
    gate = _sigmoid(jnp.dot(u_ref[...], wg_ref[...], preferred_element_type=F32))
    proj = jnp.dot(p_ref[...].astype(BF16), wp_ref[...], preferred_element_type=F32)
    h = h_ref[...] + gate * proj
    ms = jnp.mean(h * h, axis=-1, keepdims=True)
    o_ref[...] = h * lax.rsqrt(ms + RMS_EPS) * gf_ref[...]


def ple_final(u, p, h, w_gate, w_proj, g_final, tm=256):
    n, d = h.shape
    pd = p.shape[1]
    tm = min(tm, n)
    tile = pl.BlockSpec((tm, d), lambda i: (i, 0))
    return pl.pallas_call(
        _ple_kernel,
        grid=(n // tm,),
        in_specs=[tile, pl.BlockSpec((tm, pd), lambda i: (i, 0)), tile,
                  pl.BlockSpec((d, d), lambda i: (0, 0)),
                  pl.BlockSpec((pd, d), lambda i: (0, 0)),
                  pl.BlockSpec((1, d), lambda i: (0, 0))],
        out_specs=tile,
        out_shape=jax.ShapeDtypeStruct((n, d), F32),
        compiler_params=_params(("parallel",)),
        name="ple_final",
    )(u, p, h, w_gate, w_proj, g_final.reshape(1, -1).astype(F32))


def token_mixing(x2, bsz, seq, norm_g, w_in, w_branch_gate, lq1, lk1, lq2, lk2, subln_g, shift_prev,
                 shift_next, w0, w_decay_up, a0, w_iclr_up, w_gate_up, k_k, k_a, r_k, ln_x_g, ln_x_b,
                 w_br_attn, w_br_rwkv, w_out, lambda_init):
    d = x2.shape[1]
    u = rmsnorm(x2, norm_g, BF16)
    w_attn = w_in[:, :A_COLS].astype(BF16)
    w_rwkv = jnp.pad(w_in[:, A_COLS:], ((0, 0), (0, R_COLS_PAD - R_COLS))).astype(BF16)
    za = matmul(u, w_attn, out_dtype=BF16, epilogue=functools.partial(_scale_q_epilogue, tn=512),
                tm=2048, tn=512, name="proj_attn")
    zr = matmul(u, w_rwkv, out_dtype=F32, tm=2048, tn=512, name="proj_rwkv")
    ya = diff_attention(za.reshape(bsz, seq, A_COLS), lq1, lk1, lq2, lk2, subln_g, lambda_init)
    r, kk, v, lw, b, kt, bonus, g = rwkv_prep(
        zr.reshape(bsz, seq, R_COLS_PAD), shift_prev, shift_next, w0, w_decay_up, a0, w_iclr_up,
        w_gate_up, k_k, k_a, r_k)
    y0, y1 = rwkv_scan(r, kk, v, lw, b, kt)
    yr = rwkv_post(y0, y1, bonus, g, ln_x_g, ln_x_b)
    merged = merge_branches(u, ya.reshape(-1, A_WIDTH), yr.reshape(-1, R_WIDTH),
                            w_branch_gate.astype(BF16), w_br_attn.astype(BF16), w_br_rwkv.astype(BF16))
    return matmul(merged, w_out.astype(BF16), out_dtype=F32, epilogue=_residual_add, extras=(x2,),
                  tm=2048, tn=512, name="out_proj")


def kernel(x, p, norm_mix_g, w_in, w_branch_gate, lambda_q1, lambda_k1, lambda_q2, lambda_k2, subln_g, shift_prev, shift_next, w0, w_decay_up, a0, w_iclr_up, w_gate_up, k_k, k_a, r_k, ln_x_g, ln_x_b, w_br_attn, w_br_rwkv, w_out, norm_ffn_g, w_router, b_router, w_e_gate, w_e_up, w_e_down, w_s_gate, w_s_up, w_s_down, norm_ple_g, w_ple_gate, w_ple_proj, norm_final_g):
    bsz, seq, d = x.shape
    depth = w_in.shape[0]
    assert depth == 1, "the final norm is fused into the last layer's embedding kernel"
    h = x.reshape(bsz * seq, d)
    i = 0
    lambda_init = 0.8 - 0.6 * math.exp(-0.3 * i)
    h = token_mixing(h, bsz, seq, norm_mix_g[i], w_in[i], w_branch_gate[i], lambda_q1[i], lambda_k1[i],
                     lambda_q2[i], lambda_k2[i], subln_g[i], shift_prev[i], shift_next[i], w0[i],
                     w_decay_up[i], a0[i], w_iclr_up[i], w_gate_up[i], k_k[i], k_a[i], r_k[i],
                     ln_x_g[i], ln_x_b[i], w_br_attn[i], w_br_rwkv[i], w_out[i], lambda_init)
    h, u = moe_ffn(h, norm_ffn_g[i], w_router[i], b_router[i], w_e_gate[i], w_e_up[i], w_e_down[i],
                   w_s_gate[i], w_s_up[i], w_s_down[i], norm_ple_g[i])
    out = ple_final(u, p[i].reshape(bsz * seq, -1), h, w_ple_gate[i].astype(BF16),
                    w_ple_proj[i].astype(BF16), norm_final_g)
    return out.reshape(bsz, seq, d)
```

```python
import functools
import math

import jax
import jax.numpy as jnp
from jax import lax
from jax.experimental import pallas as pl
from jax.experimental.pallas import tpu as pltpu

F32 = jnp.float32
BF16 = jnp.bfloat16
HI = lax.Precision.HIGHEST

D_MODEL = 2048
PLE_DIM = 256
A_HEADS = 8
A_QK_DIM = 64
A_V_DIM = 2 * A_QK_DIM
A_WIDTH = A_HEADS * A_V_DIM
R_HEADS = 16
R_HEAD = 64
R_WIDTH = R_HEADS * R_HEAD
DECAY_LORA = 64
ICLR_LORA = 64
GATE_LORA = 160
N_EXPERTS = 64
TOP_K = 8
EXPERT_FF = 512
SHARED_FF = 512
ROUTED_SCALE = 2.5
RMS_EPS = 1e-6
GN_EPS = 64e-5
LOG2E = math.log2(math.e)
Q_COLS = A_HEADS * 2 * A_QK_DIM
A_COLS = 2 * Q_COLS + A_WIDTH
R_COLS = 3 * R_WIDTH + 2 * DECAY_LORA + 2 * ICLR_LORA + GATE_LORA
R_COLS_PAD = 3584
LANES = 128
CHUNK = 64
EXPERT_BLOCK = 256
VMEM_LIMIT = 56 * 1024 * 1024


def _params(sem):
    return pltpu.CompilerParams(dimension_semantics=sem, vmem_limit_bytes=VMEM_LIMIT)


def _sigmoid(x):
    return 1.0 / (1.0 + jnp.exp(-x))


def _rmsnorm_kernel(x_ref, g_ref, o_ref):
    x = x_ref[...].astype(F32)
    ms = jnp.mean(x * x, axis=-1, keepdims=True)
    o_ref[...] = (x * lax.rsqrt(ms + RMS_EPS) * g_ref[...]).astype(o_ref.dtype)


def rmsnorm(x, g, out_dtype, tm=512):
    m, d = x.shape
    tm = min(tm, m)
    return pl.pallas_call(
        _rmsnorm_kernel,
        grid=(m // tm,),
        in_specs=[pl.BlockSpec((tm, d), lambda i: (i, 0)),
                  pl.BlockSpec((1, d), lambda i: (0, 0))],
        out_specs=pl.BlockSpec((tm, d), lambda i: (i, 0)),
        out_shape=jax.ShapeDtypeStruct((m, d), out_dtype),
        compiler_params=_params(("parallel",)),
        name="rmsnorm",
    )(x, g.reshape(1, d).astype(F32))


def _mm_kernel(x_ref, w_ref, *rest, epilogue):
    o_ref = rest[-1]
    acc = jnp.dot(x_ref[...], w_ref[...], preferred_element_type=F32)
    if epilogue is not None:
        acc = epilogue(acc, *[e[...] for e in rest[:-1]])
    o_ref[...] = acc.astype(o_ref.dtype)


def matmul(x, w, *, out_dtype, epilogue=None, extras=(), tm=512, tn=512, name="matmul"):
    m, k = x.shape
    n = w.shape[1]
    tm = min(tm, m)
    tn = min(tn, n)
    in_specs = [pl.BlockSpec((tm, k), lambda i, j: (i, 0)),
                pl.BlockSpec((k, tn), lambda i, j: (0, j))]
    in_specs += [pl.BlockSpec((tm, tn), lambda i, j: (i, j)) for _ in extras]
    return pl.pallas_call(
        functools.partial(_mm_kernel, epilogue=epilogue),
        grid=(m // tm, pl.cdiv(n, tn)),
        in_specs=in_specs,
        out_specs=pl.BlockSpec((tm, tn), lambda i, j: (i, j)),
        out_shape=jax.ShapeDtypeStruct((m, n), out_dtype),
        compiler_params=_params(("parallel", "parallel")),
        name=name,
    )(x, w, *extras)


def _residual_add(acc, res):
    return res + acc


def _scale_q_epilogue(acc, *, tn):
    is_q = pl.program_id(1) < Q_COLS // tn
    return acc * jnp.where(is_q, LOG2E * A_QK_DIM ** -0.5, 1.0)


def _attn_kernel(slopes_ref, q_ref, k_ref, v_ref, lq1_ref, lk1_ref, lq2_ref, lk2_ref, sg_ref,
                 o_ref, bias_ref, s_ref, *, tq, seq, kc, lambda_init):
    h = pl.program_id(0)
    qi = pl.program_id(1)

    @pl.when(pl.program_id(2) == 0)
    def _():
        qpos = qi * tq + lax.broadcasted_iota(jnp.int32, (tq, seq), 0)
        kpos = lax.broadcasted_iota(jnp.int32, (tq, seq), 1)
        bias_ref[...] = (-LOG2E * slopes_ref[h]) * jnp.abs(qpos - kpos).astype(F32)

    lane = lax.broadcasted_iota(jnp.int32, (tq, LANES), 1)
    q = q_ref[...]
    zero = jnp.zeros_like(q)
    qm = [jnp.where(lane < A_QK_DIM, q, zero), jnp.where(lane < A_QK_DIM, zero, q)]
    chunks = [slice(c * kc, (c + 1) * kc) for c in range(seq // kc)]
    dims = (((1,), (1,)), ((), ()))

    def scores(m, row_max):
        for sl in chunks:
            sc = lax.dot_general(qm[m], k_ref[sl, :], dims, preferred_element_type=F32) + bias_ref[:, sl]
            s_ref[m, :, sl] = sc
            cmax = jnp.max(sc, axis=-1, keepdims=True)
            row_max = cmax if row_max is None else jnp.maximum(row_max, cmax)
        return row_max

    def weighted_values(m, row_max):
        l = jnp.zeros((tq, 1), F32)
        acc = jnp.zeros((tq, LANES), F32)
        for sl in chunks:
            e = jnp.exp2(s_ref[m, :, sl] - row_max)
            l = l + jnp.sum(e, axis=-1, keepdims=True)
            acc = acc + jnp.dot(e.astype(BF16), v_ref[sl, :], preferred_element_type=F32)
        return acc / l

    max0 = scores(0, None)
    max1 = scores(1, None)
    o0 = weighted_values(0, max0)
    o1 = weighted_values(1, max1)
    lam = (jnp.exp(jnp.sum(lq1_ref[...] * lk1_ref[...], axis=-1, keepdims=True))
           - jnp.exp(jnp.sum(lq2_ref[...] * lk2_ref[...], axis=-1, keepdims=True))
           + lambda_init)
    o = o0 - lam * o1
    ms = jnp.mean(o * o, axis=-1, keepdims=True)
    o = o * lax.rsqrt(ms + RMS_EPS) * sg_ref[...]
    o_ref[...] = (o * (1.0 - lambda_init)).astype(o_ref.dtype)


def diff_attention(za, lq1, lk1, lq2, lk2, subln_g, lambda_init, tq=256):
    b, t, _ = za.shape
    tq = min(tq, t)
    h = jnp.arange(1, A_HEADS + 1, dtype=F32)
    slopes = jnp.exp2(-8.0 * h / A_HEADS)
    vec = lambda a: a.reshape(1, -1).astype(F32)
    small = lambda n: pl.BlockSpec((1, n), lambda hi, qi, bi, s: (0, 0))
    grid_spec = pltpu.PrefetchScalarGridSpec(
        num_scalar_prefetch=1,
        grid=(A_HEADS, t // tq, b),
        in_specs=[
            pl.BlockSpec((None, tq, LANES), lambda hi, qi, bi, s: (bi, qi, hi)),
            pl.BlockSpec((None, t, LANES), lambda hi, qi, bi, s: (bi, 0, A_HEADS + hi)),
            pl.BlockSpec((None, t, LANES), lambda hi, qi, bi, s: (bi, 0, 2 * A_HEADS + hi)),
            small(A_QK_DIM), small(A_QK_DIM), small(A_QK_DIM), small(A_QK_DIM), small(A_V_DIM),
        ],
        out_specs=pl.BlockSpec((None, tq, LANES), lambda hi, qi, bi, s: (bi, qi, hi)),
        scratch_shapes=[pltpu.VMEM((tq, t), F32), pltpu.VMEM((2, tq, t), F32)],
    )
    return pl.pallas_call(
        functools.partial(_attn_kernel, tq=tq, seq=t, kc=min(512, t), lambda_init=lambda_init),
        grid_spec=grid_spec,
        out_shape=jax.ShapeDtypeStruct((b, t, A_WIDTH), BF16),
        compiler_params=_params(("parallel", "parallel", "arbitrary")),
        name="diff_attention",
    )(slopes, za, za, za, vec(lq1), vec(lk1), vec(lq2), vec(lk2), vec(subln_g))


def _head_sum(x):
    ri = lax.broadcasted_iota(jnp.int32, (LANES, LANES), 0) // R_HEAD
    ci = lax.broadcasted_iota(jnp.int32, (LANES, LANES), 1) // R_HEAD
    ones = (ri == ci).astype(F32)
    return jnp.dot(x, ones, precision=HI, preferred_element_type=F32)


def _head_sum_wide(x):
    return jnp.concatenate(
        [_head_sum(x[:, p * LANES:(p + 1) * LANES]) for p in range(x.shape[1] // LANES)], axis=1)


def _rwkv_prep_kernel(z_ref, zprev_ref, znext_ref, mup_ref, mun_ref, w0_ref, wdu_ref, a0_ref,
                      wiu_ref, wgu_ref, kk_ref, ka_ref, rk_ref,
                      r_out, kkn_out, v_out, lw_out, b_out, kt_out, bonus_out, g_out, *, tt):
    i = pl.program_id(1)
    n_i = pl.num_programs(1)
    c = R_WIDTH
    z = z_ref[...]
    row = lax.broadcasted_iota(jnp.int32, (tt, 1), 0)
    prev_row = jnp.where(i > 0, zprev_ref[7:8, :], 0.0)
    next_row = jnp.where(i < n_i - 1, znext_ref[0:1, :], 0.0)
    zp = jnp.where(row == 0, prev_row, pltpu.roll(z, 1, 0))
    zn = jnp.where(row == tt - 1, next_row, pltpu.roll(z, tt - 1, 0))
    zs = z + mup_ref[...] * (zp - z) + mun_ref[...] * (zn - z)

    r = zs[:, :c]
    k = zs[:, c:2 * c]
    v = zs[:, 2 * c:3 * c]
    o3 = 3 * c
    lane = lax.broadcasted_iota(jnp.int32, (tt, LANES), 1)
    first = lane < DECAY_LORA
    lw = jnp.tanh(zs[:, o3:o3 + LANES])
    la = zs[:, o3 + LANES:o3 + 2 * LANES]
    lg = _sigmoid(zs[:, o3 + 2 * LANES:o3 + 4 * LANES])
    g_out[...] = jnp.dot(lg.astype(BF16), wgu_ref[...], preferred_element_type=F32)

    kk = k * kk_ref[...]
    ss = _head_sum_wide(kk * kk)
    kk = kk * lax.rsqrt(jnp.maximum(ss, 1e-12))
    r_out[...] = r
    kkn_out[...] = kk
    v_out[...] = v

    kt_sum = jnp.zeros_like(k)
    for d in range(2):
        keep = first if d == 0 else jnp.logical_not(first)
        lw_d = jnp.where(keep, lw, 0.0).astype(BF16)
        la_d = jnp.where(keep, la, 0.0).astype(BF16)
        zw = jnp.dot(lw_d, wdu_ref[...], preferred_element_type=F32) + w0_ref[d:d + 1, :]
        lw_out[d] = -math.exp(-0.5) * _sigmoid(zw)
        a = _sigmoid(jnp.dot(la_d, wiu_ref[...], preferred_element_type=F32) + a0_ref[d:d + 1, :])
        kt = k * (1.0 + (a - 1.0) * ka_ref[...])
        kt_out[d] = kt
        b_out[d] = kk * a
        kt_sum = kt_sum + kt
    coef = _head_sum_wide(r * kt_sum * rk_ref[...])
    bonus_out[...] = coef * v


def rwkv_prep(zr, shift_prev, shift_next, w0, w_decay_up, a0, w_iclr_up, w_gate_up, k_k, k_a, r_k,
              tt=128):
    b, t, cp = zr.shape
    tt = min(tt, t)
    c = R_WIDTH
    pad = cp - R_COLS
    row = lambda a: a.reshape(1, -1).astype(F32)
    mup = jnp.pad(row(shift_prev), ((0, 0), (0, pad)))
    mun = jnp.pad(row(shift_next), ((0, 0), (0, pad)))
    wdu = w_decay_up.reshape(2 * DECAY_LORA, c).astype(BF16)
    wiu = w_iclr_up.reshape(2 * ICLR_LORA, c).astype(BF16)
    wgu = jnp.pad(w_gate_up, ((0, 2 * LANES - GATE_LORA), (0, 0))).astype(BF16)
    nb8 = t // 8
    const = lambda shape: pl.BlockSpec(shape, lambda bi, i: (0,) * len(shape))
    in_specs = [
        pl.BlockSpec((None, tt, cp), lambda bi, i: (bi, i, 0)),
        pl.BlockSpec((None, 8, cp), lambda bi, i: (bi, jnp.maximum(i * (tt // 8) - 1, 0), 0)),
        pl.BlockSpec((None, 8, cp), lambda bi, i: (bi, jnp.minimum((i + 1) * (tt // 8), nb8 - 1), 0)),
        const((1, cp)), const((1, cp)), const((2, c)), const((2 * DECAY_LORA, c)), const((2, c)),
        const((2 * ICLR_LORA, c)), const((2 * LANES, c)), const((1, c)), const((1, c)), const((1, c)),
    ]
    one = pl.BlockSpec((None, tt, c), lambda bi, i: (bi, i, 0))
    two = pl.BlockSpec((2, None, tt, c), lambda bi, i: (0, bi, i, 0))
    s1 = jax.ShapeDtypeStruct((b, t, c), F32)
    s2 = jax.ShapeDtypeStruct((2, b, t, c), F32)
    return pl.pallas_call(
        functools.partial(_rwkv_prep_kernel, tt=tt),
        grid=(b, t // tt),
        in_specs=in_specs,
        out_specs=[one, one, one, two, two, two, one, one],
        out_shape=[s1, s1, s1, s2, s2, s2, s1, s1],
        compiler_params=_params(("parallel", "parallel")),
        name="rwkv_prep",
    )(zr, zr, zr, mup, mun, w0.astype(F32), wdu, a0.astype(F32), wiu, wgu, row(k_k), row(k_a), row(r_k))


def _scan_kernel(r0_ref, kk0_ref, v0_ref, r1_ref, kk1_ref, v1_ref, lw0_ref, b0_ref, kt0_ref,
                 lw1_ref, b1_ref, kt1_ref, y0_ref, y1_ref, s_ref, *, chunk):
    ln = chunk

    @pl.when(pl.program_id(1) == 0)
    def _():
        s_ref[...] = jnp.zeros_like(s_ref)

    ri = lax.broadcasted_iota(jnp.int32, (ln, ln), 0)
    ci = lax.broadcasted_iota(jnp.int32, (ln, ln), 1)
    r2 = lax.broadcasted_iota(jnp.int32, (2 * ln, 2 * ln), 0)
    c2 = lax.broadcasted_iota(jnp.int32, (2 * ln, 2 * ln), 1)
    eye = (r2 == c2).astype(F32)
    head0 = lax.broadcasted_iota(jnp.int32, (ln, LANES), 1) < R_HEAD
    lane_c = (((1,), (1,)), ((), ()))
    row_c = (((0,), (0,)), ((), ()))
    n_pairs = r0_ref.shape[1] // LANES

    def dot(a, b):
        return jnp.dot(a.astype(BF16), b.astype(BF16), preferred_element_type=F32)

    def dotg(a, b, dims):
        return lax.dot_general(a.astype(BF16), b.astype(BF16), dims, preferred_element_type=F32)

    def stack(x, sl):
        xs = x[:, sl]
        return jnp.concatenate([jnp.where(head0, xs, 0.0), jnp.where(head0, 0.0, xs)], axis=0)

    ar, bs, ks, vs, gt, strict, incl, out = [], [], [], [], [], [], [], []
    for d, (r_ref, kk_ref, v_ref, lw_ref, b_ref, kt_ref, y_ref) in enumerate((
            (r0_ref, kk0_ref, v0_ref, lw0_ref, b0_ref, kt0_ref, y0_ref),
            (r1_ref, kk1_ref, v1_ref, lw1_ref, b1_ref, kt1_ref, y1_ref))):
        earlier = (ri >= ci) if d == 0 else (ri <= ci)
        lw = lw_ref[...]
        cum = jnp.dot(earlier.astype(F32), lw, precision=HI, preferred_element_type=F32)
        g_inv = jnp.exp(-cum)
        rh = r_ref[...] * jnp.exp(cum)
        ah = kk_ref[...] * jnp.exp(cum - lw)
        bh = b_ref[...] * g_inv
        kh = kt_ref[...] * g_inv
        vv = v_ref[...]
        g_tot = jnp.exp(jnp.sum(lw, axis=0, keepdims=True))
        for p in range(n_pairs):
            sl = slice(p * LANES, (p + 1) * LANES)
            ar.append(jnp.concatenate([stack(ah, sl), stack(rh, sl)], axis=0).astype(BF16))
            bs.append(stack(bh, sl).astype(BF16))
            ks.append(stack(kh, sl).astype(BF16))
            vs.append(stack(vv, sl).astype(BF16))
            gt.append(g_tot[:, sl])
            strict.append((r2 > c2) if d == 0 else (r2 < c2))
            incl.append((r2 >= c2) if d == 0 else (r2 <= c2))
            out.append((y_ref, sl))
    jobs = range(len(ar))
    g1 = [dotg(ar[j], jnp.concatenate([bs[j], ks[j]], axis=0), lane_c) for j in jobs]
    n = [jnp.where(strict[j], g1[j][:2 * ln, :2 * ln], 0.0) for j in jobs]
    m_akrk = [jnp.concatenate([jnp.where(strict[j], g1[j][:2 * ln, 2 * ln:], 0.0),
                               jnp.where(incl[j], g1[j][2 * ln:, 2 * ln:], 0.0)], axis=0).astype(BF16)
              for j in jobs]
    m_rb = [jnp.where(incl[j], g1[j][2 * ln:, :2 * ln], 0.0).astype(BF16) for j in jobs]
    x = [eye - n[j] for j in jobs]
    pw = n
    for _ in range(int(math.log2(ln)) - 1):
        pw = [dot(pw[j], pw[j]) for j in jobs]
        x = [x[j] + dot(x[j], pw[j]) for j in jobs]
    s = [s_ref[j] for j in jobs]
    asrs = [dotg(ar[j], s[j], lane_c) for j in jobs]
    mv = [dot(m_akrk[j], vs[j]) for j in jobs]
    u = [dot(x[j], asrs[j][:2 * ln] + mv[j][:2 * ln]) for j in jobs]
    ys = [asrs[j][2 * ln:] + mv[j][2 * ln:] - dot(m_rb[j], u[j]) for j in jobs]
    for j in jobs:
        y_ref, sl = out[j]
        y_ref[:, sl] = ys[j][:ln] + ys[j][ln:]
    ds = [dotg(jnp.concatenate([vs[j], (-u[j]).astype(BF16)], axis=0),
               jnp.concatenate([ks[j], bs[j]], axis=0), row_c) for j in jobs]
    for j in jobs:
        s_ref[j] = (s[j] + ds[j]) * gt[j]


def rwkv_scan(r, kk, v, lw, b, kt, chunk=CHUNK):
    bsz, t, c = r.shape
    chunk = min(chunk, t)
    nc = t // chunk
    fwd = pl.BlockSpec((None, chunk, c), lambda bi, ci: (bi, ci, 0))
    bwd = pl.BlockSpec((None, chunk, c), lambda bi, ci: (bi, nc - 1 - ci, 0))
    fwd_d = pl.BlockSpec((None, None, chunk, c), lambda bi, ci: (0, bi, ci, 0))
    bwd_d = pl.BlockSpec((None, None, chunk, c), lambda bi, ci: (1, bi, nc - 1 - ci, 0))
    shape = jax.ShapeDtypeStruct((bsz, t, c), F32)
    return pl.pallas_call(
        functools.partial(_scan_kernel, chunk=chunk),
        grid=(bsz, nc),
        in_specs=[fwd, fwd, fwd, bwd, bwd, bwd, fwd_d, fwd_d, fwd_d, bwd_d, bwd_d, bwd_d],
        out_specs=[fwd, bwd],
        out_shape=[shape, shape],
        scratch_shapes=[pltpu.VMEM((2 * c // LANES, LANES, LANES), F32)],
        compiler_params=_params(("parallel", "arbitrary")),
        name="rwkv_scan",
    )(r, kk, v, r, kk, v, lw, b, kt, lw, b, kt)


def _rwkv_post_kernel(y0_ref, y1_ref, bonus_ref, g_ref, lng_ref, lnb_ref, o_ref):
    y = y0_ref[...] + y1_ref[...]
    mu = _head_sum_wide(y) * (1.0 / R_HEAD)
    yc = y - mu
    var = _head_sum_wide(yc * yc) * (1.0 / R_HEAD)
    yn = yc * lax.rsqrt(var + GN_EPS) * lng_ref[...] + lnb_ref[...]
    o_ref[...] = ((yn + bonus_ref[...]) * g_ref[...]).astype(o_ref.dtype)


def rwkv_post(y0, y1, bonus, g, ln_g, ln_b, tt=256):
    b, t, c = y0.shape
    tt = min(tt, t)
    row = lambda a: a.reshape(1, -1).astype(F32)
    one = pl.BlockSpec((None, tt, c), lambda bi, i: (bi, i, 0))
    const = pl.BlockSpec((1, c), lambda bi, i: (0, 0))
    return pl.pallas_call(
        _rwkv_post_kernel,
        grid=(b, t // tt),
        in_specs=[one, one, one, one, const, const],
        out_specs=one,
        out_shape=jax.ShapeDtypeStruct((b, t, c), BF16),
        compiler_params=_params(("parallel", "parallel")),
        name="rwkv_post",
    )(y0, y1, bonus, g, row(ln_g), row(ln_b))


def _merge_kernel(u_ref, ya_ref, yr_ref, wga_ref, wgr_ref, wa_ref, wr_ref, o_ref):
    u = u_ref[...]
    ga = _sigmoid(jnp.dot(u, wga_ref[...], preferred_element_type=F32))
    gr = _sigmoid(jnp.dot(u, wgr_ref[...], preferred_element_type=F32))
    a = jnp.dot(ya_ref[...], wa_ref[...], preferred_element_type=F32)
    r = jnp.dot(yr_ref[...], wr_ref[...], preferred_element_type=F32)
    o_ref[...] = (ga * a + gr * r).astype(o_ref.dtype)


def merge_branches(u, ya, yr, w_gate, wa, wr, tm=1024, tn=512):
    m, d = u.shape
    ka = ya.shape[1]
    kr = yr.shape[1]
    n = wa.shape[1]
    tm = min(tm, m)
    nj = n // tn
    return pl.pallas_call(
        _merge_kernel,
        grid=(m // tm, nj),
        in_specs=[pl.BlockSpec((tm, d), lambda i, j: (i, 0)),
                  pl.BlockSpec((tm, ka), lambda i, j: (i, 0)),
                  pl.BlockSpec((tm, kr), lambda i, j: (i, 0)),
                  pl.BlockSpec((d, tn), lambda i, j: (0, j)),
                  pl.BlockSpec((d, tn), lambda i, j: (0, j + nj)),
                  pl.BlockSpec((ka, tn), lambda i, j: (0, j)),
                  pl.BlockSpec((kr, tn), lambda i, j: (0, j))],
        out_specs=pl.BlockSpec((tm, tn), lambda i, j: (i, j)),
        out_shape=jax.ShapeDtypeStruct((m, n), BF16),
        compiler_params=_params(("parallel", "parallel")),
        name="merge_branches",
    )(u, ya, yr, w_gate, w_gate, wa, wr)


def _router_kernel(u_ref, w_ref, b_ref, idx_ref, gs_ref, rank_ref, cnt_ref, carry_ref, *, tm):
    i = pl.program_id(0)

    @pl.when(i == 0)
    def _():
        carry_ref[...] = jnp.zeros_like(carry_ref)

    logits = jnp.dot(u_ref[...], w_ref[...], precision=HI, preferred_element_type=F32)
    scores = _sigmoid(logits)
    cur = scores + b_ref[...]
    lane = lax.broadcasted_iota(jnp.int32, (tm, N_EXPERTS), 1).astype(F32)
    lane_out = lax.broadcasted_iota(jnp.int32, (tm, LANES), 1)
    picks = []
    sel_f = jnp.zeros((tm, N_EXPERTS), F32)
    idx_out = jnp.zeros((tm, LANES), F32)
    for k in range(TOP_K):
        best = jnp.max(cur, axis=-1, keepdims=True)
        ik = jnp.min(jnp.where(cur == best, lane, float(N_EXPERTS)), axis=-1, keepdims=True)
        onehot = lane == ik
        picks.append(onehot)
        sel_f = jnp.where(onehot, 1.0, sel_f)
        cur = jnp.where(onehot, -jnp.inf, cur)
        idx_out = jnp.where(lane_out == k, ik, idx_out)
    gsel = scores * sel_f
    gsel = gsel / jnp.sum(gsel, axis=-1, keepdims=True) * ROUTED_SCALE
    ri = lax.broadcasted_iota(jnp.int32, (tm, tm), 0)
    ci = lax.broadcasted_iota(jnp.int32, (tm, tm), 1)
    before = jnp.where(ri > ci, 1.0, 0.0).astype(BF16)
    rank =jnp.dot(before, sel_f.astype(BF16), preferred_element_type=F32) + carry_ref[...]
    carry_ref[...] = carry_ref[...] + jnp.sum(sel_f, axis=0, keepdims=True)
    cnt_ref[...] = carry_ref[...].astype(jnp.int32)
    gs_out = jnp.zeros((tm, LANES), F32)
    rank_out = jnp.zeros((tm, LANES), F32)
    for k in range(TOP_K):
        gk = jnp.sum(jnp.where(picks[k], gsel, 0.0), axis=-1, keepdims=True)
        rk = jnp.sum(jnp.where(picks[k], rank, 0.0), axis=-1, keepdims=True)
        gs_out = jnp.where(lane_out == k, gk, gs_out)
        rank_out = jnp.where(lane_out == k, rk, rank_out)
    idx_ref[...] = idx_out.astype(jnp.int32)
    gs_ref[...] = gs_out
    rank_ref[...] = rank_out.astype(jnp.int32)


def router(u, w_router, b_router, tm=256):
    n, d = u.shape
    tm = min(tm, n)
    tile = pl.BlockSpec((tm, LANES), lambda i: (i, 0))
    return pl.pallas_call(
        functools.partial(_router_kernel, tm=tm),
        grid=(n // tm,),
        in_specs=[pl.BlockSpec((tm, d), lambda i: (i, 0)),
                  pl.BlockSpec((d, N_EXPERTS), lambda i: (0, 0)),
                  pl.BlockSpec((1, N_EXPERTS), lambda i: (0, 0))],
        out_specs=[tile, tile, tile, pl.BlockSpec((1, N_EXPERTS), lambda i: (0, 0))],
        out_shape=[jax.ShapeDtypeStruct((n, LANES), jnp.int32),
                   jax.ShapeDtypeStruct((n, LANES), F32),
                   jax.ShapeDtypeStruct((n, LANES), jnp.int32),
                   jax.ShapeDtypeStruct((1, N_EXPERTS), jnp.int32)],
        scratch_shapes=[pltpu.VMEM((1, N_EXPERTS), F32)],
        compiler_params=_params(("arbitrary",)),
        name="router",
    )(u, w_router.astype(F32), b_router.reshape(1, -1).astype(F32))


def _dispatch_kernel(dest_ref, tail_ref, nused_ref, u_ref, xs_ref, zbuf, zsem, sem, *, td, bm):
    step = pl.program_id(0)
    n_blocks = xs_ref.shape[0] // bm

    @pl.when(step == 0)
    def _():
        zbuf[...] = jnp.zeros_like(zbuf)

        def zero_block(blk):
            return pltpu.make_async_copy(zbuf, xs_ref.at[pl.ds(blk * bm, bm)], zsem)

        def tails(e, carry):
            @pl.when(tail_ref[e] >= 0)
            def _():
                zero_block(tail_ref[e]).start()
            return carry

        def tails_wait(e, carry):
            @pl.when(tail_ref[e] >= 0)
            def _():
                zero_block(tail_ref[e]).wait()
            return carry

        def unused(blk, carry):
            zero_block(blk).start()
            return carry

        def unused_wait(blk, carry):
            zero_block(blk).wait()
            return carry

        lax.fori_loop(0, N_EXPERTS, tails, 0)
        lax.fori_loop(nused_ref[0], n_blocks, unused, 0)
        lax.fori_loop(0, N_EXPERTS, tails_wait, 0)
        lax.fori_loop(nused_ref[0], n_blocks, unused_wait, 0)

    base = step * (td * TOP_K)

    for i in range(td):
        src = u_ref.at[pl.ds(i, 1)]
        for k in range(TOP_K):
            pltpu.make_async_copy(src, xs_ref.at[pl.ds(dest_ref[base + i * TOP_K + k], 1)],
                                  sem).start(priority=k % 2)

    tile = xs_ref.at[pl.ds(0, td * TOP_K)]
    pltpu.make_async_copy(tile, tile, sem).wait()


def dispatch(u, dest, tail_blk, n_used, cap, bm, td=64):
    n, d = u.shape
    td = min(td, n)
    grid_spec = pltpu.PrefetchScalarGridSpec(
        num_scalar_prefetch=3,
        grid=(n // td,),
        in_specs=[pl.BlockSpec((td, d), lambda i, *_: (i, 0))],
        out_specs=pl.BlockSpec(memory_space=pl.ANY),
        scratch_shapes=[pltpu.VMEM((bm, d), u.dtype), pltpu.SemaphoreType.DMA(()),
                        pltpu.SemaphoreType.DMA(())],
    )
    return pl.pallas_call(
        functools.partial(_dispatch_kernel, td=td, bm=bm),
        grid_spec=grid_spec,
        out_shape=jax.ShapeDtypeStruct((cap, d), u.dtype),
        compiler_params=_params(("arbitrary",)),
        name="moe_dispatch",
    )(dest.reshape(-1), tail_blk, n_used, u)


def _expert_kernel(be_ref, nused_ref, x_ref, wg_ref, wu_ref, wd_ref, y_ref, wg_bf, wu_bf, wd_bf):
    i = pl.program_id(0)

    @pl.when(jnp.logical_or(i == 0, be_ref[i] != be_ref[jnp.maximum(i - 1, 0)]))
    def _():
        wg_bf[...] = wg_ref[...].astype(BF16)
        wu_bf[...] = wu_ref[...].astype(BF16)
        wd_bf[...] = wd_ref[...].astype(BF16)

    @pl.when(i < nused_ref[0])
    def _():
        x = x_ref[...].astype(BF16)
        hg = jnp.dot(x, wg_bf[...], preferred_element_type=F32)
        hu = jnp.dot(x, wu_bf[...], preferred_element_type=F32)
        hb = (hg * _sigmoid(hg) * hu).astype(BF16)
        y_ref[...] = jnp.dot(hb, wd_bf[...], preferred_element_type=F32)

    @pl.when(i >= nused_ref[0])
    def _():
        y_ref[...] = jnp.zeros_like(y_ref)


def expert_ffn(xs, block_e, n_used, wg, wu, wd, bm):
    cap, d = xs.shape
    f = wg.shape[2]
    row = lambda i, be, nu: (jnp.maximum(jnp.minimum(i, nu[0] - 1), 0), 0)
    grid_spec = pltpu.PrefetchScalarGridSpec(
        num_scalar_prefetch=2,
        grid=(cap // bm,),
        in_specs=[pl.BlockSpec((bm, d), row),
                  pl.BlockSpec((None, d, f), lambda i, be, nu: (be[i], 0, 0)),
                  pl.BlockSpec((None, d, f), lambda i, be, nu: (be[i], 0, 0)),
                  pl.BlockSpec((None, f, d), lambda i, be, nu: (be[i], 0, 0))],
        out_specs=pl.BlockSpec((bm, d), lambda i, be, nu: (i, 0)),
        scratch_shapes=[pltpu.VMEM((d, f), BF16), pltpu.VMEM((d, f), BF16), pltpu.VMEM((f, d), BF16)],
    )
    return pl.pallas_call(
        _expert_kernel,
        grid_spec=grid_spec,
        out_shape=jax.ShapeDtypeStruct((cap, d), F32),
        compiler_params=_params(("arbitrary",)),
        name="moe_experts",
    )(block_e, n_used, xs, wg, wu, wd)


def _shared_kernel(u_ref, h_ref, wg_ref, wu_ref, wd_ref, o_ref):
    x = u_ref[...].astype(BF16)
    hg = jnp.dot(x, wg_ref[...], preferred_element_type=F32)
    hu = jnp.dot(x, wu_ref[...], preferred_element_type=F32)
    hb = (hg * _sigmoid(hg) * hu).astype(BF16)
    o_ref[...] = h_ref[...] + jnp.dot(hb, wd_ref[...], preferred_element_type=F32)


def shared_ffn(u, h, wg, wu, wd, tm=512):
    n, d = u.shape
    f = wg.shape[1]
    tm = min(tm, n)
    tile = pl.BlockSpec((tm, d), lambda i: (i, 0))
    return pl.pallas_call(
        _shared_kernel,
        grid=(n // tm,),
        in_specs=[tile, tile,
                  pl.BlockSpec((d, f), lambda i: (0, 0)),
                  pl.BlockSpec((d, f), lambda i: (0, 0)),
                  pl.BlockSpec((f, d), lambda i: (0, 0))],
        out_specs=tile,
        out_shape=jax.ShapeDtypeStruct((n, d), F32),
        compiler_params=_params(("parallel",)),
        name="shared_ffn",
    )(u, h, wg, wu, wd)


def _combine_kernel(dest_ref, gs_ref, hs_ref, g_ref, y_ref, h_out, u_out, buf_a, buf_b, sems, *, tt):
    step = pl.program_id(0)
    last = pl.num_programs(0) - 1

    def row_copy(tile, r, k, buf, sem):
        return pltpu.make_async_copy(y_ref.at[pl.ds(dest_ref[(tile * tt + r) * TOP_K + k], 1)],
                                     buf.at[k, pl.ds(r, 1)], sem)

    def wait_rows(buf, sem):
        pltpu.make_async_copy(buf, buf, sem).wait()

    @pl.when(step == 0)
    def _():
        def row(r, carry):
            for k in range(TOP_K):
                row_copy(0, r, k, buf_a, sems.at[0]).start()
            return carry

        lax.fori_loop(0, tt, row, 0)

    def phase(cur, cur_sem, nxt, nxt_sem):
        wait_rows(cur, cur_sem)
        nxt_tile = jnp.minimum(step + 1, last)
        for r in range(tt):
            for k in range(TOP_K):
                row_copy(nxt_tile, r, k, nxt, nxt_sem).start(priority=k % 2)
        gs = gs_ref[...]
        h = hs_ref[...]
        for k in range(TOP_K):
            h = h + gs[:, k:k + 1] * cur[k]
        h_out[...] = h
        ms = jnp.mean(h * h, axis=-1, keepdims=True)
        u_out[...] = (h * lax.rsqrt(ms + RMS_EPS) * g_ref[...]).astype(u_out.dtype)

        @pl.when(step == last)
        def _():
            wait_rows(nxt, nxt_sem)

    @pl.when(step % 2 == 0)
    def _():
        phase(buf_a, sems.at[0], buf_b, sems.at[1])

    @pl.when(step % 2 == 1)
    def _():
        phase(buf_b, sems.at[1], buf_a, sems.at[0])


def combine(y, dest, gsel, hs, g_next, tt=64):
    n, d = hs.shape
    tt = min(tt, n)
    tile = pl.BlockSpec((tt, d), lambda i, s: (i, 0))
    grid_spec = pltpu.PrefetchScalarGridSpec(
        num_scalar_prefetch=1,
        grid=(n // tt,),
        in_specs=[pl.BlockSpec((tt, LANES), lambda i, s: (i, 0)),
                  tile,
                  pl.BlockSpec((1, d), lambda i, s: (0, 0)),
                  pl.BlockSpec(memory_space=pl.ANY)],
        out_specs=[tile, tile],
        scratch_shapes=[pltpu.VMEM((TOP_K, tt, d), F32), pltpu.VMEM((TOP_K, tt, d), F32),
                        pltpu.SemaphoreType.DMA((2,))],
    )
    return pl.pallas_call(
        functools.partial(_combine_kernel, tt=tt),
        grid_spec=grid_spec,
        out_shape=[jax.ShapeDtypeStruct((n, d), F32), jax.ShapeDtypeStruct((n, d), BF16)],
        compiler_params=_params(("arbitrary",)),
        name="moe_combine",
    )(dest.reshape(-1), gsel, hs, g_next.reshape(1, -1).astype(F32), y)


def moe_ffn(h, g_norm, w_router, b_router, w_e_gate, w_e_up, w_e_down, w_s_gate, w_s_up, w_s_down,
            g_next, bm=EXPERT_BLOCK):
    n, d = h.shape
    u = rmsnorm(h, g_norm, F32)
    u_bf = u.astype(BF16)
    idx128, gs128, rank128, counts = router(u, w_router, b_router)
    idx = idx128[:, :TOP_K]
    rank = rank128[:, :TOP_K]
    counts = counts.reshape(-1)
    n_blk = (counts + bm - 1) // bm
    blk_end = jnp.cumsum(n_blk)
    pstart = (blk_end - n_blk) * bm
    experts = jnp.arange(N_EXPERTS, dtype=jnp.int32)
    dest = rank + jnp.sum(jnp.where(idx[:, :, None] == experts, pstart, 0), axis=-1)
    dest = dest.astype(jnp.int32)
    cap = (n * TOP_K // bm + N_EXPERTS) * bm
    blocks = jnp.arange(cap // bm, dtype=jnp.int32)
    block_e = jnp.minimum(jnp.sum(blk_end[None, :] <= blocks[:, None], axis=1), N_EXPERTS - 1)
    block_e = block_e.astype(jnp.int32)
    n_used = blk_end[-1:].astype(jnp.int32)
    tail_blk = jnp.where(n_blk > 0, blk_end - 1, -1).astype(jnp.int32)
    xs = dispatch(u, dest, tail_blk, n_used, cap, bm)
    y = expert_ffn(xs, block_e, n_used, w_e_gate.astype(F32), w_e_up.astype(F32), w_e_down.astype(F32), bm)
    hs = shared_ffn(u_bf, h, w_s_gate.astype(BF16), w_s_up.astype(BF16), w_s_down.astype(BF16))
    return combine(y, dest, gs128, hs, g_next)


def _ple_kernel(u_ref, p_ref, h_ref, wg_ref, wp_ref, gf_ref, o_ref):
    gate = _sigmoid(jnp.dot(u_ref[...], wg_ref[...], preferred_element_type=F32))
    proj = jnp.dot(p_ref[...].astype(BF16), wp_ref[...], preferred_element_type=F32)
    h = h_ref[...] + gate * proj
    ms = jnp.mean(h * h, axis=-1, keepdims=True)
    o_ref[...] = h * lax.rsqrt(ms + RMS_EPS) * gf_ref[...]


def ple_final(u, p, h, w_gate, w_proj, g_final, tm=256):
    n, d = h.shape
    pd = p.shape[1]
    tm = min(tm, n)
    tile = pl.BlockSpec((tm, d), lambda i: (i, 0))
    return pl.pallas_call(
        _ple_kernel,
        grid=(n // tm,),
        in_specs=[tile, pl.BlockSpec((tm, pd), lambda i: (i, 0)), tile,
                  pl.BlockSpec((d, d), lambda i: (0, 0)),
                  pl.BlockSpec((pd, d), lambda i: (0, 0)),
                  pl.BlockSpec((1, d), lambda i: (0, 0))],
        out_specs=tile,
        out_shape=jax.ShapeDtypeStruct((n, d), F32),
        compiler_params=_params(("parallel",)),
        name="ple_final",
    )(u, p, h, w_gate, w_proj, g_final.reshape(1, -1).astype(F32))


def token_mixing(x2, bsz, seq, norm_g, w_in, w_branch_gate, lq1, lk1, lq2, lk2, subln_g, shift_prev,
                 shift_next, w0, w_decay_up, a0, w_iclr_up, w_gate_up, k_k, k_a, r_k, ln_x_g, ln_x_b,
                 w_br_attn, w_br_rwkv, w_out, lambda_init):
    d = x2.shape[1]
    u = rmsnorm(x2, norm_g, BF16)
    w_attn = w_in[:, :A_COLS].astype(BF16)
    w_rwkv = jnp.pad(w_in[:, A_COLS:], ((0, 0), (0, R_COLS_PAD - R_COLS))).astype(BF16)
    za = matmul(u, w_attn, out_dtype=BF16, epilogue=functools.partial(_scale_q_epilogue, tn=512),
                tm=2048, tn=512, name="proj_attn")
    zr = matmul(u, w_rwkv, out_dtype=F32, tm=2048, tn=512, name="proj_rwkv")
    ya = diff_attention(za.reshape(bsz, seq, A_COLS), lq1, lk1, lq2, lk2, subln_g, lambda_init)
    r, kk, v, lw, b, kt, bonus, g = rwkv_prep(
        zr.reshape(bsz, seq, R_COLS_PAD), shift_prev, shift_next, w0, w_decay_up, a0, w_iclr_up,
        w_gate_up, k_k, k_a, r_k)
    y0, y1 = rwkv_scan(r, kk, v, lw, b, kt)
    yr = rwkv_post(y0, y1, bonus, g, ln_x_g, ln_x_b)
    merged = merge_branches(u, ya.reshape(-1, A_WIDTH), yr.reshape(-1, R_WIDTH),
                            w_branch_gate.astype(BF16), w_br_attn.astype(BF16), w_br_rwkv.astype(BF16))
    return matmul(merged, w_out.astype(BF16), out_dtype=F32, epilogue=_residual_add, extras=(x2,),
                  tm=2048, tn=512, name="out_proj")


def kernel(x, p, norm_mix_g, w_in, w_branch_gate, lambda_q1, lambda_k1, lambda_q2, lambda_k2, subln_g, shift_prev, shift_next, w0, w_decay_up, a0, w_iclr_up, w_gate_up, k_k, k_a, r_k, ln_x_g, ln_x_b, w_br_attn, w_br_rwkv, w_out, norm_ffn_g, w_router, b_router, w_e_gate, w_e_up, w_e_down, w_s_gate, w_s_up, w_s_down, norm_ple_g, w_ple_gate, w_ple_proj, norm_final_g):
    bsz, seq, d = x.shape
    depth = w_in.shape[0]
    assert depth == 1, "the final norm is fused into the last layer's embedding kernel"
    h = x.reshape(bsz * seq, d)
    i = 0
    lambda_init = 0.8 - 0.6 * math.exp(-0.3 * i)
    h = token_mixing(h, bsz, seq, norm_mix_g[i], w_in[i], w_branch_gate[i], lambda_q1[i], lambda_k1[i],
                     lambda_q2[i], lambda_k2[i], subln_g[i], shift_prev[i], shift_next[i], w0[i],
                     w_decay_up[i], a0[i], w_iclr_up[i], w_gate_up[i], k_k[i], k_a[i], r_k[i],
                     ln_x_g[i], ln_x_b[i], w_br_attn[i], w_br_rwkv[i], w_out[i], lambda_init)
    h, u = moe_ffn(h, norm_ffn_g[i], w_router[i], b_router[i], w_e_gate[i], w_e_up[i], w_e_down[i],
                   w_s_gate[i], w_s_up[i], w_s_down[i], norm_ple_g[i])
    out = ple_final(u, p[i].reshape(bsz * seq, -1), h, w_ple_gate[i].astype(BF16),
                    w_ple_proj[i].astype(BF16), norm_final_g)
    return out.reshape(bsz, seq, d)
```

```python
import functools
import math

import jax
import jax.numpy as jnp
from jax import lax
from jax.experimental import pallas as pl
from jax.experimental.pallas import tpu as pltpu

F32 = jnp.float32
BF16 = jnp.bfloat16
HI = lax.Precision.HIGHEST

D_MODEL = 2048
PLE_DIM = 256
A_HEADS = 8
A_QK_DIM = 64
A_V_DIM = 2 * A_QK_DIM
A_WIDTH = A_HEADS * A_V_DIM
R_HEADS = 16
R_HEAD = 64
R_WIDTH = R_HEADS * R_HEAD
DECAY_LORA = 64
ICLR_LORA = 64
GATE_LORA = 160
N_EXPERTS = 64
TOP_K = 8
EXPERT_FF = 512
SHARED_FF = 512
ROUTED_SCALE = 2.5
RMS_EPS = 1e-6
GN_EPS = 64e-5
LOG2E = math.log2(math.e)
Q_COLS = A_HEADS * 2 * A_QK_DIM
A_COLS = 2 * Q_COLS + A_WIDTH
R_COLS = 3 * R_WIDTH + 2 * DECAY_LORA + 2 * ICLR_LORA + GATE_LORA
R_COLS_PAD = 3584
LANES = 128
CHUNK = 64
EXPERT_BLOCK = 256
VMEM_LIMIT = 56 * 1024 * 1024


def _params(sem):
    return pltpu.CompilerParams(dimension_semantics=sem, vmem_limit_bytes=VMEM_LIMIT)


def _sigmoid(x):
    return 1.0 / (1.0 + jnp.exp(-x))


def _rmsnorm_kernel(x_ref, g_ref, o_ref):
    x = x_ref[...].astype(F32)
    ms = jnp.mean(x * x, axis=-1, keepdims=True)
    o_ref[...] = (x * lax.rsqrt(ms + RMS_EPS) * g_ref[...]).astype(o_ref.dtype)


def rmsnorm(x, g, out_dtype, tm=512):
    m, d = x.shape
    tm = min(tm, m)
    return pl.pallas_call(
        _rmsnorm_kernel,
        grid=(m // tm,),
        in_specs=[pl.BlockSpec((tm, d), lambda i: (i, 0)),
                  pl.BlockSpec((1, d), lambda i: (0, 0))],
        out_specs=pl.BlockSpec((tm, d), lambda i: (i, 0)),
        out_shape=jax.ShapeDtypeStruct((m, d), out_dtype),
        compiler_params=_params(("parallel",)),
        name="rmsnorm",
    )(x, g.reshape(1, d).astype(F32))


def _mm_kernel(x_ref, w_ref, *rest, epilogue):
    o_ref = rest[-1]
    acc = jnp.dot(x_ref[...], w_ref[...], preferred_element_type=F32)
    if epilogue is not None:
        acc = epilogue(acc, *[e[...] for e in rest[:-1]])
    o_ref[...] = acc.astype(o_ref.dtype)


def matmul(x, w, *, out_dtype, epilogue=None, extras=(), tm=512, tn=512, name="matmul"):
    m, k = x.shape
    n = w.shape[1]
    tm = min(tm, m)
    tn = min(tn, n)
    in_specs = [pl.BlockSpec((tm, k), lambda i, j: (i, 0)),
                pl.BlockSpec((k, tn), lambda i, j: (0, j))]
    in_specs += [pl.BlockSpec((tm, tn), lambda i, j: (i, j)) for _ in extras]
    return pl.pallas_call(
        functools.partial(_mm_kernel, epilogue=epilogue),
        grid=(m // tm, pl.cdiv(n, tn)),
        in_specs=in_specs,
        out_specs=pl.BlockSpec((tm, tn), lambda i, j: (i, j)),
        out_shape=jax.ShapeDtypeStruct((m, n), out_dtype),
        compiler_params=_params(("parallel", "parallel")),
        name=name,
    )(x, w, *extras)


def _residual_add(acc, res):
    return res + acc


def _scale_q_epilogue(acc, *, tn):
    is_q = pl.program_id(1) < Q_COLS // tn
    return acc * jnp.where(is_q, LOG2E * A_QK_DIM ** -0.5, 1.0)


def _attn_kernel(slopes_ref, q_ref, k_ref, v_ref, lq1_ref, lk1_ref, lq2_ref, lk2_ref, sg_ref,
                 o_ref, bias_ref, s_ref, *, tq, seq, kc, lambda_init):
    h = pl.program_id(0)
    qi = pl.program_id(1)

    @pl.when(pl.program_id(2) == 0)
    def _():
        qpos = qi * tq + lax.broadcasted_iota(jnp.int32, (tq, seq), 0)
        kpos = lax.broadcasted_iota(jnp.int32, (tq, seq), 1)
        bias_ref[...] = (-LOG2E * slopes_ref[h]) * jnp.abs(qpos - kpos).astype(F32)

    lane = lax.broadcasted_iota(jnp.int32, (tq, LANES), 1)
    q = q_ref[...]
    zero = jnp.zeros_like(q)
    qm = [jnp.where(lane < A_QK_DIM, q, zero), jnp.where(lane < A_QK_DIM, zero, q)]
    chunks = [slice(c * kc, (c + 1) * kc) for c in range(seq // kc)]
    dims = (((1,), (1,)), ((), ()))

    def scores(m, row_max):
        for sl in chunks:
            sc = lax.dot_general(qm[m], k_ref[sl, :], dims, preferred_element_type=F32) + bias_ref[:, sl]
            s_ref[m, :, sl] = sc
            cmax = jnp.max(sc, axis=-1, keepdims=True)
            row_max = cmax if row_max is None else jnp.maximum(row_max, cmax)
        return row_max

    def weighted_values(m, row_max):
        l = jnp.zeros((tq, 1), F32)
        acc = jnp.zeros((tq, LANES), F32)
        for sl in chunks:
            e = jnp.exp2(s_ref[m, :, sl] - row_max)
            l = l + jnp.sum(e, axis=-1, keepdims=True)
            acc = acc + jnp.dot(e.astype(BF16), v_ref[sl, :], preferred_element_type=F32)
        return acc / l

    max0 = scores(0, None)
    max1 = scores(1, None)
    o0 = weighted_values(0, max0)
    o1 = weighted_values(1, max1)
    lam = (jnp.exp(jnp.sum(lq1_ref[...] * lk1_ref[...], axis=-1, keepdims=True))
           - jnp.exp(jnp.sum(lq2_ref[...] * lk2_ref[...], axis=-1, keepdims=True))
           + lambda_init)
    o = o0 - lam * o1
    ms = jnp.mean(o * o, axis=-1, keepdims=True)
    o = o * lax.rsqrt(ms + RMS_EPS) * sg_ref[...]
    o_ref[...] = (o * (1.0 - lambda_init)).astype(o_ref.dtype)


def diff_attention(za, lq1, lk1, lq2, lk2, subln_g, lambda_init, tq=256):
    b, t, _ = za.shape
    tq = min(tq, t)
    h = jnp.arange(1, A_HEADS + 1, dtype=F32)
    slopes = jnp.exp2(-8.0 * h / A_HEADS)
    vec = lambda a: a.reshape(1, -1).astype(F32)
    small = lambda n: pl.BlockSpec((1, n), lambda hi, qi, bi, s: (0, 0))
    grid_spec = pltpu.PrefetchScalarGridSpec(
        num_scalar_prefetch=1,
        grid=(A_HEADS, t // tq, b),
        in_specs=[
            pl.BlockSpec((None, tq, LANES), lambda hi, qi, bi, s: (bi, qi, hi)),
            pl.BlockSpec((None, t, LANES), lambda hi, qi, bi, s: (bi, 0, A_HEADS + hi)),
            pl.BlockSpec((None, t, LANES), lambda hi, qi, bi, s: (bi, 0, 2 * A_HEADS + hi)),
            small(A_QK_DIM), small(A_QK_DIM), small(A_QK_DIM), small(A_QK_DIM), small(A_V_DIM),
        ],
        out_specs=pl.BlockSpec((None, tq, LANES), lambda hi, qi, bi, s: (bi, qi, hi)),
        scratch_shapes=[pltpu.VMEM((tq, t), F32), pltpu.VMEM((2, tq, t), F32)],
    )
    return pl.pallas_call(
        functools.partial(_attn_kernel, tq=tq, seq=t, kc=min(512, t), lambda_init=lambda_init),
        grid_spec=grid_spec,
        out_shape=jax.ShapeDtypeStruct((b, t, A_WIDTH), BF16),
        compiler_params=_params(("parallel", "parallel", "arbitrary")),
        name="diff_attention",
    )(slopes, za, za, za, vec(lq1), vec(lk1), vec(lq2), vec(lk2), vec(subln_g))


def _head_sum(x):
    ri = lax.broadcasted_iota(jnp.int32, (LANES, LANES), 0) // R_HEAD
    ci = lax.broadcasted_iota(jnp.int32, (LANES, LANES), 1) // R_HEAD
    ones = (ri == ci).astype(F32)
    return jnp.dot(x, ones, precision=HI, preferred_element_type=F32)


def _head_sum_wide(x):
    return jnp.concatenate(
        [_head_sum(x[:, p * LANES:(p + 1) * LANES]) for p in range(x.shape[1] // LANES)], axis=1)


def _rwkv_prep_kernel(z_ref, zprev_ref, znext_ref, mup_ref, mun_ref, w0_ref, wdu_ref, a0_ref,
                      wiu_ref, wgu_ref, kk_ref, ka_ref, rk_ref,
                      r_out, kkn_out, v_out, lw_out, b_out, kt_out, bonus_out, g_out, *, tt):
    i = pl.program_id(1)
    n_i = pl.num_programs(1)
    c = R_WIDTH
    z = z_ref[...]
    row = lax.broadcasted_iota(jnp.int32, (tt, 1), 0)
    prev_row = jnp.where(i > 0, zprev_ref[7:8, :], 0.0)
    next_row = jnp.where(i < n_i - 1, znext_ref[0:1, :], 0.0)
    zp = jnp.where(row == 0, prev_row, pltpu.roll(z, 1, 0))
    zn = jnp.where(row == tt - 1, next_row, pltpu.roll(z, tt - 1, 0))
    zs = z + mup_ref[...] * (zp - z) + mun_ref[...] * (zn - z)

    r = zs[:, :c]
    k = zs[:, c:2 * c]
    v = zs[:, 2 * c:3 * c]
    o3 = 3 * c
    lane = lax.broadcasted_iota(jnp.int32, (tt, LANES), 1)
    first = lane < DECAY_LORA
    lw = jnp.tanh(zs[:, o3:o3 + LANES])
    la = zs[:, o3 + LANES:o3 + 2 * LANES]
    lg = _sigmoid(zs[:, o3 + 2 * LANES:o3 + 4 * LANES])
    g_out[...] = jnp.dot(lg.astype(BF16), wgu_ref[...], preferred_element_type=F32)

    kk = k * kk_ref[...]
    ss = _head_sum_wide(kk * kk)
    kk = kk * lax.rsqrt(jnp.maximum(ss, 1e-12))
    r_out[...] = r
    kkn_out[...] = kk
    v_out[...] = v

    kt_sum = jnp.zeros_like(k)
    for d in range(2):
        keep = first if d == 0 else jnp.logical_not(first)
        lw_d = jnp.where(keep, lw, 0.0).astype(BF16)
        la_d = jnp.where(keep, la, 0.0).astype(BF16)
        zw = jnp.dot(lw_d, wdu_ref[...], preferred_element_type=F32) + w0_ref[d:d + 1, :]
        lw_out[d] = -math.exp(-0.5) * _sigmoid(zw)
        a = _sigmoid(jnp.dot(la_d, wiu_ref[...], preferred_element_type=F32) + a0_ref[d:d + 1, :])
        kt = k * (1.0 + (a - 1.0) * ka_ref[...])
        kt_out[d] = kt
        b_out[d] = kk * a
        kt_sum = kt_sum + kt
    coef = _head_sum_wide(r * kt_sum * rk_ref[...])
    bonus_out[...] = coef * v


def rwkv_prep(zr, shift_prev, shift_next, w0, w_decay_up, a0, w_iclr_up, w_gate_up, k_k, k_a, r_k,
              tt=128):
    b, t, cp = zr.shape
    tt = min(tt, t)
    c = R_WIDTH
    pad = cp - R_COLS
    row = lambda a: a.reshape(1, -1).astype(F32)
    mup = jnp.pad(row(shift_prev), ((0, 0), (0, pad)))
    mun = jnp.pad(row(shift_next), ((0, 0), (0, pad)))
    wdu = w_decay_up.reshape(2 * DECAY_LORA, c).astype(BF16)
    wiu = w_iclr_up.reshape(2 * ICLR_LORA, c).astype(BF16)
    wgu = jnp.pad(w_gate_up, ((0, 2 * LANES - GATE_LORA), (0, 0))).astype(BF16)
    nb8 = t // 8
    const = lambda shape: pl.BlockSpec(shape, lambda bi, i: (0,) * len(shape))
    in_specs = [
        pl.BlockSpec((None, tt, cp), lambda bi, i: (bi, i, 0)),
        pl.BlockSpec((None, 8, cp), lambda bi, i: (bi, jnp.maximum(i * (tt // 8) - 1, 0), 0)),
        pl.BlockSpec((None, 8, cp), lambda bi, i: (bi, jnp.minimum((i + 1) * (tt // 8), nb8 - 1), 0)),
        const((1, cp)), const((1, cp)), const((2, c)), const((2 * DECAY_LORA, c)), const((2, c)),
        const((2 * ICLR_LORA, c)), const((2 * LANES, c)), const((1, c)), const((1, c)), const((1, c)),
    ]
    one = pl.BlockSpec((None, tt, c), lambda bi, i: (bi, i, 0))
    two = pl.BlockSpec((2, None, tt, c), lambda bi, i: (0, bi, i, 0))
    s1 = jax.ShapeDtypeStruct((b, t, c), F32)
    s2 = jax.ShapeDtypeStruct((2, b, t, c), F32)
    return pl.pallas_call(
        functools.partial(_rwkv_prep_kernel, tt=tt),
        grid=(b, t // tt),
        in_specs=in_specs,
        out_specs=[one, one, one, two, two, two, one, one],
        out_shape=[s1, s1, s1, s2, s2, s2, s1, s1],
        compiler_params=_params(("parallel", "parallel")),
        name="rwkv_prep",
    )(zr, zr, zr, mup, mun, w0.astype(F32), wdu, a0.astype(F32), wiu, wgu, row(k_k), row(k_a), row(r_k))


def _scan_kernel(r0_ref, kk0_ref, v0_ref, r1_ref, kk1_ref, v1_ref, lw0_ref, b0_ref, kt0_ref,
                 lw1_ref, b1_ref, kt1_ref, y0_ref, y1_ref, s_ref, *, chunk):
    ln = chunk

    @pl.when(pl.program_id(1) == 0)
    def _():
        s_ref[...] = jnp.zeros_like(s_ref)

    ri = lax.broadcasted_iota(jnp.int32, (ln, ln), 0)
    ci = lax.broadcasted_iota(jnp.int32, (ln, ln), 1)
    r2 = lax.broadcasted_iota(jnp.int32, (2 * ln, 2 * ln), 0)
    c2 = lax.broadcasted_iota(jnp.int32, (2 * ln, 2 * ln), 1)
    eye = (r2 == c2).astype(F32)
    head0 = lax.broadcasted_iota(jnp.int32, (ln, LANES), 1) < R_HEAD
    lane_c = (((1,), (1,)), ((), ()))
    row_c = (((0,), (0,)), ((), ()))
    n_pairs = r0_ref.shape[1] // LANES

    def dot(a, b):
        return jnp.dot(a.astype(BF16), b.astype(BF16), preferred_element_type=F32)

    def dotg(a, b, dims):
        return lax.dot_general(a.astype(BF16), b.astype(BF16), dims, preferred_element_type=F32)

    def stack(x, sl):
        xs = x[:, sl]
        return jnp.concatenate([jnp.where(head0, xs, 0.0), jnp.where(head0, 0.0, xs)], axis=0)

    ar, bs, ks, vs, gt, strict, incl, out = [], [], [], [], [], [], [], []
    for d, (r_ref, kk_ref, v_ref, lw_ref, b_ref, kt_ref, y_ref) in enumerate((
            (r0_ref, kk0_ref, v0_ref, lw0_ref, b0_ref, kt0_ref, y0_ref),
            (r1_ref, kk1_ref, v1_ref, lw1_ref, b1_ref, kt1_ref, y1_ref))):
        earlier = (ri >= ci) if d == 0 else (ri <= ci)
        lw = lw_ref[...]
        cum = jnp.dot(earlier.astype(F32), lw, precision=HI, preferred_element_type=F32)
        g_inv = jnp.exp(-cum)
        rh = r_ref[...] * jnp.exp(cum)
        ah = kk_ref[...] * jnp.exp(cum - lw)
        bh = b_ref[...] * g_inv
        kh = kt_ref[...] * g_inv
        vv = v_ref[...]
        g_tot = jnp.exp(jnp.sum(lw, axis=0, keepdims=True))
        for p in range(n_pairs):
            sl = slice(p * LANES, (p + 1) * LANES)
            ar.append(jnp.concatenate([stack(ah, sl), stack(rh, sl)], axis=0).astype(BF16))
            bs.append(stack(bh, sl).astype(BF16))
            ks.append(stack(kh, sl).astype(BF16))
            vs.append(stack(vv, sl).astype(BF16))
            gt.append(g_tot[:, sl])
            strict.append((r2 > c2) if d == 0 else (r2 < c2))
            incl.append((r2 >= c2) if d == 0 else (r2 <= c2))
            out.append((y_ref, sl))
    jobs = range(len(ar))
    g1 = [dotg(ar[j], jnp.concatenate([bs[j], ks[j]], axis=0), lane_c) for j in jobs]
    n = [jnp.where(strict[j], g1[j][:2 * ln, :2 * ln], 0.0) for j in jobs]
    m_akrk = [jnp.concatenate([jnp.where(strict[j], g1[j][:2 * ln, 2 * ln:], 0.0),
                               jnp.where(incl[j], g1[j][2 * ln:, 2 * ln:], 0.0)], axis=0).astype(BF16)
              for j in jobs]
    m_rb = [jnp.where(incl[j], g1[j][2 * ln:, :2 * ln], 0.0).astype(BF16) for j in jobs]
    x = [eye - n[j] for j in jobs]
    pw = n
    for _ in range(int(math.log2(ln)) - 1):
        pw = [dot(pw[j], pw[j]) for j in jobs]
        x = [x[j] + dot(x[j], pw[j]) for j in jobs]
    s = [s_ref[j] for j in jobs]
    asrs = [dotg(ar[j], s[j], lane_c) for j in jobs]
    mv = [dot(m_akrk[j], vs[j]) for j in jobs]
    u = [dot(x[j], asrs[j][:2 * ln] + mv[j][:2 * ln]) for j in jobs]
    ys = [asrs[j][2 * ln:] + mv[j][2 * ln:] - dot(m_rb[j], u[j]) for j in jobs]
    for j in jobs:
        y_ref, sl = out[j]
        y_ref[:, sl] = ys[j][:ln] + ys[j][ln:]
    ds = [dotg(jnp.concatenate([vs[j], (-u[j]).astype(BF16)], axis=0),
               jnp.concatenate([ks[j], bs[j]], axis=0), row_c) for j in jobs]
    for j in jobs:
        s_ref[j] = (s[j] + ds[j]) * gt[j]


def rwkv_scan(r, kk, v, lw, b, kt, chunk=CHUNK):
    bsz, t, c = r.shape
    chunk = min(chunk, t)
    nc = t // chunk
    fwd = pl.BlockSpec((None, chunk, c), lambda bi, ci: (bi, ci, 0))
    bwd = pl.BlockSpec((None, chunk, c), lambda bi, ci: (bi, nc - 1 - ci, 0))
    fwd_d = pl.BlockSpec((None, None, chunk, c), lambda bi, ci: (0, bi, ci, 0))
    bwd_d = pl.BlockSpec((None, None, chunk, c), lambda bi, ci: (1, bi, nc - 1 - ci, 0))
    shape = jax.ShapeDtypeStruct((bsz, t, c), F32)
    return pl.pallas_call(
        functools.partial(_scan_kernel, chunk=chunk),
        grid=(bsz, nc),
        in_specs=[fwd, fwd, fwd, bwd, bwd, bwd, fwd_d, fwd_d, fwd_d, bwd_d, bwd_d, bwd_d],
        out_specs=[fwd, bwd],
        out_shape=[shape, shape],
        scratch_shapes=[pltpu.VMEM((2 * c // LANES, LANES, LANES), F32)],
        compiler_params=_params(("parallel", "arbitrary")),
        name="rwkv_scan",
    )(r, kk, v, r, kk, v, lw, b, kt, lw, b, kt)


def _rwkv_post_kernel(y0_ref, y1_ref, bonus_ref, g_ref, lng_ref, lnb_ref, o_ref):
    y = y0_ref[...] + y1_ref[...]
    mu = _head_sum_wide(y) * (1.0 / R_HEAD)
    yc = y - mu
    var = _head_sum_wide(yc * yc) * (1.0 / R_HEAD)
    yn = yc * lax.rsqrt(var + GN_EPS) * lng_ref[...] + lnb_ref[...]
    o_ref[...] = ((yn + bonus_ref[...]) * g_ref[...]).astype(o_ref.dtype)


def rwkv_post(y0, y1, bonus, g, ln_g, ln_b, tt=256):
    b, t, c = y0.shape
    tt = min(tt, t)
    row = lambda a: a.reshape(1, -1).astype(F32)
    one = pl.BlockSpec((None, tt, c), lambda bi, i: (bi, i, 0))
    const = pl.BlockSpec((1, c), lambda bi, i: (0, 0))
    return pl.pallas_call(
        _rwkv_post_kernel,
        grid=(b, t // tt),
        in_specs=[one, one, one, one, const, const],
        out_specs=one,
        out_shape=jax.ShapeDtypeStruct((b, t, c), BF16),
        compiler_params=_params(("parallel", "parallel")),
        name="rwkv_post",
    )(y0, y1, bonus, g, row(ln_g), row(ln_b))


def _merge_kernel(u_ref, ya_ref, yr_ref, wga_ref, wgr_ref, wa_ref, wr_ref, o_ref):
    u = u_ref[...]
    ga = _sigmoid(jnp.dot(u, wga_ref[...], preferred_element_type=F32))
    gr = _sigmoid(jnp.dot(u, wgr_ref[...], preferred_element_type=F32))
    a = jnp.dot(ya_ref[...], wa_ref[...], preferred_element_type=F32)
    r = jnp.dot(yr_ref[...], wr_ref[...], preferred_element_type=F32)
    o_ref[...] = (ga * a + gr * r).astype(o_ref.dtype)


def merge_branches(u, ya, yr, w_gate, wa, wr, tm=1024, tn=512):
    m, d = u.shape
    ka = ya.shape[1]
    kr = yr.shape[1]
    n = wa.shape[1]
    tm = min(tm, m)
    nj = n // tn
    return pl.pallas_call(
        _merge_kernel,
        grid=(m // tm, nj),
        in_specs=[pl.BlockSpec((tm, d), lambda i, j: (i, 0)),
                  pl.BlockSpec((tm, ka), lambda i, j: (i, 0)),
                  pl.BlockSpec((tm, kr), lambda i, j: (i, 0)),
                  pl.BlockSpec((d, tn), lambda i, j: (0, j)),
                  pl.BlockSpec((d, tn), lambda i, j: (0, j + nj)),
                  pl.BlockSpec((ka, tn), lambda i, j: (0, j)),
                  pl.BlockSpec((kr, tn), lambda i, j: (0, j))],
        out_specs=pl.BlockSpec((tm, tn), lambda i, j: (i, j)),
        out_shape=jax.ShapeDtypeStruct((m, n), BF16),
        compiler_params=_params(("parallel", "parallel")),
        name="merge_branches",
    )(u, ya, yr, w_gate, w_gate, wa, wr)


def _router_kernel(u_ref, w_ref, b_ref, idx_ref, gs_ref, rank_ref, cnt_ref, carry_ref, *, tm):
    i = pl.program_id(0)

    @pl.when(i == 0)
    def _():
        carry_ref[...] = jnp.zeros_like(carry_ref)

    logits = jnp.dot(u_ref[...], w_ref[...], precision=HI, preferred_element_type=F32)
    scores = _sigmoid(logits)
    cur = scores + b_ref[...]
    lane = lax.broadcasted_iota(jnp.int32, (tm, N_EXPERTS), 1).astype(F32)
    lane_out = lax.broadcasted_iota(jnp.int32, (tm, LANES), 1)
    picks = []
    sel_f = jnp.zeros((tm, N_EXPERTS), F32)
    idx_out = jnp.zeros((tm, LANES), F32)
    for k in range(TOP_K):
        best = jnp.max(cur, axis=-1, keepdims=True)
        ik = jnp.min(jnp.where(cur == best, lane, float(N_EXPERTS)), axis=-1, keepdims=True)
        onehot = lane == ik
        picks.append(onehot)
        sel_f = jnp.where(onehot, 1.0, sel_f)
        cur = jnp.where(onehot, -jnp.inf, cur)
        idx_out = jnp.where(lane_out == k, ik, idx_out)
    gsel = scores * sel_f
    gsel = gsel / jnp.sum(gsel, axis=-1, keepdims=True) * ROUTED_SCALE
    ri = lax.broadcasted_iota(jnp.int32, (tm, tm), 0)
    ci = lax.broadcasted_iota(jnp.int32, (tm, tm), 1)
    before = jnp.where(ri > ci, 1.0, 0.0).astype(BF16)
    rank =jnp.dot(before, sel_f.astype(BF16), preferred_element_type=F32) + carry_ref[...]
    carry_ref[...] = carry_ref[...] + jnp.sum(sel_f, axis=0, keepdims=True)
    cnt_ref[...] = carry_ref[...].astype(jnp.int32)
    gs_out = jnp.zeros((tm, LANES), F32)
    rank_out = jnp.zeros((tm, LANES), F32)
    for k in range(TOP_K):
        gk = jnp.sum(jnp.where(picks[k], gsel, 0.0), axis=-1, keepdims=True)
        rk = jnp.sum(jnp.where(picks[k], rank, 0.0), axis=-1, keepdims=True)
        gs_out = jnp.where(lane_out == k, gk, gs_out)
        rank_out = jnp.where(lane_out == k, rk, rank_out)
    idx_ref[...] = idx_out.astype(jnp.int32)
    gs_ref[...] = gs_out
    rank_ref[...] = rank_out.astype(jnp.int32)


def router(u, w_router, b_router, tm=256):
    n, d = u.shape
    tm = min(tm, n)
    tile = pl.BlockSpec((tm, LANES), lambda i: (i, 0))
    return pl.pallas_call(
        functools.partial(_router_kernel, tm=tm),
        grid=(n // tm,),
        in_specs=[pl.BlockSpec((tm, d), lambda i: (i, 0)),
                  pl.BlockSpec((d, N_EXPERTS), lambda i: (0, 0)),
                  pl.BlockSpec((1, N_EXPERTS), lambda i: (0, 0))],
        out_specs=[tile, tile, tile, pl.BlockSpec((1, N_EXPERTS), lambda i: (0, 0))],
        out_shape=[jax.ShapeDtypeStruct((n, LANES), jnp.int32),
                   jax.ShapeDtypeStruct((n, LANES), F32),
                   jax.ShapeDtypeStruct((n, LANES), jnp.int32),
                   jax.ShapeDtypeStruct((1, N_EXPERTS), jnp.int32)],
        scratch_shapes=[pltpu.VMEM((1, N_EXPERTS), F32)],
        compiler_params=_params(("arbitrary",)),
        name="router",
    )(u, w_router.astype(F32), b_router.reshape(1, -1).astype(F32))


def _dispatch_kernel(dest_ref, tail_ref, nused_ref, u_ref, h_ref, wg_ref, wu_ref, wd_ref,
                     xs_ref, hs_ref, zbuf, zsem, sem, *, td, bm):
    step = pl.program_id(0)
    n_blocks = xs_ref.shape[0] // bm

    @pl.when(step == 0)
    def _():
        zbuf[...] = jnp.zeros_like(zbuf)

        def zero_block(blk):
            return pltpu.make_async_copy(zbuf, xs_ref.at[pl.ds(blk * bm, bm)], zsem)

        def tails(e, carry):
            @pl.when(tail_ref[e] >= 0)
            def _():
                zero_block(tail_ref[e]).start()
            return carry

        def tails_wait(e, carry):
            @pl.when(tail_ref[e] >= 0)
            def _():
                zero_block(tail_ref[e]).wait()
            return carry

        def unused(blk, carry):
            zero_block(blk).start()
            return carry

        def unused_wait(blk, carry):
            zero_block(blk).wait()
            return carry

        lax.fori_loop(0, N_EXPERTS, tails, 0)
        lax.fori_loop(nused_ref[0], n_blocks, unused, 0)
        lax.fori_loop(0, N_EXPERTS, tails_wait, 0)
        lax.fori_loop(nused_ref[0], n_blocks, unused_wait, 0)

    base = step * (td * TOP_K)

    for i in range(td):
        src = u_ref.at[pl.ds(i, 1)]
        for k in range(TOP_K):
            pltpu.make_async_copy(src, xs_ref.at[pl.ds(dest_ref[base + i * TOP_K + k], 1)],
                                  sem).start(priority=k % 2)

    x = u_ref[...].astype(BF16)
    hg = jnp.dot(x, wg_ref[...], preferred_element_type=F32)
    hu = jnp.dot(x, wu_ref[...], preferred_element_type=F32)
    hb = (hg * _sigmoid(hg) * hu).astype(BF16)
    hs_ref[...] = h_ref[...] + jnp.dot(hb, wd_ref[...], preferred_element_type=F32)

    tile = xs_ref.at[pl.ds(0, td * TOP_K)]
    pltpu.make_async_copy(tile, tile, sem).wait()


def dispatch_and_shared(u, h, dest, tail_blk, n_used, cap, bm, wg, wu, wd, td=128):
    n, d = u.shape
    f = wg.shape[1]
    td = min(td, n)
    tile = pl.BlockSpec((td, d), lambda i, *_: (i, 0))
    grid_spec = pltpu.PrefetchScalarGridSpec(
        num_scalar_prefetch=3,
        grid=(n // td,),
        in_specs=[tile, tile,
                  pl.BlockSpec((d, f), lambda i, *_: (0, 0)),
                  pl.BlockSpec((d, f), lambda i, *_: (0, 0)),
                  pl.BlockSpec((f, d), lambda i, *_: (0, 0))],
        out_specs=[pl.BlockSpec(memory_space=pl.ANY), tile],
        scratch_shapes=[pltpu.VMEM((bm, d), u.dtype), pltpu.SemaphoreType.DMA(()),
                        pltpu.SemaphoreType.DMA(())],
    )
    return pl.pallas_call(
        functools.partial(_dispatch_kernel, td=td, bm=bm),
        grid_spec=grid_spec,
        out_shape=[jax.ShapeDtypeStruct((cap, d), u.dtype), jax.ShapeDtypeStruct((n, d), F32)],
        compiler_params=_params(("arbitrary",)),
        name="moe_dispatch",
    )(dest.reshape(-1), tail_blk, n_used, u, h, wg, wu, wd)


def _expert_kernel(be_ref, nused_ref, x_ref, wg_ref, wu_ref, wd_ref, y_ref, wg_bf, wu_bf, wd_bf):
    i = pl.program_id(0)

    @pl.when(jnp.logical_or(i == 0, be_ref[i] != be_ref[jnp.maximum(i - 1, 0)]))
    def _():
        wg_bf[...] = wg_ref[...].astype(BF16)
        wu_bf[...] = wu_ref[...].astype(BF16)
        wd_bf[...] = wd_ref[...].astype(BF16)

    @pl.when(i < nused_ref[0])
    def _():
        x = x_ref[...].astype(BF16)
        hg = jnp.dot(x, wg_bf[...], preferred_element_type=F32)
        hu = jnp.dot(x, wu_bf[...], preferred_element_type=F32)
        hb = (hg * _sigmoid(hg) * hu).astype(BF16)
        y_ref[...] = jnp.dot(hb, wd_bf[...], preferred_element_type=F32)

    @pl.when(i >= nused_ref[0])
    def _():
        y_ref[...] = jnp.zeros_like(y_ref)


def expert_ffn(xs, block_e, n_used, wg, wu, wd, bm):
    cap, d = xs.shape
    f = wg.shape[2]
    row = lambda i, be, nu: (jnp.maximum(jnp.minimum(i, nu[0] - 1), 0), 0)
    grid_spec = pltpu.PrefetchScalarGridSpec(
        num_scalar_prefetch=2,
        grid=(cap // bm,),
        in_specs=[pl.BlockSpec((bm, d), row),
                  pl.BlockSpec((None, d, f), lambda i, be, nu: (be[i], 0, 0)),
                  pl.BlockSpec((None, d, f), lambda i, be, nu: (be[i], 0, 0)),
                  pl.BlockSpec((None, f, d), lambda i, be, nu: (be[i], 0, 0))],
        out_specs=pl.BlockSpec((bm, d), lambda i, be, nu: (i, 0)),
        scratch_shapes=[pltpu.VMEM((d, f), BF16), pltpu.VMEM((d, f), BF16), pltpu.VMEM((f, d), BF16)],
    )
    return pl.pallas_call(
        _expert_kernel,
        grid_spec=grid_spec,
        out_shape=jax.ShapeDtypeStruct((cap, d), F32),
        compiler_params=_params(("arbitrary",)),
        name="moe_experts",
    )(block_e, n_used, xs, wg, wu, wd)


def _combine_kernel(dest_ref, gs_ref, hs_ref, g_ref, y_ref, h_out, u_out, buf_a, buf_b, sems, *, tt):
    step = pl.program_id(0)
    last = pl.num_programs(0) - 1

    def row_copy(tile, r, k, buf, sem):
        return pltpu.make_async_copy(y_ref.at[pl.ds(dest_ref[(tile * tt + r) * TOP_K + k], 1)],
                                     buf.at[k, pl.ds(r, 1)], sem)

    def wait_rows(buf, sem):
        pltpu.make_async_copy(buf, buf, sem).wait()

    @pl.when(step == 0)
    def _():
        def row(r, carry):
            for k in range(TOP_K):
                row_copy(0, r, k, buf_a, sems.at[0]).start()
            return carry

        lax.fori_loop(0, tt, row, 0)

    def phase(cur, cur_sem, nxt, nxt_sem):
        wait_rows(cur, cur_sem)
        nxt_tile = jnp.minimum(step + 1, last)
        for r in range(tt):
            for k in range(TOP_K):
                row_copy(nxt_tile, r, k, nxt, nxt_sem).start(priority=k % 2)
        gs = gs_ref[...]
        h = hs_ref[...]
        for k in range(TOP_K):
            h = h + gs[:, k:k + 1] * cur[k]
        h_out[...] = h
        ms = jnp.mean(h * h, axis=-1, keepdims=True)
        u_out[...] = (h * lax.rsqrt(ms + RMS_EPS) * g_ref[...]).astype(u_out.dtype)

        @pl.when(step == last)
        def _():
            wait_rows(nxt, nxt_sem)

    @pl.when(step % 2 == 0)
    def _():
        phase(buf_a, sems.at[0], buf_b, sems.at[1])

    @pl.when(step % 2 == 1)
    def _():
        phase(buf_b, sems.at[1], buf_a, sems.at[0])


def combine(y, dest, gsel, hs, g_next, tt=64):
    n, d = hs.shape
    tt = min(tt, n)
    tile = pl.BlockSpec((tt, d), lambda i, s: (i, 0))
    grid_spec = pltpu.PrefetchScalarGridSpec(
        num_scalar_prefetch=1,
        grid=(n // tt,),
        in_specs=[pl.BlockSpec((tt, LANES), lambda i, s: (i, 0)),
                  tile,
                  pl.BlockSpec((1, d), lambda i, s: (0, 0)),
                  pl.BlockSpec(memory_space=pl.ANY)],
        out_specs=[tile, tile],
        scratch_shapes=[pltpu.VMEM((TOP_K, tt, d), F32), pltpu.VMEM((TOP_K, tt, d), F32),
                        pltpu.SemaphoreType.DMA((2,))],
    )
    return pl.pallas_call(
        functools.partial(_combine_kernel, tt=tt),
        grid_spec=grid_spec,
        out_shape=[jax.ShapeDtypeStruct((n, d), F32), jax.ShapeDtypeStruct((n, d), BF16)],
        compiler_params=_params(("arbitrary",)),
        name="moe_combine",
    )(dest.reshape(-1), gsel, hs, g_next.reshape(1, -1).astype(F32), y)


def moe_ffn(h, g_norm, w_router, b_router, w_e_gate, w_e_up, w_e_down, w_s_gate, w_s_up, w_s_down,
            g_next, bm=EXPERT_BLOCK):
    n, d = h.shape
    u = rmsnorm(h, g_norm, F32)
    idx128, gs128, rank128, counts = router(u, w_router, b_router)
    idx = idx128[:, :TOP_K]
    rank = rank128[:, :TOP_K]
    counts = counts.reshape(-1)
    n_blk = (counts + bm - 1) // bm
    blk_end = jnp.cumsum(n_blk)
    pstart = (blk_end - n_blk) * bm
    experts = jnp.arange(N_EXPERTS, dtype=jnp.int32)
    dest = rank + jnp.sum(jnp.where(idx[:, :, None] == experts, pstart, 0), axis=-1)
    dest = dest.astype(jnp.int32)
    cap = (n * TOP_K // bm + N_EXPERTS) * bm
    blocks = jnp.arange(cap // bm, dtype=jnp.int32)
    block_e = jnp.minimum(jnp.sum(blk_end[None, :] <= blocks[:, None], axis=1), N_EXPERTS - 1)
    block_e = block_e.astype(jnp.int32)
    n_used = blk_end[-1:].astype(jnp.int32)
    tail_blk = jnp.where(n_blk > 0, blk_end - 1, -1).astype(jnp.int32)
    xs, hs = dispatch_and_shared(u, h, dest, tail_blk, n_used, cap, bm, w_s_gate.astype(BF16),
                                 w_s_up.astype(BF16), w_s_down.astype(BF16))
    y = expert_ffn(xs, block_e, n_used, w_e_gate.astype(F32), w_e_up.astype(F32), w_e_down.astype(F32), bm)
    return combine(y, dest, gs128, hs, g_next)


def _ple_kernel(u_ref, p_ref, h_ref, wg_ref, wp_ref, gf_ref, o_ref):
    gate = _sigmoid(jnp.dot(u_ref[...], wg_ref[...], preferred_element_type=F32))
    proj = jnp.dot(p_ref[...].astype(BF16), wp_ref[...], preferred_element_type=F32)
    h = h_ref[...] + gate * proj
    ms = jnp.mean(h * h, axis=-1, keepdims=True)
    o_ref[...] = h * lax.rsqrt(ms + RMS_EPS) * gf_ref[...]


def ple_final(u, p, h, w_gate, w_proj, g_final, tm=256):
    n, d = h.shape
    pd = p.shape[1]
    tm = min(tm, n)
    tile = pl.BlockSpec((tm, d), lambda i: (i, 0))
    return pl.pallas_call(
        _ple_kernel,
        grid=(n // tm,),
        in_specs=[tile, pl.BlockSpec((tm, pd), lambda i: (i, 0)), tile,
                  pl.BlockSpec((d, d), lambda i: (0, 0)),
                  pl.BlockSpec((pd, d), lambda i: (0, 0)),
                  pl.BlockSpec((1, d), lambda i: (0, 0))],
        out_specs=tile,
        out_shape=jax.ShapeDtypeStruct((n, d), F32),
        compiler_params=_params(("parallel",)),
        name="ple_final",
    )(u, p, h, w_gate, w_proj, g_final.reshape(1, -1).astype(F32))


def token_mixing(x2, bsz, seq, norm_g, w_in, w_branch_gate, lq1, lk1, lq2, lk2, subln_g, shift_prev,
                 shift_next, w0, w_decay_up, a0, w_iclr_up, w_gate_up, k_k, k_a, r_k, ln_x_g, ln_x_b,
                 w_br_attn, w_br_rwkv, w_out, lambda_init):
    d = x2.shape[1]
    u = rmsnorm(x2, norm_g, BF16)
    w_attn = w_in[:, :A_COLS].astype(BF16)
    w_rwkv = jnp.pad(w_in[:, A_COLS:], ((0, 0), (0, R_COLS_PAD - R_COLS))).astype(BF16)
    za = matmul(u, w_attn, out_dtype=BF16, epilogue=functools.partial(_scale_q_epilogue, tn=512),
                tm=2048, tn=512, name="proj_attn")
    zr = matmul(u, w_rwkv, out_dtype=F32, tm=2048, tn=512, name="proj_rwkv")
    ya = diff_attention(za.reshape(bsz, seq, A_COLS), lq1, lk1, lq2, lk2, subln_g, lambda_init)
    r, kk, v, lw, b, kt, bonus, g = rwkv_prep(
        zr.reshape(bsz, seq, R_COLS_PAD), shift_prev, shift_next, w0, w_decay_up, a0, w_iclr_up,
        w_gate_up, k_k, k_a, r_k)
    y0, y1 = rwkv_scan(r, kk, v, lw, b, kt)
    yr = rwkv_post(y0, y1, bonus, g, ln_x_g, ln_x_b)
    merged = merge_branches(u, ya.reshape(-1, A_WIDTH), yr.reshape(-1, R_WIDTH),
                            w_branch_gate.astype(BF16), w_br_attn.astype(BF16), w_br_rwkv.astype(BF16))
    return matmul(merged, w_out.astype(BF16), out_dtype=F32, epilogue=_residual_add, extras=(x2,),
                  tm=2048, tn=512, name="out_proj")


def kernel(x, p, norm_mix_g, w_in, w_branch_gate, lambda_q1, lambda_k1, lambda_q2, lambda_k2, subln_g, shift_prev, shift_next, w0, w_decay_up, a0, w_iclr_up, w_gate_up, k_k, k_a, r_k, ln_x_g, ln_x_b, w_br_attn, w_br_rwkv, w_out, norm_ffn_g, w_router, b_router, w_e_gate, w_e_up, w_e_down, w_s_gate, w_s_up, w_s_down, norm_ple_g, w_ple_gate, w_ple_proj, norm_final_g):
    bsz, seq, d = x.shape
    depth = w_in.shape[0]
    assert depth == 1, "the final norm is fused into the last layer's embedding kernel"
    h = x.reshape(bsz * seq, d)
    i = 0
    lambda_init = 0.8 - 0.6 * math.exp(-0.3 * i)
    h = token_mixing(h, bsz, seq, norm_mix_g[i], w_in[i], w_branch_gate[i], lambda_q1[i], lambda_k1[i],
                     lambda_q2[i], lambda_k2[i], subln_g[i], shift_prev[i], shift_next[i], w0[i],
                     w_decay_up[i], a0[i], w_iclr_up[i], w_gate_up[i], k_k[i], k_a[i], r_k[i],
                     ln_x_g[i], ln_x_b[i], w_br_attn[i], w_br_rwkv[i], w_out[i], lambda_init)
    h, u = moe_ffn(h, norm_ffn_g[i], w_router[i], b_router[i], w_e_gate[i], w_e_up[i], w_e_down[i],
                   w_s_gate[i], w_s_up[i], w_s_down[i], norm_ple_g[i])
    out = ple_final(u, p[i].reshape(bsz * seq, -1), h, w_ple_gate[i].astype(BF16),
                    w_ple_proj[i].astype(BF16), norm_final_g)
    return out.reshape(bsz, seq, d)
```

```python
import functools
import math

import jax
import jax.numpy as jnp
from jax import lax
from jax.experimental import pallas as pl
from jax.experimental.pallas import tpu as pltpu

F32 = jnp.float32
BF16 = jnp.bfloat16
HI = lax.Precision.HIGHEST

D_MODEL = 2048
PLE_DIM = 256
A_HEADS = 8
A_QK_DIM = 64
A_V_DIM = 2 * A_QK_DIM
A_WIDTH = A_HEADS * A_V_DIM
R_HEADS = 16
R_HEAD = 64
R_WIDTH = R_HEADS * R_HEAD
DECAY_LORA = 64
ICLR_LORA = 64
GATE_LORA = 160
N_EXPERTS = 64
TOP_K = 8
EXPERT_FF = 512
SHARED_FF = 512
ROUTED_SCALE = 2.5
RMS_EPS = 1e-6
GN_EPS = 64e-5
LOG2E = math.log2(math.e)
Q_COLS = A_HEADS * 2 * A_QK_DIM
A_COLS = 2 * Q_COLS + A_WIDTH
R_COLS = 3 * R_WIDTH + 2 * DECAY_LORA + 2 * ICLR_LORA + GATE_LORA
R_COLS_PAD = 3584
LANES = 128
CHUNK = 64
EXPERT_BLOCK = 256
VMEM_LIMIT = 56 * 1024 * 1024


def _params(sem):
    return pltpu.CompilerParams(dimension_semantics=sem, vmem_limit_bytes=VMEM_LIMIT)


def _sigmoid(x):
    return 1.0 / (1.0 + jnp.exp(-x))


def _rmsnorm_kernel(x_ref, g_ref, o_ref):
    x = x_ref[...].astype(F32)
    ms = jnp.mean(x * x, axis=-1, keepdims=True)
    o_ref[...] = (x * lax.rsqrt(ms + RMS_EPS) * g_ref[...]).astype(o_ref.dtype)


def rmsnorm(x, g, out_dtype, tm=512):
    m, d = x.shape
    tm = min(tm, m)
    return pl.pallas_call(
        _rmsnorm_kernel,
        grid=(m // tm,),
        in_specs=[pl.BlockSpec((tm, d), lambda i: (i, 0)),
                  pl.BlockSpec((1, d), lambda i: (0, 0))],
        out_specs=pl.BlockSpec((tm, d), lambda i: (i, 0)),
        out_shape=jax.ShapeDtypeStruct((m, d), out_dtype),
        compiler_params=_params(("parallel",)),
        name="rmsnorm",
    )(x, g.reshape(1, d).astype(F32))


def _mm_kernel(x_ref, w_ref, *rest, epilogue):
    o_ref = rest[-1]
    acc = jnp.dot(x_ref[...], w_ref[...], preferred_element_type=F32)
    if epilogue is not None:
        acc = epilogue(acc, *[e[...] for e in rest[:-1]])
    o_ref[...] = acc.astype(o_ref.dtype)


def matmul(x, w, *, out_dtype, epilogue=None, extras=(), tm=512, tn=512, name="matmul"):
    m, k = x.shape
    n = w.shape[1]
    tm = min(tm, m)
    tn = min(tn, n)
    in_specs = [pl.BlockSpec((tm, k), lambda i, j: (i, 0)),
                pl.BlockSpec((k, tn), lambda i, j: (0, j))]
    in_specs += [pl.BlockSpec((tm, tn), lambda i, j: (i, j)) for _ in extras]
    return pl.pallas_call(
        functools.partial(_mm_kernel, epilogue=epilogue),
        grid=(m // tm, pl.cdiv(n, tn)),
        in_specs=in_specs,
        out_specs=pl.BlockSpec((tm, tn), lambda i, j: (i, j)),
        out_shape=jax.ShapeDtypeStruct((m, n), out_dtype),
        compiler_params=_params(("parallel", "parallel")),
        name=name,
    )(x, w, *extras)


def _residual_add(acc, res):
    return res + acc


def _scale_q_epilogue(acc, *, tn):
    is_q = pl.program_id(1) < Q_COLS // tn
    return acc * jnp.where(is_q, LOG2E * A_QK_DIM ** -0.5, 1.0)


def _attn_kernel(slopes_ref, q_ref, k_ref, v_ref, lq1_ref, lk1_ref, lq2_ref, lk2_ref, sg_ref,
                 o_ref, bias_ref, s_ref, *, tq, seq, kc, lambda_init):
    h = pl.program_id(0)
    qi = pl.program_id(1)

    @pl.when(pl.program_id(2) == 0)
    def _():
        qpos = qi * tq + lax.broadcasted_iota(jnp.int32, (tq, seq), 0)
        kpos = lax.broadcasted_iota(jnp.int32, (tq, seq), 1)
        bias_ref[...] = (-LOG2E * slopes_ref[h]) * jnp.abs(qpos - kpos).astype(F32)

    lane = lax.broadcasted_iota(jnp.int32, (tq, LANES), 1)
    q = q_ref[...]
    zero = jnp.zeros_like(q)
    qm = [jnp.where(lane < A_QK_DIM, q, zero), jnp.where(lane < A_QK_DIM, zero, q)]
    chunks = [slice(c * kc, (c + 1) * kc) for c in range(seq // kc)]
    dims = (((1,), (1,)), ((), ()))

    def scores(m, row_max):
        for sl in chunks:
            sc = lax.dot_general(qm[m], k_ref[sl, :], dims, preferred_element_type=F32) + bias_ref[:, sl]
            s_ref[m, :, sl] = sc
            cmax = jnp.max(sc, axis=-1, keepdims=True)
            row_max = cmax if row_max is None else jnp.maximum(row_max, cmax)
        return row_max

    def weighted_values(m, row_max):
        l = jnp.zeros((tq, 1), F32)
        acc = jnp.zeros((tq, LANES), F32)
        for sl in chunks:
            e = jnp.exp2(s_ref[m, :, sl] - row_max)
            l = l + jnp.sum(e, axis=-1, keepdims=True)
            acc = acc + jnp.dot(e.astype(BF16), v_ref[sl, :], preferred_element_type=F32)
        return acc / l

    max0 = scores(0, None)
    max1 = scores(1, None)
    o0 = weighted_values(0, max0)
    o1 = weighted_values(1, max1)
    lam = (jnp.exp(jnp.sum(lq1_ref[...] * lk1_ref[...], axis=-1, keepdims=True))
           - jnp.exp(jnp.sum(lq2_ref[...] * lk2_ref[...], axis=-1, keepdims=True))
           + lambda_init)
    o = o0 - lam * o1
    ms = jnp.mean(o * o, axis=-1, keepdims=True)
    o = o * lax.rsqrt(ms + RMS_EPS) * sg_ref[...]
    o_ref[...] = (o * (1.0 - lambda_init)).astype(o_ref.dtype)


def diff_attention(za, lq1, lk1, lq2, lk2, subln_g, lambda_init, tq=256):
    b, t, _ = za.shape
    tq = min(tq, t)
    h = jnp.arange(1, A_HEADS + 1, dtype=F32)
    slopes = jnp.exp2(-8.0 * h / A_HEADS)
    vec = lambda a: a.reshape(1, -1).astype(F32)
    small = lambda n: pl.BlockSpec((1, n), lambda hi, qi, bi, s: (0, 0))
    grid_spec = pltpu.PrefetchScalarGridSpec(
        num_scalar_prefetch=1,
        grid=(A_HEADS, t // tq, b),
        in_specs=[
            pl.BlockSpec((None, tq, LANES), lambda hi, qi, bi, s: (bi, qi, hi)),
            pl.BlockSpec((None, t, LANES), lambda hi, qi, bi, s: (bi, 0, A_HEADS + hi)),
            pl.BlockSpec((None, t, LANES), lambda hi, qi, bi, s: (bi, 0, 2 * A_HEADS + hi)),
            small(A_QK_DIM), small(A_QK_DIM), small(A_QK_DIM), small(A_QK_DIM), small(A_V_DIM),
        ],
        out_specs=pl.BlockSpec((None, tq, LANES), lambda hi, qi, bi, s: (bi, qi, hi)),
        scratch_shapes=[pltpu.VMEM((tq, t), F32), pltpu.VMEM((2, tq, t), F32)],
    )
    return pl.pallas_call(
        functools.partial(_attn_kernel, tq=tq, seq=t, kc=min(512, t), lambda_init=lambda_init),
        grid_spec=grid_spec,
        out_shape=jax.ShapeDtypeStruct((b, t, A_WIDTH), BF16),
        compiler_params=_params(("parallel", "parallel", "arbitrary")),
        name="diff_attention",
    )(slopes, za, za, za, vec(lq1), vec(lk1), vec(lq2), vec(lk2), vec(subln_g))


def _head_sum(x):
    ri = lax.broadcasted_iota(jnp.int32, (LANES, LANES), 0) // R_HEAD
    ci = lax.broadcasted_iota(jnp.int32, (LANES, LANES), 1) // R_HEAD
    ones = (ri == ci).astype(F32)
    return jnp.dot(x, ones, precision=HI, preferred_element_type=F32)


def _head_sum_wide(x):
    return jnp.concatenate(
        [_head_sum(x[:, p * LANES:(p + 1) * LANES]) for p in range(x.shape[1] // LANES)], axis=1)


def _rwkv_prep_kernel(z_ref, zprev_ref, znext_ref, mup_ref, mun_ref, w0_ref, wdu_ref, a0_ref,
                      wiu_ref, wgu_ref, kk_ref, ka_ref, rk_ref,
                      r_out, kkn_out, v_out, lw_out, b_out, kt_out, bonus_out, g_out, *, tt):
    i = pl.program_id(1)
    n_i = pl.num_programs(1)
    c = R_WIDTH
    z = z_ref[...]
    row = lax.broadcasted_iota(jnp.int32, (tt, 1), 0)
    prev_row = jnp.where(i > 0, zprev_ref[7:8, :], 0.0)
    next_row = jnp.where(i < n_i - 1, znext_ref[0:1, :], 0.0)
    zp = jnp.where(row == 0, prev_row, pltpu.roll(z, 1, 0))
    zn = jnp.where(row == tt - 1, next_row, pltpu.roll(z, tt - 1, 0))
    zs = z + mup_ref[...] * (zp - z) + mun_ref[...] * (zn - z)

    r = zs[:, :c]
    k = zs[:, c:2 * c]
    v = zs[:, 2 * c:3 * c]
    o3 = 3 * c
    lane = lax.broadcasted_iota(jnp.int32, (tt, LANES), 1)
    first = lane < DECAY_LORA
    lw = jnp.tanh(zs[:, o3:o3 + LANES])
    la = zs[:, o3 + LANES:o3 + 2 * LANES]
    lg = _sigmoid(zs[:, o3 + 2 * LANES:o3 + 4 * LANES])
    g_out[...] = jnp.dot(lg.astype(BF16), wgu_ref[...], preferred_element_type=F32)

    kk = k * kk_ref[...]
    ss = _head_sum_wide(kk * kk)
    kk = kk * lax.rsqrt(jnp.maximum(ss, 1e-12))
    r_out[...] = r
    kkn_out[...] = kk
    v_out[...] = v

    kt_sum = jnp.zeros_like(k)
    for d in range(2):
        keep = first if d == 0 else jnp.logical_not(first)
        lw_d = jnp.where(keep, lw, 0.0).astype(BF16)
        la_d = jnp.where(keep, la, 0.0).astype(BF16)
        zw = jnp.dot(lw_d, wdu_ref[...], preferred_element_type=F32) + w0_ref[d:d + 1, :]
        lw_out[d] = -math.exp(-0.5) * _sigmoid(zw)
        a = _sigmoid(jnp.dot(la_d, wiu_ref[...], preferred_element_type=F32) + a0_ref[d:d + 1, :])
        kt = k * (1.0 + (a - 1.0) * ka_ref[...])
        kt_out[d] = kt
        b_out[d] = kk * a
        kt_sum = kt_sum + kt
    coef = _head_sum_wide(r * kt_sum * rk_ref[...])
    bonus_out[...] = coef * v


def rwkv_prep(zr, shift_prev, shift_next, w0, w_decay_up, a0, w_iclr_up, w_gate_up, k_k, k_a, r_k,
              tt=128):
    b, t, cp = zr.shape
    tt = min(tt, t)
    c = R_WIDTH
    pad = cp - R_COLS
    row = lambda a: a.reshape(1, -1).astype(F32)
    mup = jnp.pad(row(shift_prev), ((0, 0), (0, pad)))
    mun = jnp.pad(row(shift_next), ((0, 0), (0, pad)))
    wdu = w_decay_up.reshape(2 * DECAY_LORA, c).astype(BF16)
    wiu = w_iclr_up.reshape(2 * ICLR_LORA, c).astype(BF16)
    wgu = jnp.pad(w_gate_up, ((0, 2 * LANES - GATE_LORA), (0, 0))).astype(BF16)
    nb8 = t // 8
    const = lambda shape: pl.BlockSpec(shape, lambda bi, i: (0,) * len(shape))
    in_specs = [
        pl.BlockSpec((None, tt, cp), lambda bi, i: (bi, i, 0)),
        pl.BlockSpec((None, 8, cp), lambda bi, i: (bi, jnp.maximum(i * (tt // 8) - 1, 0), 0)),
        pl.BlockSpec((None, 8, cp), lambda bi, i: (bi, jnp.minimum((i + 1) * (tt // 8), nb8 - 1), 0)),
        const((1, cp)), const((1, cp)), const((2, c)), const((2 * DECAY_LORA, c)), const((2, c)),
        const((2 * ICLR_LORA, c)), const((2 * LANES, c)), const((1, c)), const((1, c)), const((1, c)),
    ]
    one = pl.BlockSpec((None, tt, c), lambda bi, i: (bi, i, 0))
    two = pl.BlockSpec((2, None, tt, c), lambda bi, i: (0, bi, i, 0))
    s1 = jax.ShapeDtypeStruct((b, t, c), F32)
    s2 = jax.ShapeDtypeStruct((2, b, t, c), F32)
    return pl.pallas_call(
        functools.partial(_rwkv_prep_kernel, tt=tt),
        grid=(b, t // tt),
        in_specs=in_specs,
        out_specs=[one, one, one, two, two, two, one, one],
        out_shape=[s1, s1, s1, s2, s2, s2, s1, s1],
        compiler_params=_params(("parallel", "parallel")),
        name="rwkv_prep",
    )(zr, zr, zr, mup, mun, w0.astype(F32), wdu, a0.astype(F32), wiu, wgu, row(k_k), row(k_a), row(r_k))


def _scan_kernel(r0_ref, kk0_ref, v0_ref, r1_ref, kk1_ref, v1_ref, lw0_ref, b0_ref, kt0_ref,
                 lw1_ref, b1_ref, kt1_ref, y0_ref, y1_ref, s_ref, *, chunk):
    ln = chunk

    @pl.when(pl.program_id(1) == 0)
    def _():
        s_ref[...] = jnp.zeros_like(s_ref)

    ri = lax.broadcasted_iota(jnp.int32, (ln, ln), 0)
    ci = lax.broadcasted_iota(jnp.int32, (ln, ln), 1)
    r2 = lax.broadcasted_iota(jnp.int32, (2 * ln, 2 * ln), 0)
    c2 = lax.broadcasted_iota(jnp.int32, (2 * ln, 2 * ln), 1)
    eye = (r2 == c2).astype(F32)
    head0 = lax.broadcasted_iota(jnp.int32, (ln, LANES), 1) < R_HEAD
    lane_c = (((1,), (1,)), ((), ()))
    row_c = (((0,), (0,)), ((), ()))
    n_pairs = r0_ref.shape[1] // LANES

    def dot(a, b):
        return jnp.dot(a.astype(BF16), b.astype(BF16), preferred_element_type=F32)

    def dotg(a, b, dims):
        return lax.dot_general(a.astype(BF16), b.astype(BF16), dims, preferred_element_type=F32)

    def stack(x, sl):
        xs = x[:, sl]
        return jnp.concatenate([jnp.where(head0, xs, 0.0), jnp.where(head0, 0.0, xs)], axis=0)

    ar, bs, ks, vs, gt, strict, incl, out = [], [], [], [], [], [], [], []
    for d, (r_ref, kk_ref, v_ref, lw_ref, b_ref, kt_ref, y_ref) in enumerate((
            (r0_ref, kk0_ref, v0_ref, lw0_ref, b0_ref, kt0_ref, y0_ref),
            (r1_ref, kk1_ref, v1_ref, lw1_ref, b1_ref, kt1_ref, y1_ref))):
        earlier = (ri >= ci) if d == 0 else (ri <= ci)
        lw = lw_ref[...]
        cum = jnp.dot(earlier.astype(F32), lw, precision=HI, preferred_element_type=F32)
        g_inv = jnp.exp(-cum)
        rh = r_ref[...] * jnp.exp(cum)
        ah = kk_ref[...] * jnp.exp(cum - lw)
        bh = b_ref[...] * g_inv
        kh = kt_ref[...] * g_inv
        vv = v_ref[...]
        g_tot = jnp.exp(jnp.sum(lw, axis=0, keepdims=True))
        for p in range(n_pairs):
            sl = slice(p * LANES, (p + 1) * LANES)
            ar.append(jnp.concatenate([stack(ah, sl), stack(rh, sl)], axis=0).astype(BF16))
            bs.append(stack(bh, sl).astype(BF16))
            ks.append(stack(kh, sl).astype(BF16))
            vs.append(stack(vv, sl).astype(BF16))
            gt.append(g_tot[:, sl])
            strict.append((r2 > c2) if d == 0 else (r2 < c2))
            incl.append((r2 >= c2) if d == 0 else (r2 <= c2))
            out.append((y_ref, sl))
    jobs = range(len(ar))
    g1 = [dotg(ar[j], jnp.concatenate([bs[j], ks[j]], axis=0), lane_c) for j in jobs]
    n = [jnp.where(strict[j], g1[j][:2 * ln, :2 * ln], 0.0) for j in jobs]
    m_akrk = [jnp.concatenate([jnp.where(strict[j], g1[j][:2 * ln, 2 * ln:], 0.0),
                               jnp.where(incl[j], g1[j][2 * ln:, 2 * ln:], 0.0)], axis=0).astype(BF16)
              for j in jobs]
    m_rb = [jnp.where(incl[j], g1[j][2 * ln:, :2 * ln], 0.0).astype(BF16) for j in jobs]
    x = [eye - n[j] for j in jobs]
    pw = n
    for _ in range(int(math.log2(ln)) - 1):
        pw = [dot(pw[j], pw[j]) for j in jobs]
        x = [x[j] + dot(x[j], pw[j]) for j in jobs]
    s = [s_ref[j] for j in jobs]
    asrs = [dotg(ar[j], s[j], lane_c) for j in jobs]
    mv = [dot(m_akrk[j], vs[j]) for j in jobs]
    u = [dot(x[j], asrs[j][:2 * ln] + mv[j][:2 * ln]) for j in jobs]
    ys = [asrs[j][2 * ln:] + mv[j][2 * ln:] - dot(m_rb[j], u[j]) for j in jobs]
    for j in jobs:
        y_ref, sl = out[j]
        y_ref[:, sl] = ys[j][:ln] + ys[j][ln:]
    ds = [dotg(jnp.concatenate([vs[j], (-u[j]).astype(BF16)], axis=0),
               jnp.concatenate([ks[j], bs[j]], axis=0), row_c) for j in jobs]
    for j in jobs:
        s_ref[j] = (s[j] + ds[j]) * gt[j]


def rwkv_scan(r, kk, v, lw, b, kt, chunk=CHUNK):
    bsz, t, c = r.shape
    chunk = min(chunk, t)
    nc = t // chunk
    fwd = pl.BlockSpec((None, chunk, c), lambda bi, ci: (bi, ci, 0))
    bwd = pl.BlockSpec((None, chunk, c), lambda bi, ci: (bi, nc - 1 - ci, 0))
    fwd_d = pl.BlockSpec((None, None, chunk, c), lambda bi, ci: (0, bi, ci, 0))
    bwd_d = pl.BlockSpec((None, None, chunk, c), lambda bi, ci: (1, bi, nc - 1 - ci, 0))
    shape = jax.ShapeDtypeStruct((bsz, t, c), F32)
    return pl.pallas_call(
        functools.partial(_scan_kernel, chunk=chunk),
        grid=(bsz, nc),
        in_specs=[fwd, fwd, fwd, bwd, bwd, bwd, fwd_d, fwd_d, fwd_d, bwd_d, bwd_d, bwd_d],
        out_specs=[fwd, bwd],
        out_shape=[shape, shape],
        scratch_shapes=[pltpu.VMEM((2 * c // LANES, LANES, LANES), F32)],
        compiler_params=_params(("parallel", "arbitrary")),
        name="rwkv_scan",
    )(r, kk, v, r, kk, v, lw, b, kt, lw, b, kt)


def _rwkv_post_kernel(y0_ref, y1_ref, bonus_ref, g_ref, lng_ref, lnb_ref, o_ref):
    y = y0_ref[...] + y1_ref[...]
    mu = _head_sum_wide(y) * (1.0 / R_HEAD)
    yc = y - mu
    var = _head_sum_wide(yc * yc) * (1.0 / R_HEAD)
    yn = yc * lax.rsqrt(var + GN_EPS) * lng_ref[...] + lnb_ref[...]
    o_ref[...] = ((yn + bonus_ref[...]) * g_ref[...]).astype(o_ref.dtype)


def rwkv_post(y0, y1, bonus, g, ln_g, ln_b, tt=256):
    b, t, c = y0.shape
    tt = min(tt, t)
    row = lambda a: a.reshape(1, -1).astype(F32)
    one = pl.BlockSpec((None, tt, c), lambda bi, i: (bi, i, 0))
    const = pl.BlockSpec((1, c), lambda bi, i: (0, 0))
    return pl.pallas_call(
        _rwkv_post_kernel,
        grid=(b, t // tt),
        in_specs=[one, one, one, one, const, const],
        out_specs=one,
        out_shape=jax.ShapeDtypeStruct((b, t, c), BF16),
        compiler_params=_params(("parallel", "parallel")),
        name="rwkv_post",
    )(y0, y1, bonus, g, row(ln_g), row(ln_b))


def _merge_kernel(u_ref, ya_ref, yr_ref, wga_ref, wgr_ref, wa_ref, wr_ref, o_ref):
    u = u_ref[...]
    ga = _sigmoid(jnp.dot(u, wga_ref[...], preferred_element_type=F32))
    gr = _sigmoid(jnp.dot(u, wgr_ref[...], preferred_element_type=F32))
    a = jnp.dot(ya_ref[...], wa_ref[...], preferred_element_type=F32)
    r = jnp.dot(yr_ref[...], wr_ref[...], preferred_element_type=F32)
    o_ref[...] = (ga * a + gr * r).astype(o_ref.dtype)


def merge_branches(u, ya, yr, w_gate, wa, wr, tm=1024, tn=512):
    m, d = u.shape
    ka = ya.shape[1]
    kr = yr.shape[1]
    n = wa.shape[1]
    tm = min(tm, m)
    nj = n // tn
    return pl.pallas_call(
        _merge_kernel,
        grid=(m // tm, nj),
        in_specs=[pl.BlockSpec((tm, d), lambda i, j: (i, 0)),
                  pl.BlockSpec((tm, ka), lambda i, j: (i, 0)),
                  pl.BlockSpec((tm, kr), lambda i, j: (i, 0)),
                  pl.BlockSpec((d, tn), lambda i, j: (0, j)),
                  pl.BlockSpec((d, tn), lambda i, j: (0, j + nj)),
                  pl.BlockSpec((ka, tn), lambda i, j: (0, j)),
                  pl.BlockSpec((kr, tn), lambda i, j: (0, j))],
        out_specs=pl.BlockSpec((tm, tn), lambda i, j: (i, j)),
        out_shape=jax.ShapeDtypeStruct((m, n), BF16),
        compiler_params=_params(("parallel", "parallel")),
        name="merge_branches",
    )(u, ya, yr, w_gate, w_gate, wa, wr)


def _router_kernel(u_ref, w_ref, b_ref, idx_ref, gs_ref, rank_ref, cnt_ref, carry_ref, *, tm):
    i = pl.program_id(0)

    @pl.when(i == 0)
    def _():
        carry_ref[...] = jnp.zeros_like(carry_ref)

    logits = jnp.dot(u_ref[...], w_ref[...], precision=HI, preferred_element_type=F32)
    scores = _sigmoid(logits)
    cur = scores + b_ref[...]
    lane = lax.broadcasted_iota(jnp.int32, (tm, N_EXPERTS), 1).astype(F32)
    lane_out = lax.broadcasted_iota(jnp.int32, (tm, LANES), 1)
    picks = []
    sel_f = jnp.zeros((tm, N_EXPERTS), F32)
    idx_out = jnp.zeros((tm, LANES), F32)
    for k in range(TOP_K):
        best = jnp.max(cur, axis=-1, keepdims=True)
        ik = jnp.min(jnp.where(cur == best, lane, float(N_EXPERTS)), axis=-1, keepdims=True)
        onehot = lane == ik
        picks.append(onehot)
        sel_f = jnp.where(onehot, 1.0, sel_f)
        cur = jnp.where(onehot, -jnp.inf, cur)
        idx_out = jnp.where(lane_out == k, ik, idx_out)
    gsel = scores * sel_f
    gsel = gsel / jnp.sum(gsel, axis=-1, keepdims=True) * ROUTED_SCALE
    ri = lax.broadcasted_iota(jnp.int32, (tm, tm), 0)
    ci = lax.broadcasted_iota(jnp.int32, (tm, tm), 1)
    before = jnp.where(ri > ci, 1.0, 0.0).astype(BF16)
    rank =jnp.dot(before, sel_f.astype(BF16), preferred_element_type=F32) + carry_ref[...]
    carry_ref[...] = carry_ref[...] + jnp.sum(sel_f, axis=0, keepdims=True)
    cnt_ref[...] = carry_ref[...].astype(jnp.int32)
    gs_out = jnp.zeros((tm, LANES), F32)
    rank_out = jnp.zeros((tm, LANES), F32)
    for k in range(TOP_K):
        gk = jnp.sum(jnp.where(picks[k], gsel, 0.0), axis=-1, keepdims=True)
        rk = jnp.sum(jnp.where(picks[k], rank, 0.0), axis=-1, keepdims=True)
        gs_out = jnp.where(lane_out == k, gk, gs_out)
        rank_out = jnp.where(lane_out == k, rk, rank_out)
    idx_ref[...] = idx_out.astype(jnp.int32)
    gs_ref[...] = gs_out
    rank_ref[...] = rank_out.astype(jnp.int32)


def router(u, w_router, b_router, tm=256):
    n, d = u.shape
    tm = min(tm, n)
    tile = pl.BlockSpec((tm, LANES), lambda i: (i, 0))
    return pl.pallas_call(
        functools.partial(_router_kernel, tm=tm),
        grid=(n // tm,),
        in_specs=[pl.BlockSpec((tm, d), lambda i: (i, 0)),
                  pl.BlockSpec((d, N_EXPERTS), lambda i: (0, 0)),
                  pl.BlockSpec((1, N_EXPERTS), lambda i: (0, 0))],
        out_specs=[tile, tile, tile, pl.BlockSpec((1, N_EXPERTS), lambda i: (0, 0))],
        out_shape=[jax.ShapeDtypeStruct((n, LANES), jnp.int32),
                   jax.ShapeDtypeStruct((n, LANES), F32),
                   jax.ShapeDtypeStruct((n, LANES), jnp.int32),
                   jax.ShapeDtypeStruct((1, N_EXPERTS), jnp.int32)],
        scratch_shapes=[pltpu.VMEM((1, N_EXPERTS), F32)],
        compiler_params=_params(("arbitrary",)),
        name="router",
    )(u, w_router.astype(F32), b_router.reshape(1, -1).astype(F32))


def _dispatch_kernel(dest_ref, tail_ref, nused_ref, u_ref, h_ref, wg_ref, wu_ref, wd_ref,
                     xs_ref, hs_ref, zbuf, zsem, sem, *, td, bm):
    step = pl.program_id(0)
    n_blocks = xs_ref.shape[0] // bm

    @pl.when(step == 0)
    def _():
        zbuf[...] = jnp.zeros_like(zbuf)

        def zero_block(blk):
            return pltpu.make_async_copy(zbuf, xs_ref.at[pl.ds(blk * bm, bm)], zsem)

        def tails(e, carry):
            @pl.when(tail_ref[e] >= 0)
            def _():
                zero_block(tail_ref[e]).start()
            return carry

        def tails_wait(e, carry):
            @pl.when(tail_ref[e] >= 0)
            def _():
                zero_block(tail_ref[e]).wait()
            return carry

        def unused(blk, carry):
            zero_block(blk).start()
            return carry

        def unused_wait(blk, carry):
            zero_block(blk).wait()
            return carry

        lax.fori_loop(0, N_EXPERTS, tails, 0)
        lax.fori_loop(nused_ref[0], n_blocks, unused, 0)
        lax.fori_loop(0, N_EXPERTS, tails_wait, 0)
        lax.fori_loop(nused_ref[0], n_blocks, unused_wait, 0)

    base = step * (td * TOP_K)

    for i in range(td):
        src = u_ref.at[pl.ds(i, 1)]
        for k in range(TOP_K):
            pltpu.make_async_copy(src, xs_ref.at[pl.ds(dest_ref[base + i * TOP_K + k], 1)],
                                  sem).start(priority=k % 2)

    x = u_ref[...].astype(BF16)
    hg = jnp.dot(x, wg_ref[...], preferred_element_type=F32)
    hu = jnp.dot(x, wu_ref[...], preferred_element_type=F32)
    hb = (hg * _sigmoid(hg) * hu).astype(BF16)
    hs_ref[...] = h_ref[...] + jnp.dot(hb, wd_ref[...], preferred_element_type=F32)

    tile = xs_ref.at[pl.ds(0, td * TOP_K)]
    pltpu.make_async_copy(tile, tile, sem).wait()


def dispatch_and_shared(u, h, dest, tail_blk, n_used, cap, bm, wg, wu, wd, td=128):
    n, d = u.shape
    f = wg.shape[1]
    td = min(td, n)
    tile = pl.BlockSpec((td, d), lambda i, *_: (i, 0))
    grid_spec = pltpu.PrefetchScalarGridSpec(
        num_scalar_prefetch=3,
        grid=(n // td,),
        in_specs=[tile, tile,
                  pl.BlockSpec((d, f), lambda i, *_: (0, 0)),
                  pl.BlockSpec((d, f), lambda i, *_: (0, 0)),
                  pl.BlockSpec((f, d), lambda i, *_: (0, 0))],
        out_specs=[pl.BlockSpec(memory_space=pl.ANY), tile],
        scratch_shapes=[pltpu.VMEM((bm, d), u.dtype), pltpu.SemaphoreType.DMA(()),
                        pltpu.SemaphoreType.DMA(())],
    )
    return pl.pallas_call(
        functools.partial(_dispatch_kernel, td=td, bm=bm),
        grid_spec=grid_spec,
        out_shape=[jax.ShapeDtypeStruct((cap, d), u.dtype), jax.ShapeDtypeStruct((n, d), F32)],
        compiler_params=_params(("arbitrary",)),
        name="moe_dispatch",
    )(dest.reshape(-1), tail_blk, n_used, u, h, wg, wu, wd)


def _expert_kernel(be_ref, nused_ref, x_ref, wg_ref, wu_ref, wd_ref, y_ref, wg_bf, wu_bf, wd_bf):
    i = pl.program_id(0)

    @pl.when(jnp.logical_or(i == 0, be_ref[i] != be_ref[jnp.maximum(i - 1, 0)]))
    def _():
        wg_bf[...] = wg_ref[...].astype(BF16)
        wu_bf[...] = wu_ref[...].astype(BF16)
        wd_bf[...] = wd_ref[...].astype(BF16)

    @pl.when(i < nused_ref[0])
    def _():
        x = x_ref[...].astype(BF16)
        hg = jnp.dot(x, wg_bf[...], preferred_element_type=F32)
        hu = jnp.dot(x, wu_bf[...], preferred_element_type=F32)
        hb = (hg * _sigmoid(hg) * hu).astype(BF16)
        y_ref[...] = jnp.dot(hb, wd_bf[...], preferred_element_type=F32)

    @pl.when(i >= nused_ref[0])
    def _():
        y_ref[...] = jnp.zeros_like(y_ref)


def expert_ffn(xs, block_e, n_used, wg, wu, wd, bm):
    cap, d = xs.shape
    f = wg.shape[2]
    row = lambda i, be, nu: (jnp.maximum(jnp.minimum(i, nu[0] - 1), 0), 0)
    grid_spec = pltpu.PrefetchScalarGridSpec(
        num_scalar_prefetch=2,
        grid=(cap // bm,),
        in_specs=[pl.BlockSpec((bm, d), row),
                  pl.BlockSpec((None, d, f), lambda i, be, nu: (be[i], 0, 0)),
                  pl.BlockSpec((None, d, f), lambda i, be, nu: (be[i], 0, 0)),
                  pl.BlockSpec((None, f, d), lambda i, be, nu: (be[i], 0, 0))],
        out_specs=pl.BlockSpec((bm, d), lambda i, be, nu: (i, 0)),
        scratch_shapes=[pltpu.VMEM((d, f), BF16), pltpu.VMEM((d, f), BF16), pltpu.VMEM((f, d), BF16)],
    )
    return pl.pallas_call(
        _expert_kernel,
        grid_spec=grid_spec,
        out_shape=jax.ShapeDtypeStruct((cap, d), F32),
        compiler_params=_params(("arbitrary",)),
        name="moe_experts",
    )(block_e, n_used, xs, wg, wu, wd)


def _combine_kernel(dest_ref, gs_ref, hs_ref, g_ref, p_ref, wg_ref, wp_ref, gf_ref, y_ref, o_ref,
                    buf_a, buf_b, sems, *, tt):
    step = pl.program_id(0)
    last = pl.num_programs(0) - 1

    def row_copy(tile, r, k, buf, sem):
        return pltpu.make_async_copy(y_ref.at[pl.ds(dest_ref[(tile * tt + r) * TOP_K + k], 1)],
                                     buf.at[k, pl.ds(r, 1)], sem)

    def wait_rows(buf, sem):
        pltpu.make_async_copy(buf, buf, sem).wait()

    @pl.when(step == 0)
    def _():
        def row(r, carry):
            for k in range(TOP_K):
                row_copy(0, r, k, buf_a, sems.at[0]).start()
            return carry

        lax.fori_loop(0, tt, row, 0)

    def phase(cur, cur_sem, nxt, nxt_sem):
        wait_rows(cur, cur_sem)
        nxt_tile = jnp.minimum(step + 1, last)
        for r in range(tt):
            for k in range(TOP_K):
                row_copy(nxt_tile, r, k, nxt, nxt_sem).start(priority=k % 2)
        gs = gs_ref[...]
        h = hs_ref[...]
        for k in range(TOP_K):
            h = h + gs[:, k:k + 1] * cur[k]
        ms = jnp.mean(h * h, axis=-1, keepdims=True)
        u = (h * lax.rsqrt(ms + RMS_EPS) * g_ref[...]).astype(BF16)
        gate = _sigmoid(jnp.dot(u, wg_ref[...], preferred_element_type=F32))
        proj = jnp.dot(p_ref[...].astype(BF16), wp_ref[...], preferred_element_type=F32)
        h = h + gate * proj
        ms = jnp.mean(h * h, axis=-1, keepdims=True)
        o_ref[...] = h * lax.rsqrt(ms + RMS_EPS) * gf_ref[...]

        @pl.when(step == last)
        def _():
            wait_rows(nxt, nxt_sem)

    @pl.when(step % 2 == 0)
    def _():
        phase(buf_a, sems.at[0], buf_b, sems.at[1])

    @pl.when(step % 2 == 1)
    def _():
        phase(buf_b, sems.at[1], buf_a, sems.at[0])


def combine(y, dest, gsel, hs, g_next, p, w_gate, w_proj, g_final, tt=64):
    n, d = hs.shape
    pd = p.shape[1]
    tt = min(tt, n)
    tile = pl.BlockSpec((tt, d), lambda i, s: (i, 0))
    row = lambda a: a.reshape(1, -1).astype(F32)
    grid_spec = pltpu.PrefetchScalarGridSpec(
        num_scalar_prefetch=1,
        grid=(n // tt,),
        in_specs=[pl.BlockSpec((tt, LANES), lambda i, s: (i, 0)),
                  tile,
                  pl.BlockSpec((1, d), lambda i, s: (0, 0)),
                  pl.BlockSpec((tt, pd), lambda i, s: (i, 0)),
                  pl.BlockSpec((d, d), lambda i, s: (0, 0)),
                  pl.BlockSpec((pd, d), lambda i, s: (0, 0)),
                  pl.BlockSpec((1, d), lambda i, s: (0, 0)),
                  pl.BlockSpec(memory_space=pl.ANY)],
        out_specs=tile,
        scratch_shapes=[pltpu.VMEM((TOP_K, tt, d), F32), pltpu.VMEM((TOP_K, tt, d), F32),
                        pltpu.SemaphoreType.DMA((2,))],
    )
    return pl.pallas_call(
        functools.partial(_combine_kernel, tt=tt),
        grid_spec=grid_spec,
        out_shape=jax.ShapeDtypeStruct((n, d), F32),
        compiler_params=_params(("arbitrary",)),
        name="moe_combine",
    )(dest.reshape(-1), gsel, hs, row(g_next), p, w_gate, w_proj, row(g_final), y)


def moe_ffn(h, g_norm, w_router, b_router, w_e_gate, w_e_up, w_e_down, w_s_gate, w_s_up, w_s_down,
            g_next, p, w_ple_gate, w_ple_proj, g_final, bm=EXPERT_BLOCK):
    n, d = h.shape
    u = rmsnorm(h, g_norm, F32)
    idx128, gs128, rank128, counts = router(u, w_router, b_router)
    idx = idx128[:, :TOP_K]
    rank = rank128[:, :TOP_K]
    counts = counts.reshape(-1)
    n_blk = (counts + bm - 1) // bm
    blk_end = jnp.cumsum(n_blk)
    pstart = (blk_end - n_blk) * bm
    experts = jnp.arange(N_EXPERTS, dtype=jnp.int32)
    dest = rank + jnp.sum(jnp.where(idx[:, :, None] == experts, pstart, 0), axis=-1)
    dest = dest.astype(jnp.int32)
    cap = (n * TOP_K // bm + N_EXPERTS) * bm
    blocks = jnp.arange(cap // bm, dtype=jnp.int32)
    block_e = jnp.minimum(jnp.sum(blk_end[None, :] <= blocks[:, None], axis=1), N_EXPERTS - 1)
    block_e = block_e.astype(jnp.int32)
    n_used = blk_end[-1:].astype(jnp.int32)
    tail_blk = jnp.where(n_blk > 0, blk_end - 1, -1).astype(jnp.int32)
    xs, hs = dispatch_and_shared(u, h, dest, tail_blk, n_used, cap, bm, w_s_gate.astype(BF16),
                                 w_s_up.astype(BF16), w_s_down.astype(BF16))
    y = expert_ffn(xs, block_e, n_used, w_e_gate.astype(F32), w_e_up.astype(F32), w_e_down.astype(F32), bm)
    return combine(y, dest, gs128, hs, g_next, p, w_ple_gate, w_ple_proj, g_final)


def token_mixing(x2, bsz, seq, norm_g, w_in, w_branch_gate, lq1, lk1, lq2, lk2, subln_g, shift_prev,
                 shift_next, w0, w_decay_up, a0, w_iclr_up, w_gate_up, k_k, k_a, r_k, ln_x_g, ln_x_b,
                 w_br_attn, w_br_rwkv, w_out, lambda_init):
    d = x2.shape[1]
    u = rmsnorm(x2, norm_g, BF16)
    w_attn = w_in[:, :A_COLS].astype(BF16)
    w_rwkv = jnp.pad(w_in[:, A_COLS:], ((0, 0), (0, R_COLS_PAD - R_COLS))).astype(BF16)
    za = matmul(u, w_attn, out_dtype=BF16, epilogue=functools.partial(_scale_q_epilogue, tn=512),
                tm=2048, tn=512, name="proj_attn")
    zr = matmul(u, w_rwkv, out_dtype=F32, tm=2048, tn=512, name="proj_rwkv")
    ya = diff_attention(za.reshape(bsz, seq, A_COLS), lq1, lk1, lq2, lk2, subln_g, lambda_init)
    r, kk, v, lw, b, kt, bonus, g = rwkv_prep(
        zr.reshape(bsz, seq, R_COLS_PAD), shift_prev, shift_next, w0, w_decay_up, a0, w_iclr_up,
        w_gate_up, k_k, k_a, r_k)
    y0, y1 = rwkv_scan(r, kk, v, lw, b, kt)
    yr = rwkv_post(y0, y1, bonus, g, ln_x_g, ln_x_b)
    merged = merge_branches(u, ya.reshape(-1, A_WIDTH), yr.reshape(-1, R_WIDTH),
                            w_branch_gate.astype(BF16), w_br_attn.astype(BF16), w_br_rwkv.astype(BF16))
    return matmul(merged, w_out.astype(BF16), out_dtype=F32, epilogue=_residual_add, extras=(x2,),
                  tm=2048, tn=512, name="out_proj")


def kernel(x, p, norm_mix_g, w_in, w_branch_gate, lambda_q1, lambda_k1, lambda_q2, lambda_k2, subln_g, shift_prev, shift_next, w0, w_decay_up, a0, w_iclr_up, w_gate_up, k_k, k_a, r_k, ln_x_g, ln_x_b, w_br_attn, w_br_rwkv, w_out, norm_ffn_g, w_router, b_router, w_e_gate, w_e_up, w_e_down, w_s_gate, w_s_up, w_s_down, norm_ple_g, w_ple_gate, w_ple_proj, norm_final_g):
    bsz, seq, d = x.shape
    depth = w_in.shape[0]
    assert depth == 1, "the final norm is fused into the last layer's embedding kernel"
    h = x.reshape(bsz * seq, d)
    i = 0
    lambda_init = 0.8 - 0.6 * math.exp(-0.3 * i)
    h = token_mixing(h, bsz, seq, norm_mix_g[i], w_in[i], w_branch_gate[i], lambda_q1[i], lambda_k1[i],
                     lambda_q2[i], lambda_k2[i], subln_g[i], shift_prev[i], shift_next[i], w0[i],
                     w_decay_up[i], a0[i], w_iclr_up[i], w_gate_up[i], k_k[i], k_a[i], r_k[i],
                     ln_x_g[i], ln_x_b[i], w_br_attn[i], w_br_rwkv[i], w_out[i], lambda_init)
    out = moe_ffn(h, norm_ffn_g[i], w_router[i], b_router[i], w_e_gate[i], w_e_up[i], w_e_down[i],
                  w_s_gate[i], w_s_up[i], w_s_down[i], norm_ple_g[i], p[i].reshape(bsz * seq, -1),
                  w_ple_gate[i].astype(BF16), w_ple_proj[i].astype(BF16), norm_final_g)
    return out.reshape(bsz, seq, d)
```

```python
import functools
import math

import jax
import jax.numpy as jnp
from jax import lax
from jax.experimental import pallas as pl
from jax.experimental.pallas import tpu as pltpu

F32 = jnp.float32
BF16 = jnp.bfloat16
HI = lax.Precision.HIGHEST

D_MODEL = 2048
PLE_DIM = 256
A_HEADS = 8
A_QK_DIM = 64
A_V_DIM = 2 * A_QK_DIM
A_WIDTH = A_HEADS * A_V_DIM
R_HEADS = 16
R_HEAD = 64
R_WIDTH = R_HEADS * R_HEAD
DECAY_LORA = 64
ICLR_LORA = 64
GATE_LORA = 160
N_EXPERTS = 64
TOP_K = 8
EXPERT_FF = 512
SHARED_FF = 512
ROUTED_SCALE = 2.5
RMS_EPS = 1e-6
GN_EPS = 64e-5
LOG2E = math.log2(math.e)
Q_COLS = A_HEADS * 2 * A_QK_DIM
A_COLS = 2 * Q_COLS + A_WIDTH
R_COLS = 3 * R_WIDTH + 2 * DECAY_LORA + 2 * ICLR_LORA + GATE_LORA
R_COLS_PAD = 3584
LANES = 128
CHUNK = 64
EXPERT_BLOCK = 512
VMEM_LIMIT = 56 * 1024 * 1024


def _params(sem):
    return pltpu.CompilerParams(dimension_semantics=sem, vmem_limit_bytes=VMEM_LIMIT)


def _sigmoid(x):
    return 1.0 / (1.0 + jnp.exp(-x))


def _rmsnorm_kernel(x_ref, g_ref, o_ref):
    x = x_ref[...].astype(F32)
    ms = jnp.mean(x * x, axis=-1, keepdims=True)
    o_ref[...] = (x * lax.rsqrt(ms + RMS_EPS) * g_ref[...]).astype(o_ref.dtype)


def rmsnorm(x, g, out_dtype, tm=512):
    m, d = x.shape
    tm = min(tm, m)
    return pl.pallas_call(
        _rmsnorm_kernel,
        grid=(m // tm,),
        in_specs=[pl.BlockSpec((tm, d), lambda i: (i, 0)),
                  pl.BlockSpec((1, d), lambda i: (0, 0))],
        out_specs=pl.BlockSpec((tm, d), lambda i: (i, 0)),
        out_shape=jax.ShapeDtypeStruct((m, d), out_dtype),
        compiler_params=_params(("parallel",)),
        name="rmsnorm",
    )(x, g.reshape(1, d).astype(F32))


def _mm_kernel(x_ref, w_ref, *rest, epilogue):
    o_ref = rest[-1]
    acc = jnp.dot(x_ref[...], w_ref[...], preferred_element_type=F32)
    if epilogue is not None:
        acc = epilogue(acc, *[e[...] for e in rest[:-1]])
    o_ref[...] = acc.astype(o_ref.dtype)


def matmul(x, w, *, out_dtype, epilogue=None, extras=(), tm=512, tn=512, name="matmul"):
    m, k = x.shape
    n = w.shape[1]
    tm = min(tm, m)
    tn = min(tn, n)
    in_specs = [pl.BlockSpec((tm, k), lambda i, j: (i, 0)),
                pl.BlockSpec((k, tn), lambda i, j: (0, j))]
    in_specs += [pl.BlockSpec((tm, tn), lambda i, j: (i, j)) for _ in extras]
    return pl.pallas_call(
        functools.partial(_mm_kernel, epilogue=epilogue),
        grid=(m // tm, pl.cdiv(n, tn)),
        in_specs=in_specs,
        out_specs=pl.BlockSpec((tm, tn), lambda i, j: (i, j)),
        out_shape=jax.ShapeDtypeStruct((m, n), out_dtype),
        compiler_params=_params(("parallel", "parallel")),
        name=name,
    )(x, w, *extras)


def _residual_add(acc, res):
    return res + acc


def _scale_q_epilogue(acc, *, tn):
    is_q = pl.program_id(1) < Q_COLS // tn
    return acc * jnp.where(is_q, LOG2E * A_QK_DIM ** -0.5, 1.0)


def _attn_kernel(slopes_ref, q_ref, k_ref, v_ref, lq1_ref, lk1_ref, lq2_ref, lk2_ref, sg_ref,
                 o_ref, bias_ref, s_ref, *, tq, seq, kc, lambda_init):
    h = pl.program_id(0)
    qi = pl.program_id(1)

    @pl.when(pl.program_id(2) == 0)
    def _():
        qpos = qi * tq + lax.broadcasted_iota(jnp.int32, (tq, seq), 0)
        kpos = lax.broadcasted_iota(jnp.int32, (tq, seq), 1)
        bias_ref[...] = (-LOG2E * slopes_ref[h]) * jnp.abs(qpos - kpos).astype(F32)

    lane = lax.broadcasted_iota(jnp.int32, (tq, LANES), 1)
    q = q_ref[...]
    zero = jnp.zeros_like(q)
    qm = [jnp.where(lane < A_QK_DIM, q, zero), jnp.where(lane < A_QK_DIM, zero, q)]
    chunks = [slice(c * kc, (c + 1) * kc) for c in range(seq // kc)]
    dims = (((1,), (1,)), ((), ()))

    def scores(m, row_max):
        for sl in chunks:
            sc = lax.dot_general(qm[m], k_ref[sl, :], dims, preferred_element_type=F32) + bias_ref[:, sl]
            s_ref[m, :, sl] = sc
            cmax = jnp.max(sc, axis=-1, keepdims=True)
            row_max = cmax if row_max is None else jnp.maximum(row_max, cmax)
        return row_max

    def weighted_values(m, row_max):
        l = jnp.zeros((tq, 1), F32)
        acc = jnp.zeros((tq, LANES), F32)
        for sl in chunks:
            e = jnp.exp2(s_ref[m, :, sl] - row_max)
            l = l + jnp.sum(e, axis=-1, keepdims=True)
            acc = acc + jnp.dot(e.astype(BF16), v_ref[sl, :], preferred_element_type=F32)
        return acc / l

    max0 = scores(0, None)
    max1 = scores(1, None)
    o0 = weighted_values(0, max0)
    o1 = weighted_values(1, max1)
    lam = (jnp.exp(jnp.sum(lq1_ref[...] * lk1_ref[...], axis=-1, keepdims=True))
           - jnp.exp(jnp.sum(lq2_ref[...] * lk2_ref[...], axis=-1, keepdims=True))
           + lambda_init)
    o = o0 - lam * o1
    ms = jnp.mean(o * o, axis=-1, keepdims=True)
    o = o * lax.rsqrt(ms + RMS_EPS) * sg_ref[...]
    o_ref[...] = (o * (1.0 - lambda_init)).astype(o_ref.dtype)


def diff_attention(za, lq1, lk1, lq2, lk2, subln_g, lambda_init, tq=256):
    b, t, _ = za.shape
    tq = min(tq, t)
    h = jnp.arange(1, A_HEADS + 1, dtype=F32)
    slopes = jnp.exp2(-8.0 * h / A_HEADS)
    vec = lambda a: a.reshape(1, -1).astype(F32)
    small = lambda n: pl.BlockSpec((1, n), lambda hi, qi, bi, s: (0, 0))
    grid_spec = pltpu.PrefetchScalarGridSpec(
        num_scalar_prefetch=1,
        grid=(A_HEADS, t // tq, b),
        in_specs=[
            pl.BlockSpec((None, tq, LANES), lambda hi, qi, bi, s: (bi, qi, hi)),
            pl.BlockSpec((None, t, LANES), lambda hi, qi, bi, s: (bi, 0, A_HEADS + hi)),
            pl.BlockSpec((None, t, LANES), lambda hi, qi, bi, s: (bi, 0, 2 * A_HEADS + hi)),
            small(A_QK_DIM), small(A_QK_DIM), small(A_QK_DIM), small(A_QK_DIM), small(A_V_DIM),
        ],
        out_specs=pl.BlockSpec((None, tq, LANES), lambda hi, qi, bi, s: (bi, qi, hi)),
        scratch_shapes=[pltpu.VMEM((tq, t), F32), pltpu.VMEM((2, tq, t), F32)],
    )
    return pl.pallas_call(
        functools.partial(_attn_kernel, tq=tq, seq=t, kc=min(512, t), lambda_init=lambda_init),
        grid_spec=grid_spec,
        out_shape=jax.ShapeDtypeStruct((b, t, A_WIDTH), BF16),
        compiler_params=_params(("parallel", "parallel", "arbitrary")),
        name="diff_attention",
    )(slopes, za, za, za, vec(lq1), vec(lk1), vec(lq2), vec(lk2), vec(subln_g))


def _head_sum(x):
    ri = lax.broadcasted_iota(jnp.int32, (LANES, LANES), 0) // R_HEAD
    ci = lax.broadcasted_iota(jnp.int32, (LANES, LANES), 1) // R_HEAD
    ones = (ri == ci).astype(F32)
    return jnp.dot(x, ones, precision=HI, preferred_element_type=F32)


def _head_sum_wide(x):
    return jnp.concatenate(
        [_head_sum(x[:, p * LANES:(p + 1) * LANES]) for p in range(x.shape[1] // LANES)], axis=1)


def _rwkv_prep_kernel(z_ref, zprev_ref, znext_ref, mup_ref, mun_ref, w0_ref, wdu_ref, a0_ref,
                      wiu_ref, wgu_ref, kk_ref, ka_ref, rk_ref,
                      r_out, kkn_out, v_out, lw_out, b_out, kt_out, bonus_out, g_out, *, tt):
    i = pl.program_id(1)
    n_i = pl.num_programs(1)
    c = R_WIDTH
    z = z_ref[...]
    row = lax.broadcasted_iota(jnp.int32, (tt, 1), 0)
    prev_row = jnp.where(i > 0, zprev_ref[7:8, :], 0.0)
    next_row = jnp.where(i < n_i - 1, znext_ref[0:1, :], 0.0)
    zp = jnp.where(row == 0, prev_row, pltpu.roll(z, 1, 0))
    zn = jnp.where(row == tt - 1, next_row, pltpu.roll(z, tt - 1, 0))
    zs = z + mup_ref[...] * (zp - z) + mun_ref[...] * (zn - z)

    r = zs[:, :c]
    k = zs[:, c:2 * c]
    v = zs[:, 2 * c:3 * c]
    o3 = 3 * c
    lane = lax.broadcasted_iota(jnp.int32, (tt, LANES), 1)
    first = lane < DECAY_LORA
    lw = jnp.tanh(zs[:, o3:o3 + LANES])
    la = zs[:, o3 + LANES:o3 + 2 * LANES]
    lg = _sigmoid(zs[:, o3 + 2 * LANES:o3 + 4 * LANES])
    g_out[...] = jnp.dot(lg.astype(BF16), wgu_ref[...], preferred_element_type=F32)

    kk = k * kk_ref[...]
    ss = _head_sum_wide(kk * kk)
    kk = kk * lax.rsqrt(jnp.maximum(ss, 1e-12))
    r_out[...] = r
    kkn_out[...] = kk
    v_out[...] = v

    kt_sum = jnp.zeros_like(k)
    for d in range(2):
        keep = first if d == 0 else jnp.logical_not(first)
        lw_d = jnp.where(keep, lw, 0.0).astype(BF16)
        la_d = jnp.where(keep, la, 0.0).astype(BF16)
        zw = jnp.dot(lw_d, wdu_ref[...], preferred_element_type=F32) + w0_ref[d:d + 1, :]
        lw_out[d] = -math.exp(-0.5) * _sigmoid(zw)
        a = _sigmoid(jnp.dot(la_d, wiu_ref[...], preferred_element_type=F32) + a0_ref[d:d + 1, :])
        kt = k * (1.0 + (a - 1.0) * ka_ref[...])
        kt_out[d] = kt
        b_out[d] = kk * a
        kt_sum = kt_sum + kt
    coef = _head_sum_wide(r * kt_sum * rk_ref[...])
    bonus_out[...] = coef * v


def rwkv_prep(zr, shift_prev, shift_next, w0, w_decay_up, a0, w_iclr_up, w_gate_up, k_k, k_a, r_k,
              tt=128):
    b, t, cp = zr.shape
    tt = min(tt, t)
    c = R_WIDTH
    pad = cp - R_COLS
    row = lambda a: a.reshape(1, -1).astype(F32)
    mup = jnp.pad(row(shift_prev), ((0, 0), (0, pad)))
    mun = jnp.pad(row(shift_next), ((0, 0), (0, pad)))
    wdu = w_decay_up.reshape(2 * DECAY_LORA, c).astype(BF16)
    wiu = w_iclr_up.reshape(2 * ICLR_LORA, c).astype(BF16)
    wgu = jnp.pad(w_gate_up, ((0, 2 * LANES - GATE_LORA), (0, 0))).astype(BF16)
    nb8 = t // 8
    const = lambda shape: pl.BlockSpec(shape, lambda bi, i: (0,) * len(shape))
    in_specs = [
        pl.BlockSpec((None, tt, cp), lambda bi, i: (bi, i, 0)),
        pl.BlockSpec((None, 8, cp), lambda bi, i: (bi, jnp.maximum(i * (tt // 8) - 1, 0), 0)),
        pl.BlockSpec((None, 8, cp), lambda bi, i: (bi, jnp.minimum((i + 1) * (tt // 8), nb8 - 1), 0)),
        const((1, cp)), const((1, cp)), const((2, c)), const((2 * DECAY_LORA, c)), const((2, c)),
        const((2 * ICLR_LORA, c)), const((2 * LANES, c)), const((1, c)), const((1, c)), const((1, c)),
    ]
    one = pl.BlockSpec((None, tt, c), lambda bi, i: (bi, i, 0))
    two = pl.BlockSpec((2, None, tt, c), lambda bi, i: (0, bi, i, 0))
    s1 = jax.ShapeDtypeStruct((b, t, c), F32)
    s2 = jax.ShapeDtypeStruct((2, b, t, c), F32)
    return pl.pallas_call(
        functools.partial(_rwkv_prep_kernel, tt=tt),
        grid=(b, t // tt),
        in_specs=in_specs,
        out_specs=[one, one, one, two, two, two, one, one],
        out_shape=[s1, s1, s1, s2, s2, s2, s1, s1],
        compiler_params=_params(("parallel", "parallel")),
        name="rwkv_prep",
    )(zr, zr, zr, mup, mun, w0.astype(F32), wdu, a0.astype(F32), wiu, wgu, row(k_k), row(k_a), row(r_k))


def _scan_kernel(r0_ref, kk0_ref, v0_ref, r1_ref, kk1_ref, v1_ref, lw0_ref, b0_ref, kt0_ref,
                 lw1_ref, b1_ref, kt1_ref, y0_ref, y1_ref, s_ref, *, chunk):
    ln = chunk

    @pl.when(pl.program_id(1) == 0)
    def _():
        s_ref[...] = jnp.zeros_like(s_ref)

    ri = lax.broadcasted_iota(jnp.int32, (ln, ln), 0)
    ci = lax.broadcasted_iota(jnp.int32, (ln, ln), 1)
    r2 = lax.broadcasted_iota(jnp.int32, (2 * ln, 2 * ln), 0)
    c2 = lax.broadcasted_iota(jnp.int32, (2 * ln, 2 * ln), 1)
    eye = (r2 == c2).astype(F32)
    head0 = lax.broadcasted_iota(jnp.int32, (ln, LANES), 1) < R_HEAD
    lane_c = (((1,), (1,)), ((), ()))
    row_c = (((0,), (0,)), ((), ()))
    n_pairs = r0_ref.shape[1] // LANES

    def dot(a, b):
        return jnp.dot(a.astype(BF16), b.astype(BF16), preferred_element_type=F32)

    def dotg(a, b, dims):
        return lax.dot_general(a.astype(BF16), b.astype(BF16), dims, preferred_element_type=F32)

    def stack(x, sl):
        xs = x[:, sl]
        return jnp.concatenate([jnp.where(head0, xs, 0.0), jnp.where(head0, 0.0, xs)], axis=0)

    ar, bs, ks, vs, gt, strict, incl, out = [], [], [], [], [], [], [], []
    for d, (r_ref, kk_ref, v_ref, lw_ref, b_ref, kt_ref, y_ref) in enumerate((
            (r0_ref, kk0_ref, v0_ref, lw0_ref, b0_ref, kt0_ref, y0_ref),
            (r1_ref, kk1_ref, v1_ref, lw1_ref, b1_ref, kt1_ref, y1_ref))):
        earlier = (ri >= ci) if d == 0 else (ri <= ci)
        lw = lw_ref[...]
        cum = jnp.dot(earlier.astype(F32), lw, precision=HI, preferred_element_type=F32)
        g_inv = jnp.exp(-cum)
        rh = r_ref[...] * jnp.exp(cum)
        ah = kk_ref[...] * jnp.exp(cum - lw)
        bh = b_ref[...] * g_inv
        kh = kt_ref[...] * g_inv
        vv = v_ref[...]
        g_tot = jnp.exp(jnp.sum(lw, axis=0, keepdims=True))
        for p in range(n_pairs):
            sl = slice(p * LANES, (p + 1) * LANES)
            ar.append(jnp.concatenate([stack(ah, sl), stack(rh, sl)], axis=0).astype(BF16))
            bs.append(stack(bh, sl).astype(BF16))
            ks.append(stack(kh, sl).astype(BF16))
            vs.append(stack(vv, sl).astype(BF16))
            gt.append(g_tot[:, sl])
            strict.append((r2 > c2) if d == 0 else (r2 < c2))
            incl.append((r2 >= c2) if d == 0 else (r2 <= c2))
            out.append((y_ref, sl))
    jobs = range(len(ar))
    g1 = [dotg(ar[j], jnp.concatenate([bs[j], ks[j]], axis=0), lane_c) for j in jobs]
    n = [jnp.where(strict[j], g1[j][:2 * ln, :2 * ln], 0.0) for j in jobs]
    m_akrk = [jnp.concatenate([jnp.where(strict[j], g1[j][:2 * ln, 2 * ln:], 0.0),
                               jnp.where(incl[j], g1[j][2 * ln:, 2 * ln:], 0.0)], axis=0).astype(BF16)
              for j in jobs]
    m_rb = [jnp.where(incl[j], g1[j][2 * ln:, :2 * ln], 0.0).astype(BF16) for j in jobs]
    x = [eye - n[j] for j in jobs]
    pw = n
    for _ in range(int(math.log2(ln)) - 1):
        pw = [dot(pw[j], pw[j]) for j in jobs]
        x = [x[j] + dot(x[j], pw[j]) for j in jobs]
    s = [s_ref[j] for j in jobs]
    asrs = [dotg(ar[j], s[j], lane_c) for j in jobs]
    mv = [dot(m_akrk[j], vs[j]) for j in jobs]
    u = [dot(x[j], asrs[j][:2 * ln] + mv[j][:2 * ln]) for j in jobs]
    ys = [asrs[j][2 * ln:] + mv[j][2 * ln:] - dot(m_rb[j], u[j]) for j in jobs]
    for j in jobs:
        y_ref, sl = out[j]
        y_ref[:, sl] = ys[j][:ln] + ys[j][ln:]
    ds = [dotg(jnp.concatenate([vs[j], (-u[j]).astype(BF16)], axis=0),
               jnp.concatenate([ks[j], bs[j]], axis=0), row_c) for j in jobs]
    for j in jobs:
        s_ref[j] = (s[j] + ds[j]) * gt[j]


def rwkv_scan(r, kk, v, lw, b, kt, chunk=CHUNK):
    bsz, t, c = r.shape
    chunk = min(chunk, t)
    nc = t // chunk
    fwd = pl.BlockSpec((None, chunk, c), lambda bi, ci: (bi, ci, 0))
    bwd = pl.BlockSpec((None, chunk, c), lambda bi, ci: (bi, nc - 1 - ci, 0))
    fwd_d = pl.BlockSpec((None, None, chunk, c), lambda bi, ci: (0, bi, ci, 0))
    bwd_d = pl.BlockSpec((None, None, chunk, c), lambda bi, ci: (1, bi, nc - 1 - ci, 0))
    shape = jax.ShapeDtypeStruct((bsz, t, c), F32)
    return pl.pallas_call(
        functools.partial(_scan_kernel, chunk=chunk),
        grid=(bsz, nc),
        in_specs=[fwd, fwd, fwd, bwd, bwd, bwd, fwd_d, fwd_d, fwd_d, bwd_d, bwd_d, bwd_d],
        out_specs=[fwd, bwd],
        out_shape=[shape, shape],
        scratch_shapes=[pltpu.VMEM((2 * c // LANES, LANES, LANES), F32)],
        compiler_params=_params(("parallel", "arbitrary")),
        name="rwkv_scan",
    )(r, kk, v, r, kk, v, lw, b, kt, lw, b, kt)


def _rwkv_post_kernel(y0_ref, y1_ref, bonus_ref, g_ref, lng_ref, lnb_ref, o_ref):
    y = y0_ref[...] + y1_ref[...]
    mu = _head_sum_wide(y) * (1.0 / R_HEAD)
    yc = y - mu
    var = _head_sum_wide(yc * yc) * (1.0 / R_HEAD)
    yn = yc * lax.rsqrt(var + GN_EPS) * lng_ref[...] + lnb_ref[...]
    o_ref[...] = ((yn + bonus_ref[...]) * g_ref[...]).astype(o_ref.dtype)


def rwkv_post(y0, y1, bonus, g, ln_g, ln_b, tt=256):
    b, t, c = y0.shape
    tt = min(tt, t)
    row = lambda a: a.reshape(1, -1).astype(F32)
    one = pl.BlockSpec((None, tt, c), lambda bi, i: (bi, i, 0))
    const = pl.BlockSpec((1, c), lambda bi, i: (0, 0))
    return pl.pallas_call(
        _rwkv_post_kernel,
        grid=(b, t // tt),
        in_specs=[one, one, one, one, const, const],
        out_specs=one,
        out_shape=jax.ShapeDtypeStruct((b, t, c), BF16),
        compiler_params=_params(("parallel", "parallel")),
        name="rwkv_post",
    )(y0, y1, bonus, g, row(ln_g), row(ln_b))


def _merge_kernel(u_ref, ya_ref, yr_ref, wga_ref, wgr_ref, wa_ref, wr_ref, o_ref):
    u = u_ref[...]
    ga = _sigmoid(jnp.dot(u, wga_ref[...], preferred_element_type=F32))
    gr = _sigmoid(jnp.dot(u, wgr_ref[...], preferred_element_type=F32))
    a = jnp.dot(ya_ref[...], wa_ref[...], preferred_element_type=F32)
    r = jnp.dot(yr_ref[...], wr_ref[...], preferred_element_type=F32)
    o_ref[...] = (ga * a + gr * r).astype(o_ref.dtype)


def merge_branches(u, ya, yr, w_gate, wa, wr, tm=1024, tn=512):
    m, d = u.shape
    ka = ya.shape[1]
    kr = yr.shape[1]
    n = wa.shape[1]
    tm = min(tm, m)
    nj = n // tn
    return pl.pallas_call(
        _merge_kernel,
        grid=(m // tm, nj),
        in_specs=[pl.BlockSpec((tm, d), lambda i, j: (i, 0)),
                  pl.BlockSpec((tm, ka), lambda i, j: (i, 0)),
                  pl.BlockSpec((tm, kr), lambda i, j: (i, 0)),
                  pl.BlockSpec((d, tn), lambda i, j: (0, j)),
                  pl.BlockSpec((d, tn), lambda i, j: (0, j + nj)),
                  pl.BlockSpec((ka, tn), lambda i, j: (0, j)),
                  pl.BlockSpec((kr, tn), lambda i, j: (0, j))],
        out_specs=pl.BlockSpec((tm, tn), lambda i, j: (i, j)),
        out_shape=jax.ShapeDtypeStruct((m, n), BF16),
        compiler_params=_params(("parallel", "parallel")),
        name="merge_branches",
    )(u, ya, yr, w_gate, w_gate, wa, wr)


def _router_kernel(u_ref, w_ref, b_ref, idx_ref, gs_ref, rank_ref, cnt_ref, carry_ref, *, tm):
    i = pl.program_id(0)

    @pl.when(i == 0)
    def _():
        carry_ref[...] = jnp.zeros_like(carry_ref)

    logits = jnp.dot(u_ref[...], w_ref[...], precision=HI, preferred_element_type=F32)
    scores = _sigmoid(logits)
    cur = scores + b_ref[...]
    lane = lax.broadcasted_iota(jnp.int32, (tm, N_EXPERTS), 1).astype(F32)
    lane_out = lax.broadcasted_iota(jnp.int32, (tm, LANES), 1)
    picks = []
    sel_f = jnp.zeros((tm, N_EXPERTS), F32)
    idx_out = jnp.zeros((tm, LANES), F32)
    for k in range(TOP_K):
        best = jnp.max(cur, axis=-1, keepdims=True)
        ik = jnp.min(jnp.where(cur == best, lane, float(N_EXPERTS)), axis=-1, keepdims=True)
        onehot = lane == ik
        picks.append(onehot)
        sel_f = jnp.where(onehot, 1.0, sel_f)
        cur = jnp.where(onehot, -jnp.inf, cur)
        idx_out = jnp.where(lane_out == k, ik, idx_out)
    gsel = scores * sel_f
    gsel = gsel / jnp.sum(gsel, axis=-1, keepdims=True) * ROUTED_SCALE
    ri = lax.broadcasted_iota(jnp.int32, (tm, tm), 0)
    ci = lax.broadcasted_iota(jnp.int32, (tm, tm), 1)
    before = jnp.where(ri > ci, 1.0, 0.0).astype(BF16)
    rank =jnp.dot(before, sel_f.astype(BF16), preferred_element_type=F32) + carry_ref[...]
    carry_ref[...] = carry_ref[...] + jnp.sum(sel_f, axis=0, keepdims=True)
    cnt_ref[...] = carry_ref[...].astype(jnp.int32)
    gs_out = jnp.zeros((tm, LANES), F32)
    rank_out = jnp.zeros((tm, LANES), F32)
    for k in range(TOP_K):
        gk = jnp.sum(jnp.where(picks[k], gsel, 0.0), axis=-1, keepdims=True)
        rk = jnp.sum(jnp.where(picks[k], rank, 0.0), axis=-1, keepdims=True)
        gs_out = jnp.where(lane_out == k, gk, gs_out)
        rank_out = jnp.where(lane_out == k, rk, rank_out)
    idx_ref[...] = idx_out.astype(jnp.int32)
    gs_ref[...] = gs_out
    rank_ref[...] = rank_out.astype(jnp.int32)


def router(u, w_router, b_router, tm=256):
    n, d = u.shape
    tm = min(tm, n)
    tile = pl.BlockSpec((tm, LANES), lambda i: (i, 0))
    return pl.pallas_call(
        functools.partial(_router_kernel, tm=tm),
        grid=(n // tm,),
        in_specs=[pl.BlockSpec((tm, d), lambda i: (i, 0)),
                  pl.BlockSpec((d, N_EXPERTS), lambda i: (0, 0)),
                  pl.BlockSpec((1, N_EXPERTS), lambda i: (0, 0))],
        out_specs=[tile, tile, tile, pl.BlockSpec((1, N_EXPERTS), lambda i: (0, 0))],
        out_shape=[jax.ShapeDtypeStruct((n, LANES), jnp.int32),
                   jax.ShapeDtypeStruct((n, LANES), F32),
                   jax.ShapeDtypeStruct((n, LANES), jnp.int32),
                   jax.ShapeDtypeStruct((1, N_EXPERTS), jnp.int32)],
        scratch_shapes=[pltpu.VMEM((1, N_EXPERTS), F32)],
        compiler_params=_params(("arbitrary",)),
        name="router",
    )(u, w_router.astype(F32), b_router.reshape(1, -1).astype(F32))


def _dispatch_kernel(dest_ref, tail_ref, nused_ref, u_ref, h_ref, wg_ref, wu_ref, wd_ref,
                     xs_ref, hs_ref, zbuf, zsem, sem, *, td, bm):
    step = pl.program_id(0)
    n_blocks = xs_ref.shape[0] // bm

    @pl.when(step == 0)
    def _():
        zbuf[...] = jnp.zeros_like(zbuf)

        def zero_block(blk):
            return pltpu.make_async_copy(zbuf, xs_ref.at[pl.ds(blk * bm, bm)], zsem)

        def tails(e, carry):
            @pl.when(tail_ref[e] >= 0)
            def _():
                zero_block(tail_ref[e]).start()
            return carry

        def tails_wait(e, carry):
            @pl.when(tail_ref[e] >= 0)
            def _():
                zero_block(tail_ref[e]).wait()
            return carry

        def unused(blk, carry):
            zero_block(blk).start()
            return carry

        def unused_wait(blk, carry):
            zero_block(blk).wait()
            return carry

        lax.fori_loop(0, N_EXPERTS, tails, 0)
        lax.fori_loop(nused_ref[0], n_blocks, unused, 0)
        lax.fori_loop(0, N_EXPERTS, tails_wait, 0)
        lax.fori_loop(nused_ref[0], n_blocks, unused_wait, 0)

    base = step * (td * TOP_K)

    for i in range(td):
        src = u_ref.at[pl.ds(i, 1)]
        for k in range(TOP_K):
            pltpu.make_async_copy(src, xs_ref.at[pl.ds(dest_ref[base + i * TOP_K + k], 1)],
                                  sem).start(priority=k % 2)

    x = u_ref[...].astype(BF16)
    hg = jnp.dot(x, wg_ref[...], preferred_element_type=F32)
    hu = jnp.dot(x, wu_ref[...], preferred_element_type=F32)
    hb = (hg * _sigmoid(hg) * hu).astype(BF16)
    hs_ref[...] = h_ref[...] + jnp.dot(hb, wd_ref[...], preferred_element_type=F32)

    tile = xs_ref.at[pl.ds(0, td * TOP_K)]
    pltpu.make_async_copy(tile, tile, sem).wait()


def dispatch_and_shared(u, h, dest, tail_blk, n_used, cap, bm, wg, wu, wd, td=128):
    n, d = u.shape
    f = wg.shape[1]
    td = min(td, n)
    tile = pl.BlockSpec((td, d), lambda i, *_: (i, 0))
    grid_spec = pltpu.PrefetchScalarGridSpec(
        num_scalar_prefetch=3,
        grid=(n // td,),
        in_specs=[tile, tile,
                  pl.BlockSpec((d, f), lambda i, *_: (0, 0)),
                  pl.BlockSpec((d, f), lambda i, *_: (0, 0)),
                  pl.BlockSpec((f, d), lambda i, *_: (0, 0))],
        out_specs=[pl.BlockSpec(memory_space=pl.ANY), tile],
        scratch_shapes=[pltpu.VMEM((bm, d), u.dtype), pltpu.SemaphoreType.DMA(()),
                        pltpu.SemaphoreType.DMA(())],
    )
    return pl.pallas_call(
        functools.partial(_dispatch_kernel, td=td, bm=bm),
        grid_spec=grid_spec,
        out_shape=[jax.ShapeDtypeStruct((cap, d), u.dtype), jax.ShapeDtypeStruct((n, d), F32)],
        compiler_params=_params(("arbitrary",)),
        name="moe_dispatch",
    )(dest.reshape(-1), tail_blk, n_used, u, h, wg, wu, wd)


def _expert_kernel(be_ref, nused_ref, x_ref, wg_ref, wu_ref, wd_ref, y_ref, wg_bf, wu_bf, wd_bf):
    i = pl.program_id(0)

    @pl.when(jnp.logical_or(i == 0, be_ref[i] != be_ref[jnp.maximum(i - 1, 0)]))
    def _():
        wg_bf[...] = wg_ref[...].astype(BF16)
        wu_bf[...] = wu_ref[...].astype(BF16)
        wd_bf[...] = wd_ref[...].astype(BF16)

    @pl.when(i < nused_ref[0])
    def _():
        x = x_ref[...].astype(BF16)
        hg = jnp.dot(x, wg_bf[...], preferred_element_type=F32)
        hu = jnp.dot(x, wu_bf[...], preferred_element_type=F32)
        hb = (hg * _sigmoid(hg) * hu).astype(BF16)
        y_ref[...] = jnp.dot(hb, wd_bf[...], preferred_element_type=F32)

    @pl.when(i >= nused_ref[0])
    def _():
        y_ref[...] = jnp.zeros_like(y_ref)


def expert_ffn(xs, block_e, n_used, wg, wu, wd, bm):
    cap, d = xs.shape
    f = wg.shape[2]
    row = lambda i, be, nu: (jnp.maximum(jnp.minimum(i, nu[0] - 1), 0), 0)
    grid_spec = pltpu.PrefetchScalarGridSpec(
        num_scalar_prefetch=2,
        grid=(cap // bm,),
        in_specs=[pl.BlockSpec((bm, d), row),
                  pl.BlockSpec((None, d, f), lambda i, be, nu: (be[i], 0, 0)),
                  pl.BlockSpec((None, d, f), lambda i, be, nu: (be[i], 0, 0)),
                  pl.BlockSpec((None, f, d), lambda i, be, nu: (be[i], 0, 0))],
        out_specs=pl.BlockSpec((bm, d), lambda i, be, nu: (i, 0)),
        scratch_shapes=[pltpu.VMEM((d, f), BF16), pltpu.VMEM((d, f), BF16), pltpu.VMEM((f, d), BF16)],
    )
    return pl.pallas_call(
        _expert_kernel,
        grid_spec=grid_spec,
        out_shape=jax.ShapeDtypeStruct((cap, d), F32),
        compiler_params=_params(("arbitrary",)),
        name="moe_experts",
    )(block_e, n_used, xs, wg, wu, wd)


def _combine_kernel(dest_ref, gs_ref, hs_ref, g_ref, y_ref, h_out, u_out, buf_a, buf_b, sems, *, tt):
    step = pl.program_id(0)
    last = pl.num_programs(0) - 1

    def row_copy(tile, r, k, buf, sem):
        return pltpu.make_async_copy(y_ref.at[pl.ds(dest_ref[(tile * tt + r) * TOP_K + k], 1)],
                                     buf.at[k, pl.ds(r, 1)], sem)

    def wait_rows(buf, sem):
        pltpu.make_async_copy(buf, buf, sem).wait()

    @pl.when(step == 0)
    def _():
        def row(r, carry):
            for k in range(TOP_K):
                row_copy(0, r, k, buf_a, sems.at[0]).start()
            return carry

        lax.fori_loop(0, tt, row, 0)

    def phase(cur, cur_sem, nxt, nxt_sem):
        wait_rows(cur, cur_sem)
        nxt_tile = jnp.minimum(step + 1, last)
        for r in range(tt):
            for k in range(TOP_K):
                row_copy(nxt_tile, r, k, nxt, nxt_sem).start(priority=k % 2)
        gs = gs_ref[...]
        h = hs_ref[...]
        for k in range(TOP_K):
            h = h + gs[:, k:k + 1] * cur[k]
        h_out[...] = h
        ms = jnp.mean(h * h, axis=-1, keepdims=True)
        u_out[...] = (h * lax.rsqrt(ms + RMS_EPS) * g_ref[...]).astype(u_out.dtype)

        @pl.when(step == last)
        def _():
            wait_rows(nxt, nxt_sem)

    @pl.when(step % 2 == 0)
    def _():
        phase(buf_a, sems.at[0], buf_b, sems.at[1])

    @pl.when(step % 2 == 1)
    def _():
        phase(buf_b, sems.at[1], buf_a, sems.at[0])


def combine(y, dest, gsel, hs, g_next, tt=64):
    n, d = hs.shape
    tt = min(tt, n)
    tile = pl.BlockSpec((tt, d), lambda i, s: (i, 0))
    grid_spec = pltpu.PrefetchScalarGridSpec(
        num_scalar_prefetch=1,
        grid=(n // tt,),
        in_specs=[pl.BlockSpec((tt, LANES), lambda i, s: (i, 0)),
                  tile,
                  pl.BlockSpec((1, d), lambda i, s: (0, 0)),
                  pl.BlockSpec(memory_space=pl.ANY)],
        out_specs=[tile, tile],
        scratch_shapes=[pltpu.VMEM((TOP_K, tt, d), F32), pltpu.VMEM((TOP_K, tt, d), F32),
                        pltpu.SemaphoreType.DMA((2,))],
    )
    return pl.pallas_call(
        functools.partial(_combine_kernel, tt=tt),
        grid_spec=grid_spec,
        out_shape=[jax.ShapeDtypeStruct((n, d), F32), jax.ShapeDtypeStruct((n, d), BF16)],
        compiler_params=_params(("arbitrary",)),
        name="moe_combine",
    )(dest.reshape(-1), gsel, hs, g_next.reshape(1, -1).astype(F32), y)


def moe_ffn(h, g_norm, w_router, b_router, w_e_gate, w_e_up, w_e_down, w_s_gate, w_s_up, w_s_down,
            g_next, bm=EXPERT_BLOCK):
    n, d = h.shape
    u = rmsnorm(h, g_norm, F32)
    idx128, gs128, rank128, counts = router(u, w_router, b_router)
    idx = idx128[:, :TOP_K]
    rank = rank128[:, :TOP_K]
    counts = counts.reshape(-1)
    n_blk = (counts + bm - 1) // bm
    blk_end = jnp.cumsum(n_blk)
    pstart = (blk_end - n_blk) * bm
    experts = jnp.arange(N_EXPERTS, dtype=jnp.int32)
    dest = rank + jnp.sum(jnp.where(idx[:, :, None] == experts, pstart, 0), axis=-1)
    dest = dest.astype(jnp.int32)
    cap = (n * TOP_K // bm + N_EXPERTS) * bm
    blocks = jnp.arange(cap // bm, dtype=jnp.int32)
    block_e = jnp.minimum(jnp.sum(blk_end[None, :] <= blocks[:, None], axis=1), N_EXPERTS - 1)
    block_e = block_e.astype(jnp.int32)
    n_used = blk_end[-1:].astype(jnp.int32)
    tail_blk = jnp.where(n_blk > 0, blk_end - 1, -1).astype(jnp.int32)
    xs, hs = dispatch_and_shared(u, h, dest, tail_blk, n_used, cap, bm, w_s_gate.astype(BF16),
                                 w_s_up.astype(BF16), w_s_down.astype(BF16))
    y = expert_ffn(xs, block_e, n_used, w_e_gate.astype(F32), w_e_up.astype(F32), w_e_down.astype(F32), bm)
    return combine(y, dest, gs128, hs, g_next)


def _ple_kernel(u_ref, p_ref, h_ref, wg_ref, wp_ref, gf_ref, o_ref):
    gate = _sigmoid(jnp.dot(u_ref[...], wg_ref[...], preferred_element_type=F32))
    proj = jnp.dot(p_ref[...].astype(BF16), wp_ref[...], preferred_element_type=F32)
    h = h_ref[...] + gate * proj
    ms = jnp.mean(h * h, axis=-1, keepdims=True)
    o_ref[...] = h * lax.rsqrt(ms + RMS_EPS) * gf_ref[...]


def ple_final(u, p, h, w_gate, w_proj, g_final, tm=256):
    n, d = h.shape
    pd = p.shape[1]
    tm = min(tm, n)
    tile = pl.BlockSpec((tm, d), lambda i: (i, 0))
    return pl.pallas_call(
        _ple_kernel,
        grid=(n // tm,),
        in_specs=[tile, pl.BlockSpec((tm, pd), lambda i: (i, 0)), tile,
                  pl.BlockSpec((d, d), lambda i: (0, 0)),
                  pl.BlockSpec((pd, d), lambda i: (0, 0)),
                  pl.BlockSpec((1, d), lambda i: (0, 0))],
        out_specs=tile,
        out_shape=jax.ShapeDtypeStruct((n, d), F32),
        compiler_params=_params(("parallel",)),
        name="ple_final",
    )(u, p, h, w_gate, w_proj, g_final.reshape(1, -1).astype(F32))


def token_mixing(x2, bsz, seq, norm_g, w_in, w_branch_gate, lq1, lk1, lq2, lk2, subln_g, shift_prev,
                 shift_next, w0, w_decay_up, a0, w_iclr_up, w_gate_up, k_k, k_a, r_k, ln_x_g, ln_x_b,
                 w_br_attn, w_br_rwkv, w_out, lambda_init):
    d = x2.shape[1]
    u = rmsnorm(x2, norm_g, BF16)
    w_attn = w_in[:, :A_COLS].astype(BF16)
    w_rwkv = jnp.pad(w_in[:, A_COLS:], ((0, 0), (0, R_COLS_PAD - R_COLS))).astype(BF16)
    za = matmul(u, w_attn, out_dtype=BF16, epilogue=functools.partial(_scale_q_epilogue, tn=512),
                tm=2048, tn=512, name="proj_attn")
    zr = matmul(u, w_rwkv, out_dtype=F32, tm=2048, tn=512, name="proj_rwkv")
    ya = diff_attention(za.reshape(bsz, seq, A_COLS), lq1, lk1, lq2, lk2, subln_g, lambda_init)
    r, kk, v, lw, b, kt, bonus, g = rwkv_prep(
        zr.reshape(bsz, seq, R_COLS_PAD), shift_prev, shift_next, w0, w_decay_up, a0, w_iclr_up,
        w_gate_up, k_k, k_a, r_k)
    y0, y1 = rwkv_scan(r, kk, v, lw, b, kt)
    yr = rwkv_post(y0, y1, bonus, g, ln_x_g, ln_x_b)
    merged = merge_branches(u, ya.reshape(-1, A_WIDTH), yr.reshape(-1, R_WIDTH),
                            w_branch_gate.astype(BF16), w_br_attn.astype(BF16), w_br_rwkv.astype(BF16))
    return matmul(merged, w_out.astype(BF16), out_dtype=F32, epilogue=_residual_add, extras=(x2,),
                  tm=2048, tn=512, name="out_proj")


def kernel(x, p, norm_mix_g, w_in, w_branch_gate, lambda_q1, lambda_k1, lambda_q2, lambda_k2, subln_g, shift_prev, shift_next, w0, w_decay_up, a0, w_iclr_up, w_gate_up, k_k, k_a, r_k, ln_x_g, ln_x_b, w_br_attn, w_br_rwkv, w_out, norm_ffn_g, w_router, b_router, w_e_gate, w_e_up, w_e_down, w_s_gate, w_s_up, w_s_down, norm_ple_g, w_ple_gate, w_ple_proj, norm_final_g):
    bsz, seq, d = x.shape
    depth = w_in.shape[0]
    assert depth == 1, "the final norm is fused into the last layer's embedding kernel"
    h = x.reshape(bsz * seq, d)
    i = 0
    lambda_init = 0.8 - 0.6 * math.exp(-0.3 * i)
    h = token_mixing(h, bsz, seq, norm_mix_g[i], w_in[i], w_branch_gate[i], lambda_q1[i], lambda_k1[i],
                     lambda_q2[i], lambda_k2[i], subln_g[i], shift_prev[i], shift_next[i], w0[i],
                     w_decay_up[i], a0[i], w_iclr_up[i], w_gate_up[i], k_k[i], k_a[i], r_k[i],
                     ln_x_g[i], ln_x_b[i], w_br_attn[i], w_br_rwkv[i], w_out[i], lambda_init)
    h, u = moe_ffn(h, norm_ffn_g[i], w_router[i], b_router[i], w_e_gate[i], w_e_up[i], w_e_down[i],
                   w_s_gate[i], w_s_up[i], w_s_down[i], norm_ple_g[i])
    out = ple_final(u, p[i].reshape(bsz * seq, -1), h, w_ple_gate[i].astype(BF16),
                    w_ple_proj[i].astype(BF16), norm_final_g)
    return out.reshape(bsz, seq, d)
```
